```python
import jax
import jax.numpy as jnp
from jax import lax
import numpy as np


D_MODEL = 2048
BATCH = 4
SEQ = 4096
DEPTH = 1

GRID_W = 64
CTX_LEN = 256
RET_HEADS = 8
RET_DK = 128
RET_DV = 256
RET_CHUNK = 128
ROPE_AXIS_DIM = RET_DK // 2
ROPE_BASE = 10000.0
NA_HEADS = 16
NA_DH = 128
NA_KR_MAX = 8
NA_KC = 16
NA_QCB = NA_KC
NA_KCB = 2 * NA_KC
N_EXPERTS = 32
TOP_K = 4
D_FF = 2048
MOE_BLOCK = 128
SWIGLU_ALPHA = 1.702
SWIGLU_LIMIT = 7.0
NORM_EPS = 1e-6
GN_EPS = 1e-5
NEG_INF = -1e30
RET_QK_W = RET_HEADS * RET_DK
RET_V_W = RET_HEADS * RET_DV
NA_W = NA_HEADS * NA_DH
IN_SPLITS = (RET_QK_W, RET_QK_W, RET_V_W, RET_V_W, NA_W, NA_W, NA_W, D_MODEL, D_MODEL)
IN_W = sum(IN_SPLITS)

kernel_name = 'hybrid_retention_natten_moe_dit'


def rmsnorm(x, g):
    xf = x.astype(jnp.float32)
    y = xf * lax.rsqrt(jnp.mean(xf * xf, axis=-1, keepdims=True) + NORM_EPS)
    return (y * g.astype(jnp.float32)).astype(x.dtype)


def modulate(h, shift, scale):
    return h * (1.0 + scale) + shift


def split_heads(t, n_heads):
    b, n, w = t.shape
    return t.reshape(b, n, n_heads, w // n_heads).transpose(0, 2, 1, 3)


def flip(t):
    return t[:, :, ::-1]


def axial_rope_tables(n):
    t = jnp.arange(n)
    row = (t // GRID_W).astype(jnp.float32)
    col = (t % GRID_W).astype(jnp.float32)
    inv = ROPE_BASE ** (-jnp.arange(0, ROPE_AXIS_DIM, 2, dtype=jnp.float32) / ROPE_AXIS_DIM)
    ang = jnp.stack([row[:, None] * inv, col[:, None] * inv], axis=1)
    return jnp.cos(ang), jnp.sin(ang)


def apply_axial_rope(x, cos, sin):
    b, n, h, d = x.shape
    xf = x.astype(jnp.float32).reshape(b, n, h, 2, 2, ROPE_AXIS_DIM // 2)
    x1, x2 = xf[..., 0, :], xf[..., 1, :]
    cs = cos[None, :, None]
    sn = sin[None, :, None]
    out = jnp.stack([x1 * cs - x2 * sn, x2 * cs + x1 * sn], axis=-2)
    return out.reshape(b, n, h, d).astype(x.dtype)


def chunk_states(k, v, lg, r0):
    b, h, n, dk = k.shape
    dv = v.shape[-1]
    nc = n // RET_CHUNK
    kc = k.reshape(b, h, nc, RET_CHUNK, dk)
    vc = v.reshape(b, h, nc, RET_CHUNK, dv)
    pos = jnp.arange(RET_CHUNK, dtype=jnp.float32)
    zeta = jnp.exp((RET_CHUNK - 1 - pos)[None, :] * lg[:, None])
    u = jnp.einsum('bhncd,bhnce->bhnde', kc * zeta[None, :, None, :, None], vc)
    g_chunk = jnp.exp(RET_CHUNK * lg)[None, :, None, None]

    def step(r, u_i):
        return g_chunk * r + u_i, r

    r_final, r_prev = lax.scan(step, r0, jnp.moveaxis(u, 2, 0))
    return jnp.moveaxis(r_prev, 0, 2), r_final


def retention_out(q, k, v, lg, r_prev):
    b, h, n, dk = q.shape
    dv = v.shape[-1]
    nc = n // RET_CHUNK
    qc = q.reshape(b, h, nc, RET_CHUNK, dk)
    kc = k.reshape(b, h, nc, RET_CHUNK, dk)
    vc = v.reshape(b, h, nc, RET_CHUNK, dv)
    pos = jnp.arange(RET_CHUNK, dtype=jnp.float32)
    diff = pos[:, None] - pos[None, :]
    dmask = jnp.where(diff >= 0, jnp.exp(jnp.maximum(diff, 0.0)[None] * lg[:, None, None]), 0.0)
    s = jnp.einsum('bhncd,bhnmd->bhncm', qc, kc) * dmask[None, :, None]
    inner = jnp.einsum('bhncm,bhnme->bhnce', s, vc)
    xi = jnp.exp((pos + 1.0)[None, :] * lg[:, None])
    cross = jnp.einsum('bhncd,bhnde->bhnce', qc * xi[None, :, None, :, None], r_prev)
    return (inner + cross).reshape(b, h, n, dv)


def retention_head_out(o, g, w_pa):
    b, h, n, dv = o.shape
    of = o.astype(jnp.float32)
    mu = jnp.mean(of, axis=-1, keepdims=True)
    var = jnp.mean(jnp.square(of - mu), axis=-1, keepdims=True)
    on = ((of - mu) * lax.rsqrt(var + GN_EPS)).transpose(0, 2, 1, 3).reshape(b, n, h * dv).astype(g.dtype)
    return (jax.nn.silu(g) * on) @ w_pa


def bidirectional_retention(q, k, v, lg_f, lg_b, r_init_f, r_init_b):
    prev_f, _ = chunk_states(k, v, lg_f, r_init_f)
    prev_b, _ = chunk_states(flip(k), flip(v), lg_b, r_init_b)
    return retention_out(q, k, v, lg_f, prev_f) + flip(retention_out(flip(q), flip(k), flip(v), lg_b, prev_b))


def neighbourhood_attention(q, k, v, kc, vc, rpb):
    b, h, s, d = q.shape
    rows = s // GRID_W
    kr = min(NA_KR_MAX, rows)
    scale = d ** -0.5
    qg = q.reshape(b, h, rows, GRID_W, d)
    kg = k.reshape(b, h, rows, GRID_W, d)
    vg = v.reshape(b, h, rows, GRID_W, d)
    n_cb = GRID_W // NA_QCB
    qcols = jnp.arange(GRID_W).reshape(n_cb, NA_QCB)
    kstart = jnp.clip(jnp.arange(n_cb) * NA_QCB - NA_KC // 2, 0, GRID_W - NA_KCB)
    kcols = kstart[:, None] + jnp.arange(NA_KCB)
    cs = jnp.clip(qcols - NA_KC // 2, 0, GRID_W - NA_KC)
    col_ok = (kcols[:, None, :] >= cs[:, :, None]) & (kcols[:, None, :] < cs[:, :, None] + NA_KC)
    dc_idx = jnp.clip(kcols[:, None, :] - qcols[:, :, None], -(NA_KC - 1), NA_KC - 1) + NA_KC - 1
    mask = jnp.broadcast_to(col_ok[:, :, None, :], (n_cb, NA_QCB, kr, NA_KCB)).reshape(n_cb, NA_QCB, kr * NA_KCB)
    rpb_f = rpb.astype(jnp.float32)
    n_loc = kr * NA_KCB

    def row_block(r):
        rs = jnp.clip(r - kr // 2, 0, rows - kr)
        k_rows = lax.dynamic_slice_in_dim(kg, rs, kr, axis=2)
        v_rows = lax.dynamic_slice_in_dim(vg, rs, kr, axis=2)
        k_blk = k_rows[:, :, :, kcols].transpose(0, 1, 3, 2, 4, 5).reshape(b, h, n_cb, n_loc, d)
        v_blk = v_rows[:, :, :, kcols].transpose(0, 1, 3, 2, 4, 5).reshape(b, h, n_cb, n_loc, d)
        q_blk = lax.dynamic_index_in_dim(qg, r, axis=2, keepdims=False).reshape(b, h, n_cb, NA_QCB, d)
        dr_idx = rs + jnp.arange(kr) - r + NA_KR_MAX - 1
        bias = rpb_f[:, dr_idx][:, :, dc_idx]
        bias = bias.transpose(0, 2, 3, 1, 4).reshape(h, n_cb, NA_QCB, n_loc)
        s_loc = jnp.einsum('bhjqd,bhjkd->bhjqk', q_blk, k_blk).astype(jnp.float32) * scale + bias
        s_loc = jnp.where(mask, s_loc, NEG_INF)
        s_ctx = jnp.einsum('bhjqd,bhkd->bhjqk', q_blk, kc).astype(jnp.float32) * scale
        p = jax.nn.softmax(jnp.concatenate([s_loc, s_ctx], axis=-1), axis=-1).astype(v.dtype)
        o = jnp.einsum('bhjqk,bhjkd->bhjqd', p[..., :n_loc], v_blk) + jnp.einsum('bhjqk,bhkd->bhjqd', p[..., n_loc:], vc)
        return o.reshape(b, h, GRID_W, d)

    out = lax.map(row_block, jnp.arange(rows))
    return out.transpose(1, 0, 3, 2, 4).reshape(b, s, h * d)


def context_attention(q, k, v):
    b, h, n, d = q.shape
    s = jnp.einsum('bhqd,bhkd->bhqk', q, k).astype(jnp.float32) * d ** -0.5
    p = jax.nn.softmax(s, axis=-1).astype(v.dtype)
    return jnp.einsum('bhqk,bhkd->bhqd', p, v).transpose(0, 2, 1, 3).reshape(b, n, h * d)


def token_mixer(h, hc, w_in, w_pa, w_pb, w_o, dec_f, dec_b, rpb, with_ctx_out):
    b, s, _ = h.shape
    cuts = [int(i) for i in np.cumsum(IN_SPLITS)[:-1]]
    q_r, k_r, v_r, g_r, q_n, k_n, v_n, g_a, g_b = jnp.split(h @ w_in, cuts, axis=-1)
    qc_r, kc_r, vc_r, gc_r, qc_n, kc_n, vc_n, gc_a, gc_b = jnp.split(hc @ w_in, cuts, axis=-1)

    cos, sin = axial_rope_tables(s)
    q = apply_axial_rope(q_r.reshape(b, s, RET_HEADS, RET_DK), cos, sin).transpose(0, 2, 1, 3)
    k = apply_axial_rope(k_r.reshape(b, s, RET_HEADS, RET_DK), cos, sin).transpose(0, 2, 1, 3) * RET_DK ** -0.5
    v = split_heads(v_r, RET_HEADS)
    kc = split_heads(kc_r, RET_HEADS) * RET_DK ** -0.5
    vc = split_heads(vc_r, RET_HEADS)
    lg_f = jax.nn.log_sigmoid(dec_f.astype(jnp.float32))
    lg_b = jax.nn.log_sigmoid(dec_b.astype(jnp.float32))
    r0 = jnp.zeros((b, RET_HEADS, RET_DK, RET_DV), jnp.float32)
    _, r_cf = chunk_states(kc, vc, lg_f, r0)
    _, r_cb = chunk_states(flip(kc), flip(vc), lg_b, r0)
    ret = retention_head_out(bidirectional_retention(q, k, v, lg_f, lg_b, r_cf, r_cb), g_r, w_pa)

    kc_na = split_heads(kc_n, NA_HEADS)
    vc_na = split_heads(vc_n, NA_HEADS)
    na = neighbourhood_attention(split_heads(q_n, NA_HEADS), split_heads(k_n, NA_HEADS), split_heads(v_n, NA_HEADS), kc_na, vc_na, rpb) @ w_pb

    y = (jax.nn.sigmoid(g_a) * ret + jax.nn.sigmoid(g_b) * na) @ w_o
    if not with_ctx_out:
        return y, None

    qc = split_heads(qc_r, RET_HEADS)
    ret_c = retention_head_out(bidirectional_retention(qc, kc, vc, lg_f, lg_b, r0, r0), gc_r, w_pa)
    na_c = context_attention(split_heads(qc_n, NA_HEADS), kc_na, vc_na) @ w_pb
    yc = (jax.nn.sigmoid(gc_a) * ret_c + jax.nn.sigmoid(gc_b) * na_c) @ w_o
    return y, yc


def clamped_swiglu(gate, up):
    gate = jnp.minimum(gate, SWIGLU_LIMIT)
    up = jnp.clip(up, -SWIGLU_LIMIT, SWIGLU_LIMIT)
    return gate * jax.nn.sigmoid(SWIGLU_ALPHA * gate) * (up + 1.0)


def moe_ffn(h, w_router, b_router, w_gate, b_gate, w_up, b_up, w_down, b_down):
    shape = h.shape
    dm = shape[-1]
    t = h.reshape(-1, dm)
    n_tok = t.shape[0]
    logits = (t @ w_router + b_router).astype(jnp.float32)
    top_v, top_i = lax.top_k(logits, TOP_K)
    top_w = jax.nn.softmax(top_v, axis=-1)
    n_asg = n_tok * TOP_K
    n_blocks = -(-(n_asg + N_EXPERTS * (MOE_BLOCK - 1)) // MOE_BLOCK)
    e_flat = top_i.reshape(-1)
    tok_flat = jnp.arange(n_asg, dtype=jnp.int32) // TOP_K
    w_flat = top_w.reshape(-1)
    order = jnp.argsort(e_flat)
    e_sorted = e_flat[order]
    counts = jnp.zeros((N_EXPERTS,), jnp.int32).at[e_flat].add(1)
    padded = (counts + MOE_BLOCK - 1) // MOE_BLOCK * MOE_BLOCK
    start = jnp.cumsum(counts) - counts
    pend = jnp.cumsum(padded)
    pstart = pend - padded
    dest = pstart[e_sorted] + jnp.arange(n_asg, dtype=jnp.int32) - start[e_sorted]
    slot_tok = jnp.full((n_blocks * MOE_BLOCK,), n_tok, jnp.int32).at[dest].set(tok_flat[order])
    slot_w = jnp.zeros((n_blocks * MOE_BLOCK,), jnp.float32).at[dest].set(w_flat[order])
    block_e = jnp.minimum(jnp.searchsorted(pend, jnp.arange(n_blocks, dtype=jnp.int32) * MOE_BLOCK, side='right'), N_EXPERTS - 1)
    t_pad = jnp.concatenate([t, jnp.zeros((1, dm), t.dtype)], axis=0)
    xb = t_pad[slot_tok].reshape(n_blocks, MOE_BLOCK, dm)

    def expert_block(args):
        xblk, e = args
        a = clamped_swiglu(xblk @ w_gate[e] + b_gate[e], xblk @ w_up[e] + b_up[e])
        return a @ w_down[e] + b_down[e]

    yb = lax.map(expert_block, (xb, block_e))
    y = jnp.zeros((n_tok + 1, dm), jnp.float32).at[slot_tok].add(yb.reshape(-1, dm).astype(jnp.float32) * slot_w[:, None])
    return y[:n_tok].reshape(shape).astype(h.dtype)


def setup_inputs(seed: int = 0) -> dict:
    key = jax.random.key(seed)
    ks = jax.random.split(key, 26)

    def nrm(k, shape, scale):
        return jax.random.normal(k, shape, jnp.float32) * scale

    base_logit = jnp.log(2.0 ** (5.0 + jnp.arange(RET_HEADS, dtype=jnp.float32)) - 1.0)
    return {
        'x': nrm(ks[0], (BATCH, SEQ, D_MODEL), 1.0),
        'c': nrm(ks[1], (BATCH, D_MODEL), 1.0),
        'ctx': nrm(ks[2], (BATCH, CTX_LEN, D_MODEL), 1.0),
        'c_ctx': nrm(ks[3], (D_MODEL,), 1.0),
        'ada_w': nrm(ks[4], (DEPTH, D_MODEL, 6 * D_MODEL), 0.5 * D_MODEL ** -0.5),
        'ada_b': nrm(ks[5], (DEPTH, 6 * D_MODEL), 0.02),
        'norm1_g': 1.0 + nrm(ks[6], (DEPTH, D_MODEL), 0.02),
        'norm2_g': 1.0 + nrm(ks[7], (DEPTH, D_MODEL), 0.02),
        'w_in': nrm(ks[8], (DEPTH, D_MODEL, IN_W), D_MODEL ** -0.5),
        'w_pa': nrm(ks[9], (DEPTH, RET_V_W, D_MODEL), RET_V_W ** -0.5),
        'w_pb': nrm(ks[10], (DEPTH, NA_W, D_MODEL), NA_W ** -0.5),
        'w_o': nrm(ks[11], (DEPTH, D_MODEL, D_MODEL), D_MODEL ** -0.5),
        'ret_decay_fwd': base_logit + nrm(ks[12], (DEPTH, RET_HEADS), 0.05),
        'ret_decay_bwd': base_logit + nrm(ks[13], (DEPTH, RET_HEADS), 0.05),
        'na_rpb': nrm(ks[14], (DEPTH, NA_HEADS, 2 * NA_KR_MAX - 1, 2 * NA_KC - 1), 0.1),
        'w_router': nrm(ks[15], (DEPTH, D_MODEL, N_EXPERTS), D_MODEL ** -0.5),
        'b_router': nrm(ks[16], (DEPTH, N_EXPERTS), 0.01),
        'w_gate': nrm(ks[17], (DEPTH, N_EXPERTS, D_MODEL, D_FF), D_MODEL ** -0.5),
        'b_gate': nrm(ks[18], (DEPTH, N_EXPERTS, D_FF), 0.01),
        'w_up': nrm(ks[19], (DEPTH, N_EXPERTS, D_MODEL, D_FF), D_MODEL ** -0.5),
        'b_up': nrm(ks[20], (DEPTH, N_EXPERTS, D_FF), 0.01),
        'w_down': nrm(ks[21], (DEPTH, N_EXPERTS, D_FF, D_MODEL), D_FF ** -0.5),
        'b_down': nrm(ks[22], (DEPTH, N_EXPERTS, D_MODEL), 0.01),
        'final_g': 1.0 + nrm(ks[23], (D_MODEL,), 0.02),
    }


def reference(x, c, ctx, c_ctx, ada_w, ada_b, norm1_g, norm2_g, w_in, w_pa, w_pb, w_o, ret_decay_fwd, ret_decay_bwd, na_rpb, w_router, b_router, w_gate, b_gate, w_up, b_up, w_down, b_down, final_g):
    xc = ctx
    for l in range(DEPTH):
        last = l == DEPTH - 1
        mod = jax.nn.silu(c) @ ada_w[l] + ada_b[l]
        mod_c = jax.nn.silu(c_ctx) @ ada_w[l] + ada_b[l]
        sh1, sc1, g1, sh2, sc2, g2 = jnp.split(mod[:, None, :], 6, axis=-1)
        shc1, scc1, gc1, shc2, scc2, gc2 = jnp.split(mod_c, 6, axis=-1)
        h = modulate(rmsnorm(x, norm1_g[l]), sh1, sc1)
        hc = modulate(rmsnorm(xc, norm1_g[l]), shc1, scc1)
        y, yc = token_mixer(h, hc, w_in[l], w_pa[l], w_pb[l], w_o[l], ret_decay_fwd[l], ret_decay_bwd[l], na_rpb[l], not last)
        x = x + g1 * y.astype(x.dtype)
        h2 = modulate(rmsnorm(x, norm2_g[l]), sh2, sc2)
        x = x + g2 * moe_ffn(h2, w_router[l], b_router[l], w_gate[l], b_gate[l], w_up[l], b_up[l], w_down[l], b_down[l])
        if not last:
            xc = xc + gc1 * yc.astype(xc.dtype)
            hc2 = modulate(rmsnorm(xc, norm2_g[l]), shc2, scc2)
            xc = xc + gc2 * moe_ffn(hc2, w_router[l], b_router[l], w_gate[l], b_gate[l], w_up[l], b_up[l], w_down[l], b_down[l])
    return rmsnorm(x, final_g)
```

```python
import functools

import jax
import jax.numpy as jnp
import numpy as np
from jax import lax
from jax.experimental import pallas as pl
from jax.experimental.pallas import tpu as pltpu

F32 = jnp.float32
BF16 = jnp.bfloat16

GRID_W = 64
RET_HEADS = 8
RET_DK = 128
RET_DV = 256
RET_CHUNK = 128
ROPE_AXIS_DIM = RET_DK // 2
ROPE_BASE = 10000.0
NA_HEADS = 16
NA_DH = 128
NA_KR = 8
NA_KC = 16
N_EXPERTS = 32
TOP_K = 4
SWIGLU_ALPHA = 1.702
SWIGLU_LIMIT = 7.0
NORM_EPS = 1e-6
GN_EPS = 1e-5
NEG_INF = -1e30

LANES = 128
MIB = 1024 * 1024

NA_QROWS = 4
NA_WROWS = NA_QROWS + NA_KR

IN_TM, IN_TN = 1024, 512
MERGE_TM, MERGE_TN = 1024, 512
OPROJ_TM = 256
ROUTE_TM = 512
DISPATCH_TM = 256
EXPERT_TM, EXPERT_TF = 512, 512
COMBINE_TM = 128


def _params(semantics, vmem_mib):
    return pltpu.CompilerParams(dimension_semantics=semantics, vmem_limit_bytes=vmem_mib * MIB)


def _sigmoid(x):
    return 1.0 / (1.0 + jnp.exp(-x))


def _ada_kernel(c_ref, w_ref, b_ref, o_ref):
    c = c_ref[...]
    s = c * _sigmoid(c)
    o_ref[...] = jnp.dot(s.astype(BF16), w_ref[...].astype(BF16), preferred_element_type=F32) + b_ref[...]


def _ada(c_rows, w, b):
    r, d = c_rows.shape
    n = w.shape[1]
    tn = 1024
    return pl.pallas_call(
        _ada_kernel,
        out_shape=jax.ShapeDtypeStruct((r, n), F32),
        grid=(n // tn,),
        in_specs=[
            pl.BlockSpec((r, d), lambda j: (0, 0)),
            pl.BlockSpec((d, tn), lambda j: (0, j)),
            pl.BlockSpec((1, tn), lambda j: (0, j)),
        ],
        out_specs=pl.BlockSpec((r, tn), lambda j: (0, j)),
        compiler_params=_params(("arbitrary",), 40),
        name="ada",
    )(c_rows, w, b)


def _inproj_kernel(cols_ref, x_ref, g_ref, sh_ref, sc_ref, w_ref, o_ref, h_scr):
    @pl.when(pl.program_id(2) == 0)
    def _():
        x = x_ref[0]
        ms = jnp.mean(x * x, axis=-1, keepdims=True)
        y = x * lax.rsqrt(ms + NORM_EPS) * g_ref[...]
        h_scr[...] = (y * (1.0 + sc_ref[0]) + sh_ref[0]).astype(BF16)

    o_ref[0] = jnp.dot(h_scr[...], w_ref[...], preferred_element_type=F32).astype(o_ref.dtype)


def _inproj(x, g, shift, scale, w, col_tiles, tm):
    b, n, d = x.shape
    tn = IN_TN
    nct = len(col_tiles)
    cols = jnp.asarray(np.asarray(col_tiles, np.int32))
    return pl.pallas_call(
        _inproj_kernel,
        out_shape=jax.ShapeDtypeStruct((b, n, nct * tn), BF16),
        grid_spec=pltpu.PrefetchScalarGridSpec(
            num_scalar_prefetch=1,
            grid=(b, n // tm, nct),
            in_specs=[
                pl.BlockSpec((1, tm, d), lambda bi, i, j, c: (bi, i, 0)),
                pl.BlockSpec((1, d), lambda bi, i, j, c: (0, 0)),
                pl.BlockSpec((1, 1, d), lambda bi, i, j, c: (bi, 0, 0)),
                pl.BlockSpec((1, 1, d), lambda bi, i, j, c: (bi, 0, 0)),
                pl.BlockSpec((d, tn), lambda bi, i, j, c: (0, c[j])),
            ],
            out_specs=pl.BlockSpec((1, tm, tn), lambda bi, i, j, c: (bi, i, j)),
            scratch_shapes=[pltpu.VMEM((tm, d), BF16)],
        ),
        compiler_params=_params(("arbitrary", "arbitrary", "arbitrary"), 48),
        name="inproj",
    )(cols, x, g, shift, scale, w)


def _swap_halves(x):
    lane = lax.broadcasted_iota(jnp.int32, x.shape, 1)
    return jnp.where(lane % 64 < 32, pltpu.roll(x, 96, 1), pltpu.roll(x, 32, 1))


def _ret_kernel(lgf_ref, lgb_ref, q_ref, k_ref, v_ref, g_ref, kc_ref, vc_ref, cos_ref, sin_ref,
                o_ref, ks_scr, rf_scr, rb_scr):
    h = pl.program_id(1)
    c = RET_CHUNK
    n = q_ref.shape[1]
    nc = n // c
    ncc = kc_ref.shape[1] // c
    lgf = lgf_ref[h]
    lgb = lgb_ref[h]
    k_scale = RET_DK ** -0.5

    pos_c = lax.broadcasted_iota(jnp.int32, (c, 1), 0).astype(F32)
    zeta_f = jnp.exp((c - 1.0 - pos_c) * lgf)
    zeta_b = jnp.exp(pos_c * lgb)
    xi_f = jnp.exp((pos_c + 1.0) * lgf)
    xi_b = jnp.exp((c - pos_c) * lgb)
    one = jnp.ones((1, 1), F32)
    gc_f = jnp.exp(one * (c * lgf))
    gc_b = jnp.exp(one * (c * lgb))
    ii = lax.broadcasted_iota(jnp.int32, (c, c), 0)
    jj = lax.broadcasted_iota(jnp.int32, (c, c), 1)
    diff = (ii - jj).astype(F32)
    dmask = (jnp.where(diff >= 0, jnp.exp(jnp.maximum(diff, 0.0) * lgf), 0.0)
             + jnp.where(diff <= 0, jnp.exp(jnp.maximum(-diff, 0.0) * lgb), 0.0))

    def ktv(k_bf, v_f32, zeta):
        return jnp.dot(k_bf.astype(F32).T.astype(BF16), (v_f32 * zeta).astype(BF16), preferred_element_type=F32)

    r_f = jnp.zeros((RET_DK, RET_DV), F32)
    for i in range(ncc):
        kc = (kc_ref[0, i * c:(i + 1) * c, :].astype(F32) * k_scale).astype(BF16)
        r_f = gc_f * r_f + ktv(kc, vc_ref[0, i * c:(i + 1) * c, :].astype(F32), zeta_f)
    r_b = jnp.zeros((RET_DK, RET_DV), F32)
    for i in reversed(range(ncc)):
        kc = (kc_ref[0, i * c:(i + 1) * c, :].astype(F32) * k_scale).astype(BF16)
        r_b = gc_b * r_b + ktv(kc, vc_ref[0, i * c:(i + 1) * c, :].astype(F32), zeta_b)

    def rope(x, rows):
        x = x.astype(F32)
        return x * cos_ref[rows, :] + _swap_halves(x) * sin_ref[rows, :]

    def fwd_state(i, r):
        rows = pl.ds(pl.multiple_of(i * c, c), c)
        kb = (rope(k_ref[0, rows, :], rows) * k_scale).astype(BF16)
        ks_scr[rows, :] = kb
        rf_scr[i] = r.astype(BF16)
        return gc_f * r + ktv(kb, v_ref[0, rows, :].astype(F32), zeta_f)

    lax.fori_loop(0, nc, fwd_state, r_f)

    def bwd_state(t, r):
        i = nc - 1 - t
        rows = pl.ds(pl.multiple_of(i * c, c), c)
        rb_scr[i] = r.astype(BF16)
        return gc_b * r + ktv(ks_scr[rows, :], v_ref[0, rows, :].astype(F32), zeta_b)

    lax.fori_loop(0, nc, bwd_state, r_b)

    def out_chunk(i, carry):
        rows = pl.ds(pl.multiple_of(i * c, c), c)
        q = rope(q_ref[0, rows, :], rows)
        kb = ks_scr[rows, :]
        vb = v_ref[0, rows, :]
        s = lax.dot_general(q.astype(BF16), kb, (((1,), (1,)), ((), ())), preferred_element_type=F32) * dmask
        o = jnp.dot(s.astype(BF16), vb, preferred_element_type=F32)
        o += jnp.dot((q * xi_f).astype(BF16), rf_scr[i], preferred_element_type=F32)
        o += jnp.dot((q * xi_b).astype(BF16), rb_scr[i], preferred_element_type=F32)
        mu = jnp.mean(o, axis=-1, keepdims=True)
        d = o - mu
        var = jnp.mean(d * d, axis=-1, keepdims=True)
        on = d * lax.rsqrt(var + GN_EPS)
        g = g_ref[0, rows, :].astype(F32)
        o_ref[0, rows, :] = (g * _sigmoid(g) * on).astype(o_ref.dtype)
        return carry

    lax.fori_loop(0, nc, out_chunk, 0)


def _retention(proj, projc, lgf, lgb, cos_t, sin_t):
    b, s, _ = proj.shape
    l = projc.shape[1]
    hq = RET_HEADS
    v_off = 2 * hq * RET_DK // RET_DV
    g_off = v_off + hq
    cv_off = hq * RET_DK // RET_DV
    smem = pl.BlockSpec(memory_space=pltpu.SMEM)
    return pl.pallas_call(
        _ret_kernel,
        out_shape=jax.ShapeDtypeStruct((b, s, hq * RET_DV), BF16),
        grid=(b, hq),
        in_specs=[
            smem, smem,
            pl.BlockSpec((1, s, RET_DK), lambda bi, h: (bi, 0, h)),
            pl.BlockSpec((1, s, RET_DK), lambda bi, h: (bi, 0, hq + h)),
            pl.BlockSpec((1, s, RET_DV), lambda bi, h: (bi, 0, v_off + h)),
            pl.BlockSpec((1, s, RET_DV), lambda bi, h: (bi, 0, g_off + h)),
            pl.BlockSpec((1, l, RET_DK), lambda bi, h: (bi, 0, h)),
            pl.BlockSpec((1, l, RET_DV), lambda bi, h: (bi, 0, cv_off + h)),
            pl.BlockSpec((s, RET_DK), lambda bi, h: (0, 0)),
            pl.BlockSpec((s, RET_DK), lambda bi, h: (0, 0)),
        ],
        out_specs=pl.BlockSpec((1, s, RET_DV), lambda bi, h: (bi, 0, h)),
        scratch_shapes=[
            pltpu.VMEM((s, RET_DK), BF16),
            pltpu.VMEM((s // RET_CHUNK, RET_DK, RET_DV), BF16),
            pltpu.VMEM((s // RET_CHUNK, RET_DK, RET_DV), BF16),
        ],
        compiler_params=_params(("arbitrary", "arbitrary"), 48),
        name="ret",
    )(lgf, lgb, proj, proj, proj, proj, projc, projc, cos_t, sin_t)


def _rope_tables(n):
    t = jnp.arange(n)
    row = (t // GRID_W).astype(F32)
    col = (t % GRID_W).astype(F32)
    inv = ROPE_BASE ** (-jnp.arange(0, ROPE_AXIS_DIM, 2, dtype=F32) / ROPE_AXIS_DIM)
    ar = row[:, None] * inv
    ac = col[:, None] * inv
    cos_t = jnp.concatenate([jnp.cos(ar), jnp.cos(ar), jnp.cos(ac), jnp.cos(ac)], axis=1)
    sin_t = jnp.concatenate([-jnp.sin(ar), jnp.sin(ar), -jnp.sin(ac), jnp.sin(ac)], axis=1)
    return cos_t, sin_t


def _natt_kernel(q_ref, k_ref, v_ref, kc_ref, vc_ref, bias_ref, o_ref):
    n = q_ref.shape[1]
    rows = n // GRID_W
    n_blk = rows // NA_QROWS
    nq = NA_QROWS * GRID_W
    nk = NA_WROWS * GRID_W
    scale = NA_DH ** -0.5
    kc = kc_ref[0]
    vc = vc_ref[0]
    nt = (((1,), (1,)), ((), ()))

    def block(blk, carry):
        ws = jnp.clip(blk * NA_QROWS - NA_KR // 2, 0, rows - NA_WROWS)
        variant = jnp.where(blk == 0, 0, jnp.where(blk == n_blk - 1, 2, 1))
        qrows = pl.ds(pl.multiple_of(blk * nq, nq), nq)
        krows = pl.ds(pl.multiple_of(ws * GRID_W, GRID_W), nk)
        q = q_ref[0, qrows, :]
        s_loc = lax.dot_general(q, k_ref[0, krows, :], nt, preferred_element_type=F32) * scale + bias_ref[0, variant]
        s_ctx = lax.dot_general(q, kc, nt, preferred_element_type=F32) * scale
        m = jnp.maximum(jnp.max(s_loc, axis=-1, keepdims=True), jnp.max(s_ctx, axis=-1, keepdims=True))
        p_loc = jnp.exp(s_loc - m)
        p_ctx = jnp.exp(s_ctx - m)
        denom = jnp.sum(p_loc, axis=-1, keepdims=True) + jnp.sum(p_ctx, axis=-1, keepdims=True)
        o = jnp.dot(p_loc.astype(BF16), v_ref[0, krows, :], preferred_element_type=F32)
        o += jnp.dot(p_ctx.astype(BF16), vc, preferred_element_type=F32)
        o_ref[0, qrows, :] = (o / denom).astype(o_ref.dtype)
        return carry

    lax.fori_loop(0, n_blk, block, 0)


def _natt_bias(rpb, rows):
    del rows
    w = GRID_W
    qc = np.arange(w)[:, None]
    kcol = np.arange(w)[None, :]
    cs = np.clip(qc - NA_KC // 2, 0, w - NA_KC)
    col_ok = (kcol >= cs) & (kcol < cs + NA_KC)
    dc_idx = np.clip(kcol - qc, -(NA_KC - 1), NA_KC - 1) + NA_KC - 1
    rq = np.arange(NA_QROWS)[:, None]
    wr = np.arange(NA_WROWS)[None, :]
    dr0 = np.where(wr < NA_KR, wr - rq + NA_KR - 1, -1)
    dr1 = np.where((wr >= rq) & (wr < rq + NA_KR), wr - rq + NA_KR // 2 - 1, -1)
    dr2 = np.where(wr >= NA_QROWS, wr - rq - 1, -1)
    dr = np.stack([dr0, dr1, dr2])
    t = jnp.where(col_ok[None, None], rpb.astype(F32)[:, :, dc_idx], NEG_INF)
    full = t[:, np.maximum(dr, 0)]
    full = jnp.where((dr >= 0)[None, :, :, :, None, None], full, NEG_INF)
    full = full.transpose(0, 1, 2, 4, 3, 5)
    return full.reshape(rpb.shape[0], 3, NA_QROWS * w, NA_WROWS * w)


def _natt(proj, projc, bias):
    b, s, _ = proj.shape
    l = projc.shape[1]
    nh = NA_HEADS
    q_off = (2 * RET_HEADS * RET_DK + 2 * RET_HEADS * RET_DV) // NA_DH
    k_off = q_off + nh
    v_off = k_off + nh
    ck_off = (RET_HEADS * RET_DK + RET_HEADS * RET_DV) // NA_DH
    cv_off = ck_off + nh
    nq = NA_QROWS * GRID_W
    nk = NA_WROWS * GRID_W
    return pl.pallas_call(
        _natt_kernel,
        out_shape=jax.ShapeDtypeStruct((b, s, nh * NA_DH), BF16),
        grid=(nh, b),
        in_specs=[
            pl.BlockSpec((1, s, NA_DH), lambda h, bi: (bi, 0, q_off + h)),
            pl.BlockSpec((1, s, NA_DH), lambda h, bi: (bi, 0, k_off + h)),
            pl.BlockSpec((1, s, NA_DH), lambda h, bi: (bi, 0, v_off + h)),
            pl.BlockSpec((1, l, NA_DH), lambda h, bi: (bi, 0, ck_off + h)),
            pl.BlockSpec((1, l, NA_DH), lambda h, bi: (bi, 0, cv_off + h)),
            pl.BlockSpec((1, 3, nq, nk), lambda h, bi: (h, 0, 0, 0)),
        ],
        out_specs=pl.BlockSpec((1, s, NA_DH), lambda h, bi: (bi, 0, h)),
        compiler_params=_params(("arbitrary", "arbitrary"), 48),
        name="natt",
    )(proj, proj, proj, projc, projc, bias)


def _merge_kernel(a_ref, n_ref, ga_ref, gb_ref, wa_ref, wb_ref, o_ref):
    ra = jnp.dot(a_ref[...], wa_ref[...], preferred_element_type=F32)
    rn = jnp.dot(n_ref[...], wb_ref[...], preferred_element_type=F32)
    o_ref[...] = (_sigmoid(ga_ref[...].astype(F32)) * ra + _sigmoid(gb_ref[...].astype(F32)) * rn).astype(o_ref.dtype)


def _merge(ret_in, na_in, proj2d, w_pa, w_pb):
    t, d = ret_in.shape
    tm, tn = MERGE_TM, MERGE_TN
    ga_off = (proj2d.shape[1] - 2 * d) // tn
    gb_off = (proj2d.shape[1] - d) // tn
    return pl.pallas_call(
        _merge_kernel,
        out_shape=jax.ShapeDtypeStruct((t, d), BF16),
        grid=(t // tm, d // tn),
        in_specs=[
            pl.BlockSpec((tm, ret_in.shape[1]), lambda i, j: (i, 0)),
            pl.BlockSpec((tm, na_in.shape[1]), lambda i, j: (i, 0)),
            pl.BlockSpec((tm, tn), lambda i, j: (i, ga_off + j)),
            pl.BlockSpec((tm, tn), lambda i, j: (i, gb_off + j)),
            pl.BlockSpec((w_pa.shape[0], tn), lambda i, j: (0, j)),
            pl.BlockSpec((w_pb.shape[0], tn), lambda i, j: (0, j)),
        ],
        out_specs=pl.BlockSpec((tm, tn), lambda i, j: (i, j)),
        compiler_params=_params(("arbitrary", "arbitrary"), 48),
        name="merge",
    )(ret_in, na_in, proj2d, proj2d, w_pa, w_pb)


def _oproj_kernel(m_ref, x_ref, g1_ref, ng_ref, sh_ref, sc_ref, wo_ref, wr_ref, br_ref, x1_ref, h2_ref, lg_ref):
    y = jnp.dot(m_ref[...], wo_ref[...], preferred_element_type=F32)
    x1 = x_ref[0] + g1_ref[0] * y
    x1_ref[0] = x1
    ms = jnp.mean(x1 * x1, axis=-1, keepdims=True)
    h2 = x1 * lax.rsqrt(ms + NORM_EPS) * ng_ref[...]
    h2 = h2 * (1.0 + sc_ref[0]) + sh_ref[0]
    h2_ref[...] = h2
    lg_ref[...] = jnp.dot(h2, wr_ref[...], preferred_element_type=F32, precision=lax.Precision.HIGHEST) + br_ref[...]


def _oproj(m, x, g1, norm_g, sh2, sc2, w_o, w_r, b_r):
    b, s, d = x.shape
    tm = OPROJ_TM
    spt = s // tm
    return pl.pallas_call(
        _oproj_kernel,
        out_shape=(
            jax.ShapeDtypeStruct((b, s, d), F32),
            jax.ShapeDtypeStruct((b * s, d), F32),
            jax.ShapeDtypeStruct((b * s, LANES), F32),
        ),
        grid=(b, spt),
        in_specs=[
            pl.BlockSpec((tm, d), lambda bi, i: (bi * spt + i, 0)),
            pl.BlockSpec((1, tm, d), lambda bi, i: (bi, i, 0)),
            pl.BlockSpec((1, 1, d), lambda bi, i: (bi, 0, 0)),
            pl.BlockSpec((1, d), lambda bi, i: (0, 0)),
            pl.BlockSpec((1, 1, d), lambda bi, i: (bi, 0, 0)),
            pl.BlockSpec((1, 1, d), lambda bi, i: (bi, 0, 0)),
            pl.BlockSpec((d, d), lambda bi, i: (0, 0)),
            pl.BlockSpec((d, LANES), lambda bi, i: (0, 0)),
            pl.BlockSpec((1, LANES), lambda bi, i: (0, 0)),
        ],
        out_specs=(
            pl.BlockSpec((1, tm, d), lambda bi, i: (bi, i, 0)),
            pl.BlockSpec((tm, d), lambda bi, i: (bi * spt + i, 0)),
            pl.BlockSpec((tm, LANES), lambda bi, i: (bi * spt + i, 0)),
        ),
        compiler_params=_params(("arbitrary", "arbitrary"), 48),
        name="oproj",
    )(m, x, g1, norm_g, sh2, sc2, w_o, w_r, b_r)


def _route_kernel(lg_ref, e_ref, w_ref, r_ref, cnt_ref, run_scr):
    i = pl.program_id(0)
    tm = lg_ref.shape[0]

    @pl.when(i == 0)
    def _():
        run_scr[...] = jnp.zeros_like(run_scr)

    l = lg_ref[...]
    lane = lax.broadcasted_iota(jnp.int32, l.shape, 1)
    vals, idxs, hots = [], [], []
    for _ in range(TOP_K):
        m = jnp.max(l, axis=-1, keepdims=True)
        idx = jnp.min(jnp.where(l == m, lane, LANES), axis=-1, keepdims=True)
        hot = lane == idx
        l = jnp.where(hot, -jnp.inf, l)
        vals.append(m)
        idxs.append(idx)
        hots.append(hot)
    exps = [jnp.exp(v - vals[0]) for v in vals]
    tot = exps[0]
    for e in exps[1:]:
        tot = tot + e

    member = hots[0]
    for hot in hots[1:]:
        member = member | hot
    member = member.astype(F32)
    ri = lax.broadcasted_iota(jnp.int32, (tm, tm), 0)
    ci = lax.broadcasted_iota(jnp.int32, (tm, tm), 1)
    lower = (ci < ri).astype(BF16)
    before = jnp.dot(lower, member.astype(BF16), preferred_element_type=F32) + run_scr[...]

    e_out = jnp.zeros(l.shape, jnp.int32)
    w_out = jnp.zeros(l.shape, F32)
    r_out = jnp.zeros(l.shape, jnp.int32)
    for k in range(TOP_K):
        rank = jnp.sum(jnp.where(hots[k], before, 0.0), axis=-1, keepdims=True).astype(jnp.int32)
        e_out = jnp.where(lane == k, idxs[k], e_out)
        w_out = jnp.where(lane == k, exps[k] / tot, w_out)
        r_out = jnp.where(lane == k, rank, r_out)
    e_ref[...] = e_out
    w_ref[...] = w_out
    r_ref[...] = r_out
    run_scr[...] += jnp.sum(member, axis=0, keepdims=True)
    cnt_ref[...] = run_scr[...].astype(jnp.int32)


def _route(logits):
    t = logits.shape[0]
    tm = ROUTE_TM
    row = pl.BlockSpec((tm, LANES), lambda i: (i, 0))
    return pl.pallas_call(
        _route_kernel,
        out_shape=(
            jax.ShapeDtypeStruct((t, LANES), jnp.int32),
            jax.ShapeDtypeStruct((t, LANES), F32),
            jax.ShapeDtypeStruct((t, LANES), jnp.int32),
            jax.ShapeDtypeStruct((1, LANES), jnp.int32),
        ),
        grid=(t // tm,),
        in_specs=[row],
        out_specs=(row, row, row, pl.BlockSpec((1, LANES), lambda i: (0, 0))),
        scratch_shapes=[pltpu.VMEM((1, LANES), F32)],
        compiler_params=_params(("arbitrary",), 32),
        name="route",
    )(logits)


def _dispatch_kernel(dest_ref, h_ref, xb_ref, sem):
    tm = h_ref.shape[0]

    def row_copy(t, k):
        return pltpu.make_async_copy(h_ref.at[pl.ds(t, 1)], xb_ref.at[pl.ds(dest_ref[TOP_K * t + k], 1)], sem)

    def issue(t, carry):
        for k in range(TOP_K):
            row_copy(t, k).start()
        return carry

    lax.fori_loop(0, tm, issue, 0)

    def drain(t, carry):
        for k in range(TOP_K):
            row_copy(t, k).wait()
        return carry

    lax.fori_loop(0, tm, drain, 0)


def _dispatch(h2, dest_flat):
    t, d = h2.shape
    tm = DISPATCH_TM
    return pl.pallas_call(
        _dispatch_kernel,
        out_shape=jax.ShapeDtypeStruct((t * TOP_K, d), F32),
        grid=(t // tm,),
        in_specs=[
            pl.BlockSpec((tm * TOP_K,), lambda i: (i,), memory_space=pltpu.SMEM),
            pl.BlockSpec((tm, d), lambda i: (i, 0)),
        ],
        out_specs=pl.BlockSpec(memory_space=pl.ANY),
        scratch_shapes=[pltpu.SemaphoreType.DMA],
        compiler_params=_params(("arbitrary",), 32),
        name="dispatch",
    )(dest_flat, h2)


def _expert_kernel(tile_ref, exp_ref, lo_ref, hi_ref, first_ref,
                   x_ref, wg_ref, bg_ref, wu_ref, bu_ref, wd_ref, bd_ref, o_ref, xs_scr):
    w = pl.program_id(0)
    j = pl.program_id(1)
    lo = lo_ref[w]
    hi = hi_ref[w]

    @pl.when(j == 0)
    def _():
        xs_scr[...] = x_ref[...].astype(BF16)

    @pl.when((j == 0) & (first_ref[w] == 1))
    def _():
        o_ref[...] = jnp.zeros_like(o_ref)

    @pl.when(hi > lo)
    def _():
        xs = xs_scr[...]
        gate = jnp.dot(xs, wg_ref[0].astype(BF16), preferred_element_type=F32) + bg_ref[0]
        up = jnp.dot(xs, wu_ref[0].astype(BF16), preferred_element_type=F32) + bu_ref[0]
        gate = jnp.minimum(gate, SWIGLU_LIMIT)
        up = jnp.clip(up, -SWIGLU_LIMIT, SWIGLU_LIMIT)
        act = gate * _sigmoid(SWIGLU_ALPHA * gate) * (up + 1.0)
        row = lax.broadcasted_iota(jnp.int32, (act.shape[0], 1), 0)
        mine = (row >= lo) & (row < hi)
        act = jnp.where(mine, act, 0.0)
        y = jnp.dot(act.astype(BF16), wd_ref[0].astype(BF16), preferred_element_type=F32)
        y += jnp.where(mine & (j == 0), bd_ref[0], 0.0)
        o_ref[...] += y


def _experts(xb, items, w_gate, b_gate, w_up, b_up, w_down, b_down):
    r, d = xb.shape
    e, _, f = w_gate.shape
    tm, tf = EXPERT_TM, EXPERT_TF
    nj = f // tf
    tile, expert, lo, hi, first = items
    n_items = tile.shape[0]

    def jj(j, w, hi_ref, lo_ref):
        return jnp.where(hi_ref[w] > lo_ref[w], j, nj - 1)

    return pl.pallas_call(
        _expert_kernel,
        out_shape=jax.ShapeDtypeStruct((r, d), F32),
        grid_spec=pltpu.PrefetchScalarGridSpec(
            num_scalar_prefetch=5,
            grid=(n_items, nj),
            in_specs=[
                pl.BlockSpec((tm, d), lambda w, j, ti, ex, lo_, hi_, fi: (ti[w], 0)),
                pl.BlockSpec((1, d, tf), lambda w, j, ti, ex, lo_, hi_, fi: (ex[w], 0, jj(j, w, hi_, lo_))),
                pl.BlockSpec((1, 1, tf), lambda w, j, ti, ex, lo_, hi_, fi: (ex[w], 0, jj(j, w, hi_, lo_))),
                pl.BlockSpec((1, d, tf), lambda w, j, ti, ex, lo_, hi_, fi: (ex[w], 0, jj(j, w, hi_, lo_))),
                pl.BlockSpec((1, 1, tf), lambda w, j, ti, ex, lo_, hi_, fi: (ex[w], 0, jj(j, w, hi_, lo_))),
                pl.BlockSpec((1, tf, d), lambda w, j, ti, ex, lo_, hi_, fi: (ex[w], jj(j, w, hi_, lo_), 0)),
                pl.BlockSpec((1, 1, d), lambda w, j, ti, ex, lo_, hi_, fi: (ex[w], 0, 0)),
            ],
            out_specs=pl.BlockSpec((tm, d), lambda w, j, ti, ex, lo_, hi_, fi: (ti[w], 0)),
            scratch_shapes=[pltpu.VMEM((tm, d), BF16)],
        ),
        compiler_params=_params(("arbitrary", "arbitrary"), 58),
        name="experts",
    )(tile, expert, lo, hi, first, xb, w_gate, b_gate.reshape(e, 1, f), w_up, b_up.reshape(e, 1, f),
      w_down, b_down.reshape(e, 1, d))


def _work_items(counts, n_rows):
    tm = EXPERT_TM
    n_tiles = n_rows // tm
    n_items = n_tiles + N_EXPERTS - 1
    cum = jnp.cumsum(counts)
    start = cum - counts
    tile_lo = jnp.arange(n_tiles, dtype=jnp.int32) * tm
    e_lo = jnp.searchsorted(cum, tile_lo, side="right").astype(jnp.int32)
    e_hi = jnp.searchsorted(cum, tile_lo + tm - 1, side="right").astype(jnp.int32)
    per_tile = e_hi - e_lo + 1
    off = jnp.cumsum(per_tile) - per_tile
    total = jnp.sum(per_tile)
    w = jnp.arange(n_items, dtype=jnp.int32)
    valid = w < total
    tile = jnp.clip(jnp.searchsorted(off, w, side="right").astype(jnp.int32) - 1, 0, n_tiles - 1)
    expert = jnp.where(valid, e_lo[tile] + w - off[tile], e_hi[n_tiles - 1])
    tile = jnp.where(valid, tile, n_tiles - 1)
    lo = jnp.clip(start[expert] - tile * tm, 0, tm)
    hi = jnp.clip(cum[expert] - tile * tm, 0, tm)
    hi = jnp.where(valid, jnp.maximum(hi, lo), lo)
    first = (valid & (w == off[tile])).astype(jnp.int32)
    return tile, expert.astype(jnp.int32), lo.astype(jnp.int32), hi.astype(jnp.int32), first, start


def _combine_kernel(dest_ref, yb_ref, w_ref, x1_ref, g2_ref, fg_ref, o_ref, buf, sem):
    tm = x1_ref.shape[1]

    def row_copy(t, k):
        return pltpu.make_async_copy(yb_ref.at[pl.ds(dest_ref[TOP_K * t + k], 1)], buf.at[k, pl.ds(t, 1)], sem)

    def issue(t, carry):
        for k in range(TOP_K):
            row_copy(t, k).start()
        return carry

    lax.fori_loop(0, tm, issue, 0)

    def drain(t, carry):
        for k in range(TOP_K):
            row_copy(t, k).wait()
        return carry

    lax.fori_loop(0, tm, drain, 0)

    wts = w_ref[...]
    moe = buf[0] * wts[:, 0:1]
    for k in range(1, TOP_K):
        moe += buf[k] * wts[:, k:k + 1]
    x2 = x1_ref[0] + g2_ref[0] * moe
    ms = jnp.mean(x2 * x2, axis=-1, keepdims=True)
    o_ref[0] = x2 * lax.rsqrt(ms + NORM_EPS) * fg_ref[...]


def _combine(yb, dest_flat, top_w, x1, g2, final_g):
    b, s, d = x1.shape
    tm = COMBINE_TM
    spt = s // tm
    return pl.pallas_call(
        _combine_kernel,
        out_shape=jax.ShapeDtypeStruct((b, s, d), F32),
        grid=(b, spt),
        in_specs=[
            pl.BlockSpec((tm * TOP_K,), lambda bi, i: (bi * spt + i,), memory_space=pltpu.SMEM),
            pl.BlockSpec(memory_space=pl.ANY),
            pl.BlockSpec((tm, LANES), lambda bi, i: (bi * spt + i, 0)),
            pl.BlockSpec((1, tm, d), lambda bi, i: (bi, i, 0)),
            pl.BlockSpec((1, 1, d), lambda bi, i: (bi, 0, 0)),
            pl.BlockSpec((1, d), lambda bi, i: (0, 0)),
        ],
        out_specs=pl.BlockSpec((1, tm, d), lambda bi, i: (bi, i, 0)),
        scratch_shapes=[pltpu.VMEM((TOP_K, tm, d), F32), pltpu.SemaphoreType.DMA],
        compiler_params=_params(("arbitrary", "arbitrary"), 32),
        name="combine",
    )(dest_flat, yb, top_w, x1, g2, final_g)


def kernel(x, c, ctx, c_ctx, ada_w, ada_b, norm1_g, norm2_g, w_in, w_pa, w_pb, w_o, ret_decay_fwd, ret_decay_bwd,
           na_rpb, w_router, b_router, w_gate, b_gate, w_up, b_up, w_down, b_down, final_g):
    assert ada_w.shape[0] == 1, "single layer"
    b, s, d = x.shape
    l = ctx.shape[1]
    in_w = w_in.shape[2]
    rows = s // GRID_W
    assert s % IN_TM == 0 and l % RET_CHUNK == 0 and rows >= NA_WROWS and rows % NA_QROWS == 0

    c_rows = jnp.zeros((16, d), F32).at[:b].set(c).at[b].set(c_ctx)
    mod = _ada(c_rows, ada_w[0], ada_b[0][None, :])
    sh1, sc1, g1, sh2, sc2, g2 = [mod[:b, None, i * d:(i + 1) * d] for i in range(6)]
    shc1 = jnp.broadcast_to(mod[b, 0 * d:1 * d][None, None, :], (b, 1, d))
    scc1 = jnp.broadcast_to(mod[b, 1 * d:2 * d][None, None, :], (b, 1, d))

    w_in_bf = w_in[0].astype(BF16)
    n1 = norm1_g[0][None, :]
    proj = _inproj(x, n1, sh1, sc1, w_in_bf, tuple(range(in_w // IN_TN)), IN_TM)
    qk_w = RET_HEADS * RET_DK
    v_w = RET_HEADS * RET_DV
    na_w = NA_HEADS * NA_DH
    ctx_cols = tuple(range(qk_w // IN_TN, (2 * qk_w + v_w) // IN_TN)) + tuple(
        range((2 * qk_w + 2 * v_w + na_w) // IN_TN, (2 * qk_w + 2 * v_w + 3 * na_w) // IN_TN))
    projc = _inproj(ctx, n1, shc1, scc1, w_in_bf, ctx_cols, l)

    lgf = jax.nn.log_sigmoid(ret_decay_fwd[0].astype(F32))
    lgb = jax.nn.log_sigmoid(ret_decay_bwd[0].astype(F32))
    cos_t, sin_t = _rope_tables(s)
    ret_in = _retention(proj, projc, lgf, lgb, cos_t, sin_t)
    na_in = _natt(proj, projc, _natt_bias(na_rpb[0], rows))

    t = b * s
    m = _merge(ret_in.reshape(t, -1), na_in.reshape(t, -1), proj.reshape(t, in_w),
               w_pa[0].astype(BF16), w_pb[0].astype(BF16))

    w_r = jnp.zeros((d, LANES), F32).at[:, :N_EXPERTS].set(w_router[0])
    b_r = jnp.full((1, LANES), NEG_INF, F32).at[0, :N_EXPERTS].set(b_router[0])
    x1, h2, logits = _oproj(m, x, g1, norm2_g[0][None, :], sh2, sc2, w_o[0].astype(BF16), w_r, b_r)

    top_e, top_w, rank, counts = _route(logits)
    counts = counts[0, :N_EXPERTS]
    items = _work_items(counts, t * TOP_K)
    start = items[5]
    dest = (start[top_e[:, :TOP_K]] + rank[:, :TOP_K]).reshape(-1)

    xb = _dispatch(h2, dest)
    yb = _experts(xb, items[:5], w_gate[0], b_gate[0], w_up[0], b_up[0], w_down[0], b_down[0])
    return _combine(yb, dest, top_w, x1, g2, final_g[None, :])
```

```python
import functools

import jax
import jax.numpy as jnp
import numpy as np
from jax import lax
from jax.experimental import pallas as pl
from jax.experimental.pallas import tpu as pltpu

F32 = jnp.float32
BF16 = jnp.bfloat16

GRID_W = 64
RET_HEADS = 8
RET_DK = 128
RET_DV = 256
RET_CHUNK = 128
ROPE_AXIS_DIM = RET_DK // 2
ROPE_BASE = 10000.0
NA_HEADS = 16
NA_DH = 128
NA_KR = 8
NA_KC = 16
N_EXPERTS = 32
TOP_K = 4
SWIGLU_ALPHA = 1.702
SWIGLU_LIMIT = 7.0
NORM_EPS = 1e-6
GN_EPS = 1e-5
NEG_INF = -1e30
LOG2_E = 1.4426950408889634

LANES = 128
MIB = 1024 * 1024

NA_QROWS = 4
NA_WROWS = NA_QROWS + NA_KR

IN_TM, IN_TN = 1024, 512
MERGE_TM, MERGE_TN = 1024, 512
OPROJ_TM = 256
ROUTE_TM = 512
DISPATCH_TM = 256
EXPERT_TM, EXPERT_SUB, EXPERT_TF = 1024, 256, 512
COMBINE_TM = 128


def _params(semantics, vmem_mib):
    return pltpu.CompilerParams(dimension_semantics=semantics, vmem_limit_bytes=vmem_mib * MIB)


def _sigmoid(x):
    return 1.0 / (1.0 + jnp.exp(-x))


def _ada_kernel(c_ref, w_ref, b_ref, o_ref):
    c = c_ref[...]
    s = c * _sigmoid(c)
    o_ref[...] = jnp.dot(s.astype(BF16), w_ref[...].astype(BF16), preferred_element_type=F32) + b_ref[...]


def _ada(c_rows, w, b):
    r, d = c_rows.shape
    n = w.shape[1]
    tn = 1024
    return pl.pallas_call(
        _ada_kernel,
        out_shape=jax.ShapeDtypeStruct((r, n), F32),
        grid=(n // tn,),
        in_specs=[
            pl.BlockSpec((r, d), lambda j: (0, 0)),
            pl.BlockSpec((d, tn), lambda j: (0, j)),
            pl.BlockSpec((1, tn), lambda j: (0, j)),
        ],
        out_specs=pl.BlockSpec((r, tn), lambda j: (0, j)),
        compiler_params=_params(("arbitrary",), 40),
        name="ada",
    )(c_rows, w, b)


def _inproj_kernel(cols_ref, x_ref, g_ref, sh_ref, sc_ref, w_ref, o_ref, h_scr):
    @pl.when(pl.program_id(2) == 0)
    def _():
        x = x_ref[0]
        ms = jnp.mean(x * x, axis=-1, keepdims=True)
        y = x * lax.rsqrt(ms + NORM_EPS) * g_ref[...]
        h_scr[...] = (y * (1.0 + sc_ref[0]) + sh_ref[0]).astype(BF16)

    o_ref[0] = jnp.dot(h_scr[...], w_ref[...], preferred_element_type=F32).astype(o_ref.dtype)


def _inproj(x, g, shift, scale, w, col_tiles, tm):
    b, n, d = x.shape
    tn = IN_TN
    nct = len(col_tiles)
    cols = jnp.asarray(np.asarray(col_tiles, np.int32))
    return pl.pallas_call(
        _inproj_kernel,
        out_shape=jax.ShapeDtypeStruct((b, n, nct * tn), BF16),
        grid_spec=pltpu.PrefetchScalarGridSpec(
            num_scalar_prefetch=1,
            grid=(b, n // tm, nct),
            in_specs=[
                pl.BlockSpec((1, tm, d), lambda bi, i, j, c: (bi, i, 0)),
                pl.BlockSpec((1, d), lambda bi, i, j, c: (0, 0)),
                pl.BlockSpec((1, 1, d), lambda bi, i, j, c: (bi, 0, 0)),
                pl.BlockSpec((1, 1, d), lambda bi, i, j, c: (bi, 0, 0)),
                pl.BlockSpec((d, tn), lambda bi, i, j, c: (0, c[j])),
            ],
            out_specs=pl.BlockSpec((1, tm, tn), lambda bi, i, j, c: (bi, i, j)),
            scratch_shapes=[pltpu.VMEM((tm, d), BF16)],
        ),
        compiler_params=_params(("arbitrary", "arbitrary", "arbitrary"), 48),
        name="inproj",
    )(cols, x, g, shift, scale, w)


def _swap_halves(x):
    lane = lax.broadcasted_iota(jnp.int32, x.shape, 1)
    return jnp.where(lane % 64 < 32, pltpu.roll(x, 96, 1), pltpu.roll(x, 32, 1))


def _ret_kernel(lgf_ref, lgb_ref, q_ref, k_ref, v_ref, g_ref, kc_ref, vc_ref, cos_ref, sin_ref,
                o_ref, ks_scr, rf_scr, rb_scr, ub_scr):
    h = pl.program_id(1)
    c = RET_CHUNK
    n = q_ref.shape[1]
    nc = n // c
    ncc = kc_ref.shape[1] // c
    lgf = lgf_ref[h]
    lgb = lgb_ref[h]
    k_scale = RET_DK ** -0.5

    pos_c = lax.broadcasted_iota(jnp.int32, (c, 1), 0).astype(F32)
    zeta_f = jnp.exp((c - 1.0 - pos_c) * lgf)
    zeta_b = jnp.exp(pos_c * lgb)
    xi_f = jnp.exp((pos_c + 1.0) * lgf)
    xi_b = jnp.exp((c - pos_c) * lgb)
    one = jnp.ones((1, 1), F32)
    gc_f = jnp.exp(one * (c * lgf))
    gc_b = jnp.exp(one * (c * lgb))
    ii = lax.broadcasted_iota(jnp.int32, (c, c), 0)
    jj = lax.broadcasted_iota(jnp.int32, (c, c), 1)
    diff = (ii - jj).astype(F32)
    dmask = (jnp.where(diff >= 0, jnp.exp(jnp.maximum(diff, 0.0) * lgf), 0.0)
             + jnp.where(diff <= 0, jnp.exp(jnp.maximum(-diff, 0.0) * lgb), 0.0))

    def ktv(k_bf, v_f32, zeta):
        return jnp.dot(k_bf.astype(F32).T.astype(BF16), (v_f32 * zeta).astype(BF16), preferred_element_type=F32)

    r_f = jnp.zeros((RET_DK, RET_DV), F32)
    for i in range(ncc):
        kc = (kc_ref[0, i * c:(i + 1) * c, :].astype(F32) * k_scale).astype(BF16)
        r_f = gc_f * r_f + ktv(kc, vc_ref[0, i * c:(i + 1) * c, :].astype(F32), zeta_f)
    r_b = jnp.zeros((RET_DK, RET_DV), F32)
    for i in reversed(range(ncc)):
        kc = (kc_ref[0, i * c:(i + 1) * c, :].astype(F32) * k_scale).astype(BF16)
        r_b = gc_b * r_b + ktv(kc, vc_ref[0, i * c:(i + 1) * c, :].astype(F32), zeta_b)

    def rope(x, rows):
        x = x.astype(F32)
        return x * cos_ref[rows, :] + _swap_halves(x) * sin_ref[rows, :]

    def chunk_updates(i, r):
        rows = pl.ds(pl.multiple_of(i * c, c), c)
        kb = (rope(k_ref[0, rows, :], rows) * k_scale).astype(BF16)
        ks_scr[rows, :] = kb
        kt = kb.astype(F32).T.astype(BF16)
        v = v_ref[0, rows, :].astype(F32)
        rf_scr[i] = r.astype(BF16)
        ub_scr[i] = jnp.dot(kt, (v * zeta_b).astype(BF16), preferred_element_type=F32)
        return gc_f * r + jnp.dot(kt, (v * zeta_f).astype(BF16), preferred_element_type=F32)

    lax.fori_loop(0, nc, chunk_updates, r_f, unroll=2)

    def bwd_scan(t, r):
        i = nc - 1 - t
        rb_scr[i] = r.astype(BF16)
        return gc_b * r + ub_scr[i]

    lax.fori_loop(0, nc, bwd_scan, r_b, unroll=2)

    def out_chunk(i, carry):
        rows = pl.ds(pl.multiple_of(i * c, c), c)
        q = rope(q_ref[0, rows, :], rows)
        kb = ks_scr[rows, :]
        vb = v_ref[0, rows, :]
        s = lax.dot_general(q.astype(BF16), kb, (((1,), (1,)), ((), ())), preferred_element_type=F32) * dmask
        o = jnp.dot(s.astype(BF16), vb, preferred_element_type=F32)
        o += jnp.dot((q * xi_f).astype(BF16), rf_scr[i], preferred_element_type=F32)
        o += jnp.dot((q * xi_b).astype(BF16), rb_scr[i], preferred_element_type=F32)
        mu = jnp.mean(o, axis=-1, keepdims=True)
        d = o - mu
        var = jnp.mean(d * d, axis=-1, keepdims=True)
        on = d * lax.rsqrt(var + GN_EPS)
        g = g_ref[0, rows, :].astype(F32)
        o_ref[0, rows, :] = (g * _sigmoid(g) * on).astype(o_ref.dtype)
        return carry

    lax.fori_loop(0, nc, out_chunk, 0, unroll=2)


def _retention(proj, projc, lgf, lgb, cos_t, sin_t):
    b, s, _ = proj.shape
    l = projc.shape[1]
    hq = RET_HEADS
    v_off = 2 * hq * RET_DK // RET_DV
    g_off = v_off + hq
    cv_off = hq * RET_DK // RET_DV
    smem = pl.BlockSpec(memory_space=pltpu.SMEM)
    return pl.pallas_call(
        _ret_kernel,
        out_shape=jax.ShapeDtypeStruct((b, s, hq * RET_DV), BF16),
        grid=(b, hq),
        in_specs=[
            smem, smem,
            pl.BlockSpec((1, s, RET_DK), lambda bi, h: (bi, 0, h)),
            pl.BlockSpec((1, s, RET_DK), lambda bi, h: (bi, 0, hq + h)),
            pl.BlockSpec((1, s, RET_DV), lambda bi, h: (bi, 0, v_off + h)),
            pl.BlockSpec((1, s, RET_DV), lambda bi, h: (bi, 0, g_off + h)),
            pl.BlockSpec((1, l, RET_DK), lambda bi, h: (bi, 0, h)),
            pl.BlockSpec((1, l, RET_DV), lambda bi, h: (bi, 0, cv_off + h)),
            pl.BlockSpec((s, RET_DK), lambda bi, h: (0, 0)),
            pl.BlockSpec((s, RET_DK), lambda bi, h: (0, 0)),
        ],
        out_specs=pl.BlockSpec((1, s, RET_DV), lambda bi, h: (bi, 0, h)),
        scratch_shapes=[
            pltpu.VMEM((s, RET_DK), BF16),
            pltpu.VMEM((s // RET_CHUNK, RET_DK, RET_DV), BF16),
            pltpu.VMEM((s // RET_CHUNK, RET_DK, RET_DV), BF16),
            pltpu.VMEM((s // RET_CHUNK, RET_DK, RET_DV), F32),
        ],
        compiler_params=_params(("arbitrary", "arbitrary"), 48),
        name="ret",
    )(lgf, lgb, proj, proj, proj, proj, projc, projc, cos_t, sin_t)


def _rope_tables(n):
    t = jnp.arange(n)
    row = (t // GRID_W).astype(F32)
    col = (t % GRID_W).astype(F32)
    inv = ROPE_BASE ** (-jnp.arange(0, ROPE_AXIS_DIM, 2, dtype=F32) / ROPE_AXIS_DIM)
    ar = row[:, None] * inv
    ac = col[:, None] * inv
    cos_t = jnp.concatenate([jnp.cos(ar), jnp.cos(ar), jnp.cos(ac), jnp.cos(ac)], axis=1)
    sin_t = jnp.concatenate([-jnp.sin(ar), jnp.sin(ar), -jnp.sin(ac), jnp.sin(ac)], axis=1)
    return cos_t, sin_t


def _natt_kernel(q_ref, k_ref, v_ref, kc_ref, vc_ref, bias_ref, o_ref):
    n = q_ref.shape[1]
    rows = n // GRID_W
    n_blk = rows // NA_QROWS
    nq = NA_QROWS * GRID_W
    nk = NA_WROWS * GRID_W
    scale = NA_DH ** -0.5
    kc = kc_ref[0]
    vc = vc_ref[0]
    nt = (((1,), (1,)), ((), ()))

    def block(blk, carry):
        ws = jnp.clip(blk * NA_QROWS - NA_KR // 2, 0, rows - NA_WROWS)
        variant = jnp.where(blk == 0, 0, jnp.where(blk == n_blk - 1, 2, 1))
        qrows = pl.ds(pl.multiple_of(blk * nq, nq), nq)
        krows = pl.ds(pl.multiple_of(ws * GRID_W, GRID_W), nk)
        q = q_ref[0, qrows, :]
        s_loc = lax.dot_general(q, k_ref[0, krows, :], nt, preferred_element_type=F32) + bias_ref[0, variant]
        s_ctx = lax.dot_general(q, kc, nt, preferred_element_type=F32)
        m = jnp.maximum(jnp.max(s_loc, axis=-1, keepdims=True), jnp.max(s_ctx, axis=-1, keepdims=True))
        p_loc = jnp.exp2((s_loc - m) * (scale * LOG2_E))
        p_ctx = jnp.exp2((s_ctx - m) * (scale * LOG2_E))
        denom = jnp.sum(p_loc, axis=-1, keepdims=True) + jnp.sum(p_ctx, axis=-1, keepdims=True)
        o = jnp.dot(p_loc.astype(BF16), v_ref[0, krows, :], preferred_element_type=F32)
        o += jnp.dot(p_ctx.astype(BF16), vc, preferred_element_type=F32)
        o_ref[0, qrows, :] = (o / denom).astype(o_ref.dtype)
        return carry

    lax.fori_loop(0, n_blk, block, 0, unroll=2)


def _natt_bias(rpb, rows):
    del rows
    w = GRID_W
    qc = np.arange(w)[:, None]
    kcol = np.arange(w)[None, :]
    cs = np.clip(qc - NA_KC // 2, 0, w - NA_KC)
    col_ok = (kcol >= cs) & (kcol < cs + NA_KC)
    dc_idx = np.clip(kcol - qc, -(NA_KC - 1), NA_KC - 1) + NA_KC - 1
    rq = np.arange(NA_QROWS)[:, None]
    wr = np.arange(NA_WROWS)[None, :]
    dr0 = np.where(wr < NA_KR, wr - rq + NA_KR - 1, -1)
    dr1 = np.where((wr >= rq) & (wr < rq + NA_KR), wr - rq + NA_KR // 2 - 1, -1)
    dr2 = np.where(wr >= NA_QROWS, wr - rq - 1, -1)
    dr = np.stack([dr0, dr1, dr2])
    t = jnp.where(col_ok[None, None], rpb.astype(F32)[:, :, dc_idx] * NA_DH ** 0.5, NEG_INF)
    full = t[:, np.maximum(dr, 0)]
    full = jnp.where((dr >= 0)[None, :, :, :, None, None], full, NEG_INF)
    full = full.transpose(0, 1, 2, 4, 3, 5)
    return full.reshape(rpb.shape[0], 3, NA_QROWS * w, NA_WROWS * w)


def _natt(proj, projc, bias):
    b, s, _ = proj.shape
    l = projc.shape[1]
    nh = NA_HEADS
    q_off = (2 * RET_HEADS * RET_DK + 2 * RET_HEADS * RET_DV) // NA_DH
    k_off = q_off + nh
    v_off = k_off + nh
    ck_off = (RET_HEADS * RET_DK + RET_HEADS * RET_DV) // NA_DH
    cv_off = ck_off + nh
    nq = NA_QROWS * GRID_W
    nk = NA_WROWS * GRID_W
    return pl.pallas_call(
        _natt_kernel,
        out_shape=jax.ShapeDtypeStruct((b, s, nh * NA_DH), BF16),
        grid=(nh, b),
        in_specs=[
            pl.BlockSpec((1, s, NA_DH), lambda h, bi: (bi, 0, q_off + h)),
            pl.BlockSpec((1, s, NA_DH), lambda h, bi: (bi, 0, k_off + h)),
            pl.BlockSpec((1, s, NA_DH), lambda h, bi: (bi, 0, v_off + h)),
            pl.BlockSpec((1, l, NA_DH), lambda h, bi: (bi, 0, ck_off + h)),
            pl.BlockSpec((1, l, NA_DH), lambda h, bi: (bi, 0, cv_off + h)),
            pl.BlockSpec((1, 3, nq, nk), lambda h, bi: (h, 0, 0, 0)),
        ],
        out_specs=pl.BlockSpec((1, s, NA_DH), lambda h, bi: (bi, 0, h)),
        compiler_params=_params(("arbitrary", "arbitrary"), 48),
        name="natt",
    )(proj, proj, proj, projc, projc, bias)


def _merge_kernel(a_ref, n_ref, ga_ref, gb_ref, wa_ref, wb_ref, o_ref):
    ra = jnp.dot(a_ref[...], wa_ref[...], preferred_element_type=F32)
    rn = jnp.dot(n_ref[...], wb_ref[...], preferred_element_type=F32)
    o_ref[...] = (_sigmoid(ga_ref[...].astype(F32)) * ra + _sigmoid(gb_ref[...].astype(F32)) * rn).astype(o_ref.dtype)


def _merge(ret_in, na_in, proj2d, w_pa, w_pb):
    t, d = ret_in.shape
    tm, tn = MERGE_TM, MERGE_TN
    ga_off = (proj2d.shape[1] - 2 * d) // tn
    gb_off = (proj2d.shape[1] - d) // tn
    return pl.pallas_call(
        _merge_kernel,
        out_shape=jax.ShapeDtypeStruct((t, d), BF16),
        grid=(t // tm, d // tn),
        in_specs=[
            pl.BlockSpec((tm, ret_in.shape[1]), lambda i, j: (i, 0)),
            pl.BlockSpec((tm, na_in.shape[1]), lambda i, j: (i, 0)),
            pl.BlockSpec((tm, tn), lambda i, j: (i, ga_off + j)),
            pl.BlockSpec((tm, tn), lambda i, j: (i, gb_off + j)),
            pl.BlockSpec((w_pa.shape[0], tn), lambda i, j: (0, j)),
            pl.BlockSpec((w_pb.shape[0], tn), lambda i, j: (0, j)),
        ],
        out_specs=pl.BlockSpec((tm, tn), lambda i, j: (i, j)),
        compiler_params=_params(("arbitrary", "arbitrary"), 48),
        name="merge",
    )(ret_in, na_in, proj2d, proj2d, w_pa, w_pb)


def _pack_bf16_pairs(lo, hi):
    lo_bits = lax.bitcast_convert_type(lo.astype(F32), jnp.uint32)
    hi_bits = lax.bitcast_convert_type(hi.astype(F32), jnp.uint32)
    return (lo_bits >> 16) | (hi_bits & jnp.uint32(0xFFFF0000))


def _unpack_bf16_pairs(words):
    lo = lax.bitcast_convert_type(words << 16, F32).astype(BF16)
    hi = lax.bitcast_convert_type(words & jnp.uint32(0xFFFF0000), F32).astype(BF16)
    return lo, hi


def _oproj_kernel(m_ref, x_ref, g1_ref, ng_ref, sh_ref, sc_ref, wo_ref, wrh_ref, wrl_ref, br_ref,
                  x1_ref, h2_ref, lg_ref):
    y = jnp.dot(m_ref[...], wo_ref[...], preferred_element_type=F32)
    x1 = x_ref[0] + g1_ref[0] * y
    x1_ref[0] = x1
    ms = jnp.mean(x1 * x1, axis=-1, keepdims=True)
    h2 = x1 * lax.rsqrt(ms + NORM_EPS) * ng_ref[...]
    h2 = h2 * (1.0 + sc_ref[0]) + sh_ref[0]
    h2_hi = h2.astype(BF16)
    half = h2.shape[1] // 2
    h2_ref[...] = _pack_bf16_pairs(h2_hi[:, :half], h2_hi[:, half:])
    h2_lo = (h2 - h2_hi.astype(F32)).astype(BF16)
    lg = jnp.dot(h2_hi, wrh_ref[...], preferred_element_type=F32)
    lg += jnp.dot(h2_lo, wrh_ref[...], preferred_element_type=F32)
    lg += jnp.dot(h2_hi, wrl_ref[...], preferred_element_type=F32)
    lg_ref[...] = lg + br_ref[...]


def _oproj(m, x, g1, norm_g, sh2, sc2, w_o, w_r, b_r):
    b, s, d = x.shape
    tm = OPROJ_TM
    spt = s // tm
    w_r_hi = w_r.astype(BF16)
    w_r_lo = (w_r - w_r_hi.astype(F32)).astype(BF16)
    return pl.pallas_call(
        _oproj_kernel,
        out_shape=(
            jax.ShapeDtypeStruct((b, s, d), F32),
            jax.ShapeDtypeStruct((b * s, d // 2), jnp.uint32),
            jax.ShapeDtypeStruct((b * s, LANES), F32),
        ),
        grid=(b, spt),
        in_specs=[
            pl.BlockSpec((tm, d), lambda bi, i: (bi * spt + i, 0)),
            pl.BlockSpec((1, tm, d), lambda bi, i: (bi, i, 0)),
            pl.BlockSpec((1, 1, d), lambda bi, i: (bi, 0, 0)),
            pl.BlockSpec((1, d), lambda bi, i: (0, 0)),
            pl.BlockSpec((1, 1, d), lambda bi, i: (bi, 0, 0)),
            pl.BlockSpec((1, 1, d), lambda bi, i: (bi, 0, 0)),
            pl.BlockSpec((d, d), lambda bi, i: (0, 0)),
            pl.BlockSpec((d, LANES), lambda bi, i: (0, 0)),
            pl.BlockSpec((d, LANES), lambda bi, i: (0, 0)),
            pl.BlockSpec((1, LANES), lambda bi, i: (0, 0)),
        ],
        out_specs=(
            pl.BlockSpec((1, tm, d), lambda bi, i: (bi, i, 0)),
            pl.BlockSpec((tm, d // 2), lambda bi, i: (bi * spt + i, 0)),
            pl.BlockSpec((tm, LANES), lambda bi, i: (bi * spt + i, 0)),
        ),
        compiler_params=_params(("arbitrary", "arbitrary"), 48),
        name="oproj",
    )(m, x, g1, norm_g, sh2, sc2, w_o, w_r_hi, w_r_lo, b_r)


def _route_kernel(lg_ref, e_ref, w_ref, r_ref, cnt_ref, run_scr):
    i = pl.program_id(0)
    tm = lg_ref.shape[0]

    @pl.when(i == 0)
    def _():
        run_scr[...] = jnp.zeros_like(run_scr)

    l = lg_ref[...]
    lane = lax.broadcasted_iota(jnp.int32, l.shape, 1)
    vals, idxs, hots = [], [], []
    for _ in range(TOP_K):
        m = jnp.max(l, axis=-1, keepdims=True)
        idx = jnp.min(jnp.where(l == m, lane, LANES), axis=-1, keepdims=True)
        hot = lane == idx
        l = jnp.where(hot, -jnp.inf, l)
        vals.append(m)
        idxs.append(idx)
        hots.append(hot)
    exps = [jnp.exp(v - vals[0]) for v in vals]
    tot = exps[0]
    for e in exps[1:]:
        tot = tot + e

    member = hots[0]
    for hot in hots[1:]:
        member = member | hot
    member = member.astype(F32)
    ri = lax.broadcasted_iota(jnp.int32, (tm, tm), 0)
    ci = lax.broadcasted_iota(jnp.int32, (tm, tm), 1)
    lower = (ci < ri).astype(BF16)
    before = jnp.dot(lower, member.astype(BF16), preferred_element_type=F32) + run_scr[...]

    e_out = jnp.zeros(l.shape, jnp.int32)
    w_out = jnp.zeros(l.shape, F32)
    r_out = jnp.zeros(l.shape, jnp.int32)
    for k in range(TOP_K):
        rank = jnp.sum(jnp.where(hots[k], before, 0.0), axis=-1, keepdims=True).astype(jnp.int32)
        e_out = jnp.where(lane == k, idxs[k], e_out)
        w_out = jnp.where(lane == k, exps[k] / tot, w_out)
        r_out = jnp.where(lane == k, rank, r_out)
    e_ref[...] = e_out
    w_ref[...] = w_out
    r_ref[...] = r_out
    run_scr[...] += jnp.sum(member, axis=0, keepdims=True)
    cnt_ref[...] = run_scr[...].astype(jnp.int32)


def _route(logits):
    t = logits.shape[0]
    tm = ROUTE_TM
    row = pl.BlockSpec((tm, LANES), lambda i: (i, 0))
    return pl.pallas_call(
        _route_kernel,
        out_shape=(
            jax.ShapeDtypeStruct((t, LANES), jnp.int32),
            jax.ShapeDtypeStruct((t, LANES), F32),
            jax.ShapeDtypeStruct((t, LANES), jnp.int32),
            jax.ShapeDtypeStruct((1, LANES), jnp.int32),
        ),
        grid=(t // tm,),
        in_specs=[row],
        out_specs=(row, row, row, pl.BlockSpec((1, LANES), lambda i: (0, 0))),
        scratch_shapes=[pltpu.VMEM((1, LANES), F32)],
        compiler_params=_params(("arbitrary",), 32),
        name="route",
    )(logits)


def _dispatch_kernel(dest_ref, h_ref, xb_ref, sem):
    tm = h_ref.shape[0]

    def row_copy(t, k):
        return pltpu.make_async_copy(h_ref.at[pl.ds(t, 1)], xb_ref.at[pl.ds(dest_ref[TOP_K * t + k], 1)], sem)

    def issue(t, carry):
        for k in range(TOP_K):
            row_copy(t, k).start()
        return carry

    lax.fori_loop(0, tm, issue, 0)
    for k in range(TOP_K):
        pltpu.make_async_copy(h_ref, xb_ref.at[pl.ds(0, tm)], sem).wait()


def _dispatch(h2, dest_flat):
    t, d = h2.shape
    tm = DISPATCH_TM
    return pl.pallas_call(
        _dispatch_kernel,
        out_shape=jax.ShapeDtypeStruct((t * TOP_K, d), h2.dtype),
        grid=(t // tm,),
        in_specs=[
            pl.BlockSpec((tm * TOP_K,), lambda i: (i,), memory_space=pltpu.SMEM),
            pl.BlockSpec((tm, d), lambda i: (i, 0)),
        ],
        out_specs=pl.BlockSpec(memory_space=pl.ANY),
        scratch_shapes=[pltpu.SemaphoreType.DMA],
        compiler_params=_params(("arbitrary",), 32),
        name="dispatch",
    )(dest_flat, h2)


def _expert_kernel(tile_ref, exp_ref, lo_ref, hi_ref, first_ref,
                   x_ref, wg_ref, bg_ref, wu_ref, bu_ref, wd_ref, bd_ref, o_ref, xs_scr, act_scr):
    w = pl.program_id(0)
    j = pl.program_id(1)
    nj = pl.num_programs(1) - 1
    lo = lo_ref[w]
    hi = hi_ref[w]
    sub = EXPERT_SUB
    half = x_ref.shape[1]

    @pl.when(j == 0)
    def _():
        x_lo, x_hi = _unpack_bf16_pairs(x_ref[...])
        xs_scr[:, :half] = x_lo
        xs_scr[:, half:] = x_hi

    @pl.when((j == 0) & (first_ref[w] == 1))
    def _():
        o_ref[...] = jnp.zeros_like(o_ref)

    def hidden(s, mine):
        xs = xs_scr[s * sub:(s + 1) * sub, :]
        gate = jnp.dot(xs, wg_ref[0].astype(BF16), preferred_element_type=F32) + bg_ref[0]
        up = jnp.dot(xs, wu_ref[0].astype(BF16), preferred_element_type=F32) + bu_ref[0]
        gate = jnp.minimum(gate, SWIGLU_LIMIT)
        up = jnp.clip(up, -SWIGLU_LIMIT, SWIGLU_LIMIT)
        act = gate * _sigmoid(SWIGLU_ALPHA * gate) * (up + 1.0)
        act_scr[j % 2, s * sub:(s + 1) * sub, :] = jnp.where(mine, act, 0.0).astype(BF16)

    def project(s):
        act = act_scr[(j + 1) % 2, s * sub:(s + 1) * sub, :]
        return jnp.dot(act, wd_ref[0].astype(BF16), preferred_element_type=F32)

    for s in range(x_ref.shape[0] // sub):
        active = (lo < (s + 1) * sub) & (hi > s * sub)
        rows = slice(s * sub, (s + 1) * sub)

        def row_mask(s=s):
            row = lax.broadcasted_iota(jnp.int32, (sub, 1), 0) + s * sub
            return (row >= lo) & (row < hi)

        @pl.when(active & (j == 0))
        def _(s=s):
            hidden(s, row_mask())

        @pl.when(active & (j > 0) & (j < nj))
        def _(s=s, rows=rows):
            hidden(s, row_mask())
            o_ref[rows, :] += project(s)

        @pl.when(active & (j == nj))
        def _(s=s, rows=rows):
            o_ref[rows, :] += project(s) + jnp.where(row_mask(), bd_ref[0], 0.0)


def _experts(xb, items, w_gate, b_gate, w_up, b_up, w_down, b_down):
    r, half = xb.shape
    e, d, f = w_gate.shape
    tm, tf = EXPERT_TM, EXPERT_TF
    nj = f // tf
    tile, expert, lo, hi, first = items
    n_items = tile.shape[0]

    def j_up(j, w, hi_ref, lo_ref):
        return jnp.where(hi_ref[w] > lo_ref[w], jnp.minimum(j, nj - 1), nj - 1)

    def j_down(j, w, hi_ref, lo_ref):
        return jnp.where(hi_ref[w] > lo_ref[w], jnp.maximum(j - 1, 0), nj - 1)

    return pl.pallas_call(
        _expert_kernel,
        out_shape=jax.ShapeDtypeStruct((r, d), F32),
        grid_spec=pltpu.PrefetchScalarGridSpec(
            num_scalar_prefetch=5,
            grid=(n_items, nj + 1),
            in_specs=[
                pl.BlockSpec((tm, half), lambda w, j, ti, ex, lo_, hi_, fi: (ti[w], 0)),
                pl.BlockSpec((1, d, tf), lambda w, j, ti, ex, lo_, hi_, fi: (ex[w], 0, j_up(j, w, hi_, lo_))),
                pl.BlockSpec((1, 1, tf), lambda w, j, ti, ex, lo_, hi_, fi: (ex[w], 0, j_up(j, w, hi_, lo_))),
                pl.BlockSpec((1, d, tf), lambda w, j, ti, ex, lo_, hi_, fi: (ex[w], 0, j_up(j, w, hi_, lo_))),
                pl.BlockSpec((1, 1, tf), lambda w, j, ti, ex, lo_, hi_, fi: (ex[w], 0, j_up(j, w, hi_, lo_))),
                pl.BlockSpec((1, tf, d), lambda w, j, ti, ex, lo_, hi_, fi: (ex[w], j_down(j, w, hi_, lo_), 0)),
                pl.BlockSpec((1, 1, d), lambda w, j, ti, ex, lo_, hi_, fi: (ex[w], 0, 0)),
            ],
            out_specs=pl.BlockSpec((tm, d), lambda w, j, ti, ex, lo_, hi_, fi: (ti[w], 0)),
            scratch_shapes=[pltpu.VMEM((tm, d), BF16), pltpu.VMEM((2, tm, tf), BF16)],
        ),
        compiler_params=_params(("arbitrary", "arbitrary"), 60),
        name="experts",
    )(tile, expert, lo, hi, first, xb, w_gate, b_gate.reshape(e, 1, f), w_up, b_up.reshape(e, 1, f),
      w_down, b_down.reshape(e, 1, d))


def _work_items(counts, n_rows):
    tm = EXPERT_TM
    n_tiles = n_rows // tm
    n_items = n_tiles + N_EXPERTS - 1
    cum = jnp.cumsum(counts)
    start = cum - counts
    tile_lo = jnp.arange(n_tiles, dtype=jnp.int32) * tm
    e_lo = jnp.searchsorted(cum, tile_lo, side="right").astype(jnp.int32)
    e_hi = jnp.searchsorted(cum, tile_lo + tm - 1, side="right").astype(jnp.int32)
    per_tile = e_hi - e_lo + 1
    off = jnp.cumsum(per_tile) - per_tile
    total = jnp.sum(per_tile)
    w = jnp.arange(n_items, dtype=jnp.int32)
    valid = w < total
    tile = jnp.clip(jnp.searchsorted(off, w, side="right").astype(jnp.int32) - 1, 0, n_tiles - 1)
    expert = jnp.where(valid, e_lo[tile] + w - off[tile], e_hi[n_tiles - 1])
    tile = jnp.where(valid, tile, n_tiles - 1)
    lo = jnp.clip(start[expert] - tile * tm, 0, tm)
    hi = jnp.clip(cum[expert] - tile * tm, 0, tm)
    hi = jnp.where(valid, jnp.maximum(hi, lo), lo)
    first = (valid & (w == off[tile])).astype(jnp.int32)
    return tile, expert.astype(jnp.int32), lo.astype(jnp.int32), hi.astype(jnp.int32), first, start


def _combine_kernel(dest_ref, dnext_ref, yb_ref, w_ref, x1_ref, g2_ref, fg_ref, o_ref, buf, sem):
    i = pl.program_id(0)
    tm = x1_ref.shape[0]
    slot = i % 2

    def gather(d_ref, into):
        def issue(t, carry):
            for k in range(TOP_K):
                pltpu.make_async_copy(yb_ref.at[pl.ds(d_ref[TOP_K * t + k], 1)], buf.at[into, k, pl.ds(t, 1)],
                                      sem.at[into]).start()
            return carry

        lax.fori_loop(0, tm, issue, 0)

    @pl.when(i == 0)
    def _():
        gather(dest_ref, 0)

    @pl.when(i + 1 < pl.num_programs(0))
    def _():
        gather(dnext_ref, 1 - slot)

    for k in range(TOP_K):
        pltpu.make_async_copy(yb_ref.at[pl.ds(0, tm)], buf.at[slot, k], sem.at[slot]).wait()

    wts = w_ref[...]
    moe = buf[slot, 0] * wts[:, 0:1]
    for k in range(1, TOP_K):
        moe += buf[slot, k] * wts[:, k:k + 1]
    x2 = x1_ref[...] + g2_ref[0] * moe
    ms = jnp.mean(x2 * x2, axis=-1, keepdims=True)
    o_ref[...] = x2 * lax.rsqrt(ms + NORM_EPS) * fg_ref[...]


def _combine(yb, dest_flat, top_w, x1, g2, final_g):
    b, s, d = x1.shape
    tm = COMBINE_TM
    spt = s // tm
    n = b * spt
    out = pl.pallas_call(
        _combine_kernel,
        out_shape=jax.ShapeDtypeStruct((b * s, d), F32),
        grid=(n,),
        in_specs=[
            pl.BlockSpec((tm * TOP_K,), lambda i: (i,), memory_space=pltpu.SMEM),
            pl.BlockSpec((tm * TOP_K,), lambda i: (jnp.minimum(i + 1, n - 1),), memory_space=pltpu.SMEM),
            pl.BlockSpec(memory_space=pl.ANY),
            pl.BlockSpec((tm, LANES), lambda i: (i, 0)),
            pl.BlockSpec((tm, d), lambda i: (i, 0)),
            pl.BlockSpec((1, 1, d), lambda i: (i // spt, 0, 0)),
            pl.BlockSpec((1, d), lambda i: (0, 0)),
        ],
        out_specs=pl.BlockSpec((tm, d), lambda i: (i, 0)),
        scratch_shapes=[pltpu.VMEM((2, TOP_K, tm, d), F32), pltpu.SemaphoreType.DMA((2,))],
        compiler_params=_params(("arbitrary",), 40),
        name="combine",
    )(dest_flat, dest_flat, yb, top_w, x1.reshape(b * s, d), g2, final_g)
    return out.reshape(b, s, d)


def kernel(x, c, ctx, c_ctx, ada_w, ada_b, norm1_g, norm2_g, w_in, w_pa, w_pb, w_o, ret_decay_fwd, ret_decay_bwd,
           na_rpb, w_router, b_router, w_gate, b_gate, w_up, b_up, w_down, b_down, final_g):
    assert ada_w.shape[0] == 1, "single layer"
    b, s, d = x.shape
    l = ctx.shape[1]
    in_w = w_in.shape[2]
    rows = s // GRID_W
    assert s % IN_TM == 0 and l % RET_CHUNK == 0 and rows >= NA_WROWS and rows % NA_QROWS == 0

    c_rows = jnp.zeros((16, d), F32).at[:b].set(c).at[b].set(c_ctx)
    mod = _ada(c_rows, ada_w[0], ada_b[0][None, :])
    sh1, sc1, g1, sh2, sc2, g2 = [mod[:b, None, i * d:(i + 1) * d] for i in range(6)]
    shc1 = jnp.broadcast_to(mod[b, 0 * d:1 * d][None, None, :], (b, 1, d))
    scc1 = jnp.broadcast_to(mod[b, 1 * d:2 * d][None, None, :], (b, 1, d))

    w_in_bf = w_in[0].astype(BF16)
    n1 = norm1_g[0][None, :]
    proj = _inproj(x, n1, sh1, sc1, w_in_bf, tuple(range(in_w // IN_TN)), IN_TM)
    qk_w = RET_HEADS * RET_DK
    v_w = RET_HEADS * RET_DV
    na_w = NA_HEADS * NA_DH
    ctx_cols = tuple(range(qk_w // IN_TN, (2 * qk_w + v_w) // IN_TN)) + tuple(
        range((2 * qk_w + 2 * v_w + na_w) // IN_TN, (2 * qk_w + 2 * v_w + 3 * na_w) // IN_TN))
    projc = _inproj(ctx, n1, shc1, scc1, w_in_bf, ctx_cols, l)

    lgf = jax.nn.log_sigmoid(ret_decay_fwd[0].astype(F32))
    lgb = jax.nn.log_sigmoid(ret_decay_bwd[0].astype(F32))
    cos_t, sin_t = _rope_tables(s)
    ret_in = _retention(proj, projc, lgf, lgb, cos_t, sin_t)
    na_in = _natt(proj, projc, _natt_bias(na_rpb[0], rows))

    t = b * s
    m = _merge(ret_in.reshape(t, -1), na_in.reshape(t, -1), proj.reshape(t, in_w),
               w_pa[0].astype(BF16), w_pb[0].astype(BF16))

    w_r = jnp.zeros((d, LANES), F32).at[:, :N_EXPERTS].set(w_router[0])
    b_r = jnp.full((1, LANES), NEG_INF, F32).at[0, :N_EXPERTS].set(b_router[0])
    x1, h2, logits = _oproj(m, x, g1, norm2_g[0][None, :], sh2, sc2, w_o[0].astype(BF16), w_r, b_r)

    top_e, top_w, rank, counts = _route(logits)
    counts = counts[0, :N_EXPERTS]
    items = _work_items(counts, t * TOP_K)
    start = items[5]
    dest = (start[top_e[:, :TOP_K]] + rank[:, :TOP_K]).reshape(-1)

    xb = _dispatch(h2, dest)
    yb = _experts(xb, items[:5], w_gate[0], b_gate[0], w_up[0], b_up[0], w_down[0], b_down[0])
    return _combine(yb, dest, top_w, x1, g2, final_g[None, :])
```

```python
import functools

import jax
import jax.numpy as jnp
import numpy as np
from jax import lax
from jax.experimental import pallas as pl
from jax.experimental.pallas import tpu as pltpu

F32 = jnp.float32
BF16 = jnp.bfloat16

GRID_W = 64
RET_HEADS = 8
RET_DK = 128
RET_DV = 256
RET_CHUNK = 128
ROPE_AXIS_DIM = RET_DK // 2
ROPE_BASE = 10000.0
NA_HEADS = 16
NA_DH = 128
NA_KR = 8
NA_KC = 16
N_EXPERTS = 32
TOP_K = 4
SWIGLU_ALPHA = 1.702
SWIGLU_LIMIT = 7.0
NORM_EPS = 1e-6
GN_EPS = 1e-5
NEG_INF = -1e30
LOG2_E = 1.4426950408889634

LANES = 128
MIB = 1024 * 1024

NA_QROWS = 4
NA_WROWS = NA_QROWS + NA_KR

IN_TM, IN_TN = 1024, 512
MERGE_TM, MERGE_TN = 1024, 512
OPROJ_TM = 256
ROUTE_TM = 512
DISPATCH_TM = 256
EXPERT_TM, EXPERT_SUB, EXPERT_TF = 1024, 256, 512
COMBINE_TM = 128


def _params(semantics, vmem_mib):
    return pltpu.CompilerParams(dimension_semantics=semantics, vmem_limit_bytes=vmem_mib * MIB)


def _sigmoid(x):
    return 1.0 / (1.0 + jnp.exp(-x))


def _ada_kernel(c_ref, w_ref, b_ref, o_ref):
    c = c_ref[...]
    s = c * _sigmoid(c)
    o_ref[...] = jnp.dot(s.astype(BF16), w_ref[...].astype(BF16), preferred_element_type=F32) + b_ref[...]


def _ada(c_rows, w, b):
    r, d = c_rows.shape
    n = w.shape[1]
    tn = 1024
    return pl.pallas_call(
        _ada_kernel,
        out_shape=jax.ShapeDtypeStruct((r, n), F32),
        grid=(n // tn,),
        in_specs=[
            pl.BlockSpec((r, d), lambda j: (0, 0)),
            pl.BlockSpec((d, tn), lambda j: (0, j)),
            pl.BlockSpec((1, tn), lambda j: (0, j)),
        ],
        out_specs=pl.BlockSpec((r, tn), lambda j: (0, j)),
        compiler_params=_params(("arbitrary",), 40),
        name="ada",
    )(c_rows, w, b)


def _inproj_kernel(cols_ref, x_ref, g_ref, sh_ref, sc_ref, w_ref, o_ref, h_scr):
    @pl.when(pl.program_id(2) == 0)
    def _():
        x = x_ref[0]
        ms = jnp.mean(x * x, axis=-1, keepdims=True)
        y = x * lax.rsqrt(ms + NORM_EPS) * g_ref[...]
        h_scr[...] = (y * (1.0 + sc_ref[0]) + sh_ref[0]).astype(BF16)

    o_ref[0] = jnp.dot(h_scr[...], w_ref[...], preferred_element_type=F32).astype(o_ref.dtype)


def _inproj(x, g, shift, scale, w, col_tiles, tm):
    b, n, d = x.shape
    tn = IN_TN
    nct = len(col_tiles)
    cols = jnp.asarray(np.asarray(col_tiles, np.int32))
    return pl.pallas_call(
        _inproj_kernel,
        out_shape=jax.ShapeDtypeStruct((b, n, nct * tn), BF16),
        grid_spec=pltpu.PrefetchScalarGridSpec(
            num_scalar_prefetch=1,
            grid=(b, n // tm, nct),
            in_specs=[
                pl.BlockSpec((1, tm, d), lambda bi, i, j, c: (bi, i, 0)),
                pl.BlockSpec((1, d), lambda bi, i, j, c: (0, 0)),
                pl.BlockSpec((1, 1, d), lambda bi, i, j, c: (bi, 0, 0)),
                pl.BlockSpec((1, 1, d), lambda bi, i, j, c: (bi, 0, 0)),
                pl.BlockSpec((d, tn), lambda bi, i, j, c: (0, c[j])),
            ],
            out_specs=pl.BlockSpec((1, tm, tn), lambda bi, i, j, c: (bi, i, j)),
            scratch_shapes=[pltpu.VMEM((tm, d), BF16)],
        ),
        compiler_params=_params(("arbitrary", "arbitrary", "arbitrary"), 48),
        name="inproj",
    )(cols, x, g, shift, scale, w)


def _swap_halves(x):
    lane = lax.broadcasted_iota(jnp.int32, x.shape, 1)
    return jnp.where(lane % 64 < 32, pltpu.roll(x, 96, 1), pltpu.roll(x, 32, 1))


def _ret_kernel(lgf_ref, lgb_ref, q_ref, k_ref, v_ref, g_ref, kc_ref, vc_ref, cos_ref, sin_ref,
                o_ref, ks_scr, rf_scr, rb_scr, ub_scr):
    h = pl.program_id(1)
    c = RET_CHUNK
    n = q_ref.shape[1]
    nc = n // c
    ncc = kc_ref.shape[1] // c
    lgf = lgf_ref[h]
    lgb = lgb_ref[h]
    k_scale = RET_DK ** -0.5

    pos_c = lax.broadcasted_iota(jnp.int32, (c, 1), 0).astype(F32)
    zeta_f = jnp.exp((c - 1.0 - pos_c) * lgf)
    zeta_b = jnp.exp(pos_c * lgb)
    xi_f = jnp.exp((pos_c + 1.0) * lgf)
    xi_b = jnp.exp((c - pos_c) * lgb)
    one = jnp.ones((1, 1), F32)
    gc_f = jnp.exp(one * (c * lgf))
    gc_b = jnp.exp(one * (c * lgb))
    ii = lax.broadcasted_iota(jnp.int32, (c, c), 0)
    jj = lax.broadcasted_iota(jnp.int32, (c, c), 1)
    diff = (ii - jj).astype(F32)
    dmask = (jnp.where(diff >= 0, jnp.exp(jnp.maximum(diff, 0.0) * lgf), 0.0)
             + jnp.where(diff <= 0, jnp.exp(jnp.maximum(-diff, 0.0) * lgb), 0.0))

    def ktv(k_bf, v_f32, zeta):
        return jnp.dot(k_bf.astype(F32).T.astype(BF16), (v_f32 * zeta).astype(BF16), preferred_element_type=F32)

    r_f = jnp.zeros((RET_DK, RET_DV), F32)
    for i in range(ncc):
        kc = (kc_ref[0, i * c:(i + 1) * c, :].astype(F32) * k_scale).astype(BF16)
        r_f = gc_f * r_f + ktv(kc, vc_ref[0, i * c:(i + 1) * c, :].astype(F32), zeta_f)
    r_b = jnp.zeros((RET_DK, RET_DV), F32)
    for i in reversed(range(ncc)):
        kc = (kc_ref[0, i * c:(i + 1) * c, :].astype(F32) * k_scale).astype(BF16)
        r_b = gc_b * r_b + ktv(kc, vc_ref[0, i * c:(i + 1) * c, :].astype(F32), zeta_b)

    def rope(x, rows):
        x = x.astype(F32)
        return x * cos_ref[rows, :] + _swap_halves(x) * sin_ref[rows, :]

    def chunk_updates(i, r):
        rows = pl.ds(pl.multiple_of(i * c, c), c)
        kb = (rope(k_ref[0, rows, :], rows) * k_scale).astype(BF16)
        ks_scr[rows, :] = kb
        kt = kb.astype(F32).T.astype(BF16)
        v = v_ref[0, rows, :].astype(F32)
        rf_scr[i] = r.astype(BF16)
        ub_scr[i] = jnp.dot(kt, (v * zeta_b).astype(BF16), preferred_element_type=F32)
        return gc_f * r + jnp.dot(kt, (v * zeta_f).astype(BF16), preferred_element_type=F32)

    lax.fori_loop(0, nc, chunk_updates, r_f, unroll=4)

    def bwd_scan(t, r):
        i = nc - 1 - t
        rb_scr[i] = r.astype(BF16)
        return gc_b * r + ub_scr[i]

    lax.fori_loop(0, nc, bwd_scan, r_b, unroll=2)

    def out_chunk(i, carry):
        rows = pl.ds(pl.multiple_of(i * c, c), c)
        q = rope(q_ref[0, rows, :], rows)
        kb = ks_scr[rows, :]
        vb = v_ref[0, rows, :]
        s = lax.dot_general(q.astype(BF16), kb, (((1,), (1,)), ((), ())), preferred_element_type=F32) * dmask
        o = jnp.dot(s.astype(BF16), vb, preferred_element_type=F32)
        o += jnp.dot((q * xi_f).astype(BF16), rf_scr[i], preferred_element_type=F32)
        o += jnp.dot((q * xi_b).astype(BF16), rb_scr[i], preferred_element_type=F32)
        mu = jnp.mean(o, axis=-1, keepdims=True)
        d = o - mu
        var = jnp.mean(d * d, axis=-1, keepdims=True)
        on = d * lax.rsqrt(var + GN_EPS)
        g = g_ref[0, rows, :].astype(F32)
        o_ref[0, rows, :] = (g * _sigmoid(g) * on).astype(o_ref.dtype)
        return carry

    lax.fori_loop(0, nc, out_chunk, 0, unroll=4)


def _retention(proj, projc, lgf, lgb, cos_t, sin_t):
    b, s, _ = proj.shape
    l = projc.shape[1]
    hq = RET_HEADS
    v_off = 2 * hq * RET_DK // RET_DV
    g_off = v_off + hq
    cv_off = hq * RET_DK // RET_DV
    smem = pl.BlockSpec(memory_space=pltpu.SMEM)
    return pl.pallas_call(
        _ret_kernel,
        out_shape=jax.ShapeDtypeStruct((b, s, hq * RET_DV), BF16),
        grid=(b, hq),
        in_specs=[
            smem, smem,
            pl.BlockSpec((1, s, RET_DK), lambda bi, h: (bi, 0, h)),
            pl.BlockSpec((1, s, RET_DK), lambda bi, h: (bi, 0, hq + h)),
            pl.BlockSpec((1, s, RET_DV), lambda bi, h: (bi, 0, v_off + h)),
            pl.BlockSpec((1, s, RET_DV), lambda bi, h: (bi, 0, g_off + h)),
            pl.BlockSpec((1, l, RET_DK), lambda bi, h: (bi, 0, h)),
            pl.BlockSpec((1, l, RET_DV), lambda bi, h: (bi, 0, cv_off + h)),
            pl.BlockSpec((s, RET_DK), lambda bi, h: (0, 0)),
            pl.BlockSpec((s, RET_DK), lambda bi, h: (0, 0)),
        ],
        out_specs=pl.BlockSpec((1, s, RET_DV), lambda bi, h: (bi, 0, h)),
        scratch_shapes=[
            pltpu.VMEM((s, RET_DK), BF16),
            pltpu.VMEM((s // RET_CHUNK, RET_DK, RET_DV), BF16),
            pltpu.VMEM((s // RET_CHUNK, RET_DK, RET_DV), BF16),
            pltpu.VMEM((s // RET_CHUNK, RET_DK, RET_DV), F32),
        ],
        compiler_params=_params(("arbitrary", "arbitrary"), 48),
        name="ret",
    )(lgf, lgb, proj, proj, proj, proj, projc, projc, cos_t, sin_t)


def _rope_tables(n):
    t = jnp.arange(n)
    row = (t // GRID_W).astype(F32)
    col = (t % GRID_W).astype(F32)
    inv = ROPE_BASE ** (-jnp.arange(0, ROPE_AXIS_DIM, 2, dtype=F32) / ROPE_AXIS_DIM)
    ar = row[:, None] * inv
    ac = col[:, None] * inv
    cos_t = jnp.concatenate([jnp.cos(ar), jnp.cos(ar), jnp.cos(ac), jnp.cos(ac)], axis=1)
    sin_t = jnp.concatenate([-jnp.sin(ar), jnp.sin(ar), -jnp.sin(ac), jnp.sin(ac)], axis=1)
    return cos_t, sin_t


def _natt_kernel(q_ref, k_ref, v_ref, kc_ref, vc_ref, pair_ref, o_ref, bias_scr):
    @pl.when(pl.program_id(1) == 0)
    def _():
        _natt_fill_bias(pair_ref, bias_scr)

    n = q_ref.shape[1]
    rows = n // GRID_W
    n_blk = rows // NA_QROWS
    nq = NA_QROWS * GRID_W
    nk = NA_WROWS * GRID_W
    scale = NA_DH ** -0.5
    kc = kc_ref[0]
    vc = vc_ref[0]
    nt = (((1,), (1,)), ((), ()))

    def block(blk, carry):
        ws = jnp.clip(blk * NA_QROWS - NA_KR // 2, 0, rows - NA_WROWS)
        variant = jnp.where(blk == 0, 0, jnp.where(blk == n_blk - 1, 2, 1))
        qrows = pl.ds(pl.multiple_of(blk * nq, nq), nq)
        krows = pl.ds(pl.multiple_of(ws * GRID_W, GRID_W), nk)
        q = q_ref[0, qrows, :]
        s_loc = lax.dot_general(q, k_ref[0, krows, :], nt, preferred_element_type=F32) + bias_scr[variant]
        s_ctx = lax.dot_general(q, kc, nt, preferred_element_type=F32)
        m = jnp.maximum(jnp.max(s_loc, axis=-1, keepdims=True), jnp.max(s_ctx, axis=-1, keepdims=True))
        p_loc = jnp.exp2((s_loc - m) * (scale * LOG2_E))
        p_ctx = jnp.exp2((s_ctx - m) * (scale * LOG2_E))
        denom = jnp.sum(p_loc, axis=-1, keepdims=True) + jnp.sum(p_ctx, axis=-1, keepdims=True)
        o = jnp.dot(p_loc.astype(BF16), v_ref[0, krows, :], preferred_element_type=F32)
        o += jnp.dot(p_ctx.astype(BF16), vc, preferred_element_type=F32)
        o_ref[0, qrows, :] = (o / denom).astype(o_ref.dtype)
        return carry

    lax.fori_loop(0, n_blk, block, 0, unroll=2)


def _natt_row_offsets():
    rq = np.arange(NA_QROWS)[:, None]
    wr = np.arange(NA_WROWS)[None, :]
    dr0 = np.where(wr < NA_KR, wr - rq + NA_KR - 1, -1)
    dr1 = np.where((wr >= rq) & (wr < rq + NA_KR), wr - rq + NA_KR // 2 - 1, -1)
    dr2 = np.where(wr >= NA_QROWS, wr - rq - 1, -1)
    return np.stack([dr0, dr1, dr2])


def _natt_fill_bias(pair_ref, bias_scr):
    w = GRID_W
    dr = _natt_row_offsets()
    neg = jnp.full((w, 2 * w), NEG_INF, F32)
    lane_row = lax.broadcasted_iota(jnp.int32, (w, NA_WROWS * w), 1) // w
    for kind in range(dr.shape[0]):
        for rq in range(NA_QROWS):
            tiles = []
            for p in range(NA_WROWS // 2):
                a, b_ = int(dr[kind, rq, 2 * p]), int(dr[kind, rq, 2 * p + 1])
                if a < 0 and b_ < 0:
                    tiles.append(neg)
                else:
                    tiles.append(pair_ref[0, b_ if b_ >= 0 else a + 1])
            strip = jnp.concatenate(tiles, axis=1)
            valid = np.nonzero(dr[kind, rq] >= 0)[0]
            keep = (lane_row >= int(valid[0])) & (lane_row <= int(valid[-1]))
            bias_scr[kind, rq * w:(rq + 1) * w, :] = jnp.where(keep, strip, NEG_INF)


def _natt_pair_tiles(rpb):
    w = GRID_W
    qc = np.arange(w)[:, None]
    kcol = np.arange(w)[None, :]
    cs = np.clip(qc - NA_KC // 2, 0, w - NA_KC)
    col_ok = (kcol >= cs) & (kcol < cs + NA_KC)
    dc_idx = np.clip(kcol - qc, -(NA_KC - 1), NA_KC - 1) + NA_KC - 1
    t = jnp.where(col_ok[None, None], rpb.astype(F32)[:, :, dc_idx] * NA_DH ** 0.5, NEG_INF)
    neg = jnp.full((rpb.shape[0], 1, w, w), NEG_INF, F32)
    return jnp.concatenate([jnp.concatenate([neg, t], axis=1), jnp.concatenate([t, neg], axis=1)], axis=3)


def _natt(proj, projc, pairs):
    b, s, _ = proj.shape
    l = projc.shape[1]
    nh = NA_HEADS
    q_off = (2 * RET_HEADS * RET_DK + 2 * RET_HEADS * RET_DV) // NA_DH
    k_off = q_off + nh
    v_off = k_off + nh
    ck_off = (RET_HEADS * RET_DK + RET_HEADS * RET_DV) // NA_DH
    cv_off = ck_off + nh
    nq = NA_QROWS * GRID_W
    nk = NA_WROWS * GRID_W
    return pl.pallas_call(
        _natt_kernel,
        out_shape=jax.ShapeDtypeStruct((b, s, nh * NA_DH), BF16),
        grid=(nh, b),
        in_specs=[
            pl.BlockSpec((1, s, NA_DH), lambda h, bi: (bi, 0, q_off + h)),
            pl.BlockSpec((1, s, NA_DH), lambda h, bi: (bi, 0, k_off + h)),
            pl.BlockSpec((1, s, NA_DH), lambda h, bi: (bi, 0, v_off + h)),
            pl.BlockSpec((1, l, NA_DH), lambda h, bi: (bi, 0, ck_off + h)),
            pl.BlockSpec((1, l, NA_DH), lambda h, bi: (bi, 0, cv_off + h)),
            pl.BlockSpec((1, 2 * NA_KR, GRID_W, 2 * GRID_W), lambda h, bi: (h, 0, 0, 0)),
        ],
        out_specs=pl.BlockSpec((1, s, NA_DH), lambda h, bi: (bi, 0, h)),
        scratch_shapes=[pltpu.VMEM((3, nq, nk), F32)],
        compiler_params=_params(("arbitrary", "arbitrary"), 48),
        name="natt",
    )(proj, proj, proj, projc, projc, pairs)


def _merge_kernel(a_ref, n_ref, ga_ref, gb_ref, wa_ref, wb_ref, o_ref):
    ra = jnp.dot(a_ref[...], wa_ref[...], preferred_element_type=F32)
    rn = jnp.dot(n_ref[...], wb_ref[...], preferred_element_type=F32)
    o_ref[...] = (_sigmoid(ga_ref[...].astype(F32)) * ra + _sigmoid(gb_ref[...].astype(F32)) * rn).astype(o_ref.dtype)


def _merge(ret_in, na_in, proj2d, w_pa, w_pb):
    t, d = ret_in.shape
    tm, tn = MERGE_TM, MERGE_TN
    ga_off = (proj2d.shape[1] - 2 * d) // tn
    gb_off = (proj2d.shape[1] - d) // tn
    return pl.pallas_call(
        _merge_kernel,
        out_shape=jax.ShapeDtypeStruct((t, d), BF16),
        grid=(t // tm, d // tn),
        in_specs=[
            pl.BlockSpec((tm, ret_in.shape[1]), lambda i, j: (i, 0)),
            pl.BlockSpec((tm, na_in.shape[1]), lambda i, j: (i, 0)),
            pl.BlockSpec((tm, tn), lambda i, j: (i, ga_off + j)),
            pl.BlockSpec((tm, tn), lambda i, j: (i, gb_off + j)),
            pl.BlockSpec((w_pa.shape[0], tn), lambda i, j: (0, j)),
            pl.BlockSpec((w_pb.shape[0], tn), lambda i, j: (0, j)),
        ],
        out_specs=pl.BlockSpec((tm, tn), lambda i, j: (i, j)),
        compiler_params=_params(("arbitrary", "arbitrary"), 48),
        name="merge",
    )(ret_in, na_in, proj2d, proj2d, w_pa, w_pb)


def _pack_bf16_pairs(lo, hi):
    lo_bits = lax.bitcast_convert_type(lo.astype(F32), jnp.uint32)
    hi_bits = lax.bitcast_convert_type(hi.astype(F32), jnp.uint32)
    return (lo_bits >> 16) | (hi_bits & jnp.uint32(0xFFFF0000))


def _unpack_bf16_pairs(words):
    lo = lax.bitcast_convert_type(words << 16, F32).astype(BF16)
    hi = lax.bitcast_convert_type(words & jnp.uint32(0xFFFF0000), F32).astype(BF16)
    return lo, hi


def _oproj_kernel(m_ref, x_ref, g1_ref, ng_ref, sh_ref, sc_ref, wo_ref, wrh_ref, wrl_ref, br_ref,
                  x1_ref, h2_ref, lg_ref):
    y = jnp.dot(m_ref[...], wo_ref[...], preferred_element_type=F32)
    x1 = x_ref[0] + g1_ref[0] * y
    x1_ref[0] = x1
    ms = jnp.mean(x1 * x1, axis=-1, keepdims=True)
    h2 = x1 * lax.rsqrt(ms + NORM_EPS) * ng_ref[...]
    h2 = h2 * (1.0 + sc_ref[0]) + sh_ref[0]
    h2_hi = h2.astype(BF16)
    half = h2.shape[1] // 2
    h2_ref[...] = _pack_bf16_pairs(h2_hi[:, :half], h2_hi[:, half:])
    h2_lo = (h2 - h2_hi.astype(F32)).astype(BF16)
    lg = jnp.dot(h2_hi, wrh_ref[...], preferred_element_type=F32)
    lg += jnp.dot(h2_lo, wrh_ref[...], preferred_element_type=F32)
    lg += jnp.dot(h2_hi, wrl_ref[...], preferred_element_type=F32)
    lg_ref[...] = lg + br_ref[...]


def _oproj(m, x, g1, norm_g, sh2, sc2, w_o, w_r, b_r):
    b, s, d = x.shape
    tm = OPROJ_TM
    spt = s // tm
    w_r_hi = w_r.astype(BF16)
    w_r_lo = (w_r - w_r_hi.astype(F32)).astype(BF16)
    return pl.pallas_call(
        _oproj_kernel,
        out_shape=(
            jax.ShapeDtypeStruct((b, s, d), F32),
            jax.ShapeDtypeStruct((b * s, d // 2), jnp.uint32),
            jax.ShapeDtypeStruct((b * s, LANES), F32),
        ),
        grid=(b, spt),
        in_specs=[
            pl.BlockSpec((tm, d), lambda bi, i: (bi * spt + i, 0)),
            pl.BlockSpec((1, tm, d), lambda bi, i: (bi, i, 0)),
            pl.BlockSpec((1, 1, d), lambda bi, i: (bi, 0, 0)),
            pl.BlockSpec((1, d), lambda bi, i: (0, 0)),
            pl.BlockSpec((1, 1, d), lambda bi, i: (bi, 0, 0)),
            pl.BlockSpec((1, 1, d), lambda bi, i: (bi, 0, 0)),
            pl.BlockSpec((d, d), lambda bi, i: (0, 0)),
            pl.BlockSpec((d, LANES), lambda bi, i: (0, 0)),
            pl.BlockSpec((d, LANES), lambda bi, i: (0, 0)),
            pl.BlockSpec((1, LANES), lambda bi, i: (0, 0)),
        ],
        out_specs=(
            pl.BlockSpec((1, tm, d), lambda bi, i: (bi, i, 0)),
            pl.BlockSpec((tm, d // 2), lambda bi, i: (bi * spt + i, 0)),
            pl.BlockSpec((tm, LANES), lambda bi, i: (bi * spt + i, 0)),
        ),
        compiler_params=_params(("arbitrary", "arbitrary"), 48),
        name="oproj",
    )(m, x, g1, norm_g, sh2, sc2, w_o, w_r_hi, w_r_lo, b_r)


def _route_kernel(lg_ref, e_ref, w_ref, r_ref, cnt_ref, run_scr):
    i = pl.program_id(0)
    tm = lg_ref.shape[0]

    @pl.when(i == 0)
    def _():
        run_scr[...] = jnp.zeros_like(run_scr)

    l = lg_ref[...]
    lane = lax.broadcasted_iota(jnp.int32, l.shape, 1)
    vals, idxs, hots = [], [], []
    for _ in range(TOP_K):
        m = jnp.max(l, axis=-1, keepdims=True)
        idx = jnp.min(jnp.where(l == m, lane, LANES), axis=-1, keepdims=True)
        hot = lane == idx
        l = jnp.where(hot, -jnp.inf, l)
        vals.append(m)
        idxs.append(idx)
        hots.append(hot)
    exps = [jnp.exp(v - vals[0]) for v in vals]
    tot = exps[0]
    for e in exps[1:]:
        tot = tot + e

    member = hots[0]
    for hot in hots[1:]:
        member = member | hot
    member = member.astype(F32)
    ri = lax.broadcasted_iota(jnp.int32, (tm, tm), 0)
    ci = lax.broadcasted_iota(jnp.int32, (tm, tm), 1)
    lower = (ci < ri).astype(BF16)
    before = jnp.dot(lower, member.astype(BF16), preferred_element_type=F32) + run_scr[...]

    e_out = jnp.zeros(l.shape, jnp.int32)
    w_out = jnp.zeros(l.shape, F32)
    r_out = jnp.zeros(l.shape, jnp.int32)
    for k in range(TOP_K):
        rank = jnp.sum(jnp.where(hots[k], before, 0.0), axis=-1, keepdims=True).astype(jnp.int32)
        e_out = jnp.where(lane == k, idxs[k], e_out)
        w_out = jnp.where(lane == k, exps[k] / tot, w_out)
        r_out = jnp.where(lane == k, rank, r_out)
    e_ref[...] = e_out
    w_ref[...] = w_out
    r_ref[...] = r_out
    run_scr[...] += jnp.sum(member, axis=0, keepdims=True)
    cnt_ref[...] = run_scr[...].astype(jnp.int32)


def _route(logits):
    t = logits.shape[0]
    tm = ROUTE_TM
    row = pl.BlockSpec((tm, LANES), lambda i: (i, 0))
    return pl.pallas_call(
        _route_kernel,
        out_shape=(
            jax.ShapeDtypeStruct((t, LANES), jnp.int32),
            jax.ShapeDtypeStruct((t, LANES), F32),
            jax.ShapeDtypeStruct((t, LANES), jnp.int32),
            jax.ShapeDtypeStruct((1, LANES), jnp.int32),
        ),
        grid=(t // tm,),
        in_specs=[row],
        out_specs=(row, row, row, pl.BlockSpec((1, LANES), lambda i: (0, 0))),
        scratch_shapes=[pltpu.VMEM((1, LANES), F32)],
        compiler_params=_params(("arbitrary",), 32),
        name="route",
    )(logits)


def _dispatch_kernel(dest_ref, h_ref, xb_ref, sem):
    tm = h_ref.shape[0]

    def row_copy(t, k):
        return pltpu.make_async_copy(h_ref.at[pl.ds(t, 1)], xb_ref.at[pl.ds(dest_ref[TOP_K * t + k], 1)], sem)

    def issue(t, carry):
        for k in range(TOP_K):
            row_copy(t, k).start()
        return carry

    lax.fori_loop(0, tm, issue, 0)
    for k in range(TOP_K):
        pltpu.make_async_copy(h_ref, xb_ref.at[pl.ds(0, tm)], sem).wait()


def _dispatch(h2, dest_flat):
    t, d = h2.shape
    tm = DISPATCH_TM
    return pl.pallas_call(
        _dispatch_kernel,
        out_shape=jax.ShapeDtypeStruct((t * TOP_K, d), h2.dtype),
        grid=(t // tm,),
        in_specs=[
            pl.BlockSpec((tm * TOP_K,), lambda i: (i,), memory_space=pltpu.SMEM),
            pl.BlockSpec((tm, d), lambda i: (i, 0)),
        ],
        out_specs=pl.BlockSpec(memory_space=pl.ANY),
        scratch_shapes=[pltpu.SemaphoreType.DMA],
        compiler_params=_params(("arbitrary",), 32),
        name="dispatch",
    )(dest_flat, h2)


def _expert_kernel(tile_ref, exp_ref, lo_ref, hi_ref, first_ref,
                   x_ref, wg_ref, bg_ref, wu_ref, bu_ref, wd_ref, bd_ref, o_ref, xs_scr, act_scr):
    w = pl.program_id(0)
    j = pl.program_id(1)
    nj = pl.num_programs(1) - 1
    lo = lo_ref[w]
    hi = hi_ref[w]
    sub = EXPERT_SUB
    half = x_ref.shape[1]

    @pl.when(j == 0)
    def _():
        x_lo, x_hi = _unpack_bf16_pairs(x_ref[...])
        xs_scr[:, :half] = x_lo
        xs_scr[:, half:] = x_hi

    @pl.when((j == 0) & (first_ref[w] == 1))
    def _():
        o_ref[...] = jnp.zeros_like(o_ref)

    def hidden(s, mine):
        xs = xs_scr[s * sub:(s + 1) * sub, :]
        gate = jnp.dot(xs, wg_ref[0].astype(BF16), preferred_element_type=F32) + bg_ref[0]
        up = jnp.dot(xs, wu_ref[0].astype(BF16), preferred_element_type=F32) + bu_ref[0]
        gate = jnp.minimum(gate, SWIGLU_LIMIT)
        up = jnp.clip(up, -SWIGLU_LIMIT, SWIGLU_LIMIT)
        act = gate * _sigmoid(SWIGLU_ALPHA * gate) * (up + 1.0)
        act_scr[j % 2, s * sub:(s + 1) * sub, :] = jnp.where(mine, act, 0.0).astype(BF16)

    def project(s):
        act = act_scr[(j + 1) % 2, s * sub:(s + 1) * sub, :]
        return jnp.dot(act, wd_ref[0].astype(BF16), preferred_element_type=F32)

    for s in range(x_ref.shape[0] // sub):
        active = (lo < (s + 1) * sub) & (hi > s * sub)
        rows = slice(s * sub, (s + 1) * sub)

        def row_mask(s=s):
            row = lax.broadcasted_iota(jnp.int32, (sub, 1), 0) + s * sub
            return (row >= lo) & (row < hi)

        @pl.when(active & (j == 0))
        def _(s=s):
            hidden(s, row_mask())

        @pl.when(active & (j > 0) & (j < nj))
        def _(s=s, rows=rows):
            hidden(s, row_mask())
            o_ref[rows, :] += project(s)

        @pl.when(active & (j == nj))
        def _(s=s, rows=rows):
            o_ref[rows, :] += project(s) + jnp.where(row_mask(), bd_ref[0], 0.0)


def _experts(xb, items, w_gate, b_gate, w_up, b_up, w_down, b_down):
    r, half = xb.shape
    e, d, f = w_gate.shape
    tm, tf = EXPERT_TM, EXPERT_TF
    nj = f // tf
    tile, expert, lo, hi, first = items
    n_items = tile.shape[0]

    def j_up(j, w, hi_ref, lo_ref):
        return jnp.where(hi_ref[w] > lo_ref[w], jnp.minimum(j, nj - 1), nj - 1)

    def j_down(j, w, hi_ref, lo_ref):
        return jnp.where(hi_ref[w] > lo_ref[w], jnp.maximum(j - 1, 0), nj - 1)

    def e_down(j, w, ex):
        return jnp.where(j == 0, ex[jnp.maximum(w - 1, 0)], ex[w])

    def jd_first(j, w, hi_ref, lo_ref):
        return jnp.where(j == 0, nj - 1, j_down(j, w, hi_ref, lo_ref))

    return pl.pallas_call(
        _expert_kernel,
        out_shape=jax.ShapeDtypeStruct((r, d), F32),
        grid_spec=pltpu.PrefetchScalarGridSpec(
            num_scalar_prefetch=5,
            grid=(n_items, nj + 1),
            in_specs=[
                pl.BlockSpec((tm, half), lambda w, j, ti, ex, lo_, hi_, fi: (ti[w], 0)),
                pl.BlockSpec((1, d, tf), lambda w, j, ti, ex, lo_, hi_, fi: (ex[w], 0, j_up(j, w, hi_, lo_))),
                pl.BlockSpec((1, 1, tf), lambda w, j, ti, ex, lo_, hi_, fi: (ex[w], 0, j_up(j, w, hi_, lo_))),
                pl.BlockSpec((1, d, tf), lambda w, j, ti, ex, lo_, hi_, fi: (ex[w], 0, j_up(j, w, hi_, lo_))),
                pl.BlockSpec((1, 1, tf), lambda w, j, ti, ex, lo_, hi_, fi: (ex[w], 0, j_up(j, w, hi_, lo_))),
                pl.BlockSpec((1, tf, d),
                             lambda w, j, ti, ex, lo_, hi_, fi: (e_down(j, w, ex), jd_first(j, w, hi_, lo_), 0)),
                pl.BlockSpec((1, 1, d), lambda w, j, ti, ex, lo_, hi_, fi: (ex[w], 0, 0)),
            ],
            out_specs=pl.BlockSpec((tm, d), lambda w, j, ti, ex, lo_, hi_, fi: (ti[w], 0)),
            scratch_shapes=[pltpu.VMEM((tm, d), BF16), pltpu.VMEM((2, tm, tf), BF16)],
        ),
        compiler_params=_params(("arbitrary", "arbitrary"), 60),
        name="experts",
    )(tile, expert, lo, hi, first, xb, w_gate, b_gate.reshape(e, 1, f), w_up, b_up.reshape(e, 1, f),
      w_down, b_down.reshape(e, 1, d))


def _work_items(counts, n_rows):
    tm = EXPERT_TM
    n_tiles = n_rows // tm
    n_items = n_tiles + N_EXPERTS - 1
    cum = jnp.cumsum(counts)
    start = cum - counts
    tile_lo = jnp.arange(n_tiles, dtype=jnp.int32) * tm
    e_lo = jnp.searchsorted(cum, tile_lo, side="right").astype(jnp.int32)
    e_hi = jnp.searchsorted(cum, tile_lo + tm - 1, side="right").astype(jnp.int32)
    per_tile = e_hi - e_lo + 1
    off = jnp.cumsum(per_tile) - per_tile
    total = jnp.sum(per_tile)
    w = jnp.arange(n_items, dtype=jnp.int32)
    valid = w < total
    tile = jnp.clip(jnp.searchsorted(off, w, side="right").astype(jnp.int32) - 1, 0, n_tiles - 1)
    expert = jnp.where(valid, e_lo[tile] + w - off[tile], e_hi[n_tiles - 1])
    tile = jnp.where(valid, tile, n_tiles - 1)
    lo = jnp.clip(start[expert] - tile * tm, 0, tm)
    hi = jnp.clip(cum[expert] - tile * tm, 0, tm)
    hi = jnp.where(valid, jnp.maximum(hi, lo), lo)
    first = (valid & (w == off[tile])).astype(jnp.int32)
    return tile, expert.astype(jnp.int32), lo.astype(jnp.int32), hi.astype(jnp.int32), first, start


def _combine_kernel(dest_ref, dnext_ref, yb_ref, w_ref, x1_ref, g2_ref, fg_ref, o_ref, buf, sem):
    i = pl.program_id(0)
    tm = x1_ref.shape[0]
    slot = i % 2

    def gather(d_ref, into):
        def issue(t, carry):
            for k in range(TOP_K):
                pltpu.make_async_copy(yb_ref.at[pl.ds(d_ref[TOP_K * t + k], 1)], buf.at[into, k, pl.ds(t, 1)],
                                      sem.at[into]).start()
            return carry

        lax.fori_loop(0, tm, issue, 0)

    @pl.when(i == 0)
    def _():
        gather(dest_ref, 0)

    @pl.when(i + 1 < pl.num_programs(0))
    def _():
        gather(dnext_ref, 1 - slot)

    for k in range(TOP_K):
        pltpu.make_async_copy(yb_ref.at[pl.ds(0, tm)], buf.at[slot, k], sem.at[slot]).wait()

    wts = w_ref[...]
    moe = buf[slot, 0] * wts[:, 0:1]
    for k in range(1, TOP_K):
        moe += buf[slot, k] * wts[:, k:k + 1]
    x2 = x1_ref[...] + g2_ref[0] * moe
    ms = jnp.mean(x2 * x2, axis=-1, keepdims=True)
    o_ref[...] = x2 * lax.rsqrt(ms + NORM_EPS) * fg_ref[...]


def _combine(yb, dest_flat, top_w, x1, g2, final_g):
    b, s, d = x1.shape
    tm = COMBINE_TM
    spt = s // tm
    n = b * spt
    out = pl.pallas_call(
        _combine_kernel,
        out_shape=jax.ShapeDtypeStruct((b * s, d), F32),
        grid=(n,),
        in_specs=[
            pl.BlockSpec((tm * TOP_K,), lambda i: (i,), memory_space=pltpu.SMEM),
            pl.BlockSpec((tm * TOP_K,), lambda i: (jnp.minimum(i + 1, n - 1),), memory_space=pltpu.SMEM),
            pl.BlockSpec(memory_space=pl.ANY),
            pl.BlockSpec((tm, LANES), lambda i: (i, 0)),
            pl.BlockSpec((tm, d), lambda i: (i, 0)),
            pl.BlockSpec((1, 1, d), lambda i: (i // spt, 0, 0)),
            pl.BlockSpec((1, d), lambda i: (0, 0)),
        ],
        out_specs=pl.BlockSpec((tm, d), lambda i: (i, 0)),
        scratch_shapes=[pltpu.VMEM((2, TOP_K, tm, d), F32), pltpu.SemaphoreType.DMA((2,))],
        compiler_params=_params(("arbitrary",), 40),
        name="combine",
    )(dest_flat, dest_flat, yb, top_w, x1.reshape(b * s, d), g2, final_g)
    return out.reshape(b, s, d)


def kernel(x, c, ctx, c_ctx, ada_w, ada_b, norm1_g, norm2_g, w_in, w_pa, w_pb, w_o, ret_decay_fwd, ret_decay_bwd,
           na_rpb, w_router, b_router, w_gate, b_gate, w_up, b_up, w_down, b_down, final_g):
    assert ada_w.shape[0] == 1, "single layer"
    b, s, d = x.shape
    l = ctx.shape[1]
    in_w = w_in.shape[2]
    rows = s // GRID_W
    assert s % IN_TM == 0 and l % RET_CHUNK == 0 and rows >= NA_WROWS and rows % NA_QROWS == 0

    c_rows = jnp.zeros((16, d), F32).at[:b].set(c).at[b].set(c_ctx)
    mod = _ada(c_rows, ada_w[0], ada_b[0][None, :])
    sh1, sc1, g1, sh2, sc2, g2 = [mod[:b, None, i * d:(i + 1) * d] for i in range(6)]
    shc1 = jnp.broadcast_to(mod[b, 0 * d:1 * d][None, None, :], (b, 1, d))
    scc1 = jnp.broadcast_to(mod[b, 1 * d:2 * d][None, None, :], (b, 1, d))

    w_in_bf = w_in[0].astype(BF16)
    n1 = norm1_g[0][None, :]
    proj = _inproj(x, n1, sh1, sc1, w_in_bf, tuple(range(in_w // IN_TN)), IN_TM)
    qk_w = RET_HEADS * RET_DK
    v_w = RET_HEADS * RET_DV
    na_w = NA_HEADS * NA_DH
    ctx_cols = tuple(range(qk_w // IN_TN, (2 * qk_w + v_w) // IN_TN)) + tuple(
        range((2 * qk_w + 2 * v_w + na_w) // IN_TN, (2 * qk_w + 2 * v_w + 3 * na_w) // IN_TN))
    projc = _inproj(ctx, n1, shc1, scc1, w_in_bf, ctx_cols, l)

    lgf = jax.nn.log_sigmoid(ret_decay_fwd[0].astype(F32))
    lgb = jax.nn.log_sigmoid(ret_decay_bwd[0].astype(F32))
    cos_t, sin_t = _rope_tables(s)
    ret_in = _retention(proj, projc, lgf, lgb, cos_t, sin_t)
    na_in = _natt(proj, projc, _natt_pair_tiles(na_rpb[0]))

    t = b * s
    m = _merge(ret_in.reshape(t, -1), na_in.reshape(t, -1), proj.reshape(t, in_w),
               w_pa[0].astype(BF16), w_pb[0].astype(BF16))

    w_r = jnp.zeros((d, LANES), F32).at[:, :N_EXPERTS].set(w_router[0])
    b_r = jnp.full((1, LANES), NEG_INF, F32).at[0, :N_EXPERTS].set(b_router[0])
    x1, h2, logits = _oproj(m, x, g1, norm2_g[0][None, :], sh2, sc2, w_o[0].astype(BF16), w_r, b_r)

    top_e, top_w, rank, counts = _route(logits)
    counts = counts[0, :N_EXPERTS]
    items = _work_items(counts, t * TOP_K)
    start = items[5]
    dest = (start[top_e[:, :TOP_K]] + rank[:, :TOP_K]).reshape(-1)

    xb = _dispatch(h2, dest)
    yb = _experts(xb, items[:5], w_gate[0], b_gate[0], w_up[0], b_up[0], w_down[0], b_down[0])
    return _combine(yb, dest, top_w, x1, g2, final_g[None, :])
```

```python
import functools

import jax
import jax.numpy as jnp
import numpy as np
from jax import lax
from jax.experimental import pallas as pl
from jax.experimental.pallas import tpu as pltpu

F32 = jnp.float32
BF16 = jnp.bfloat16

GRID_W = 64
RET_HEADS = 8
RET_DK = 128
RET_DV = 256
RET_CHUNK = 128
ROPE_AXIS_DIM = RET_DK // 2
ROPE_BASE = 10000.0
NA_HEADS = 16
NA_DH = 128
NA_KR = 8
NA_KC = 16
N_EXPERTS = 32
TOP_K = 4
SWIGLU_ALPHA = 1.702
SWIGLU_LIMIT = 7.0
NORM_EPS = 1e-6
GN_EPS = 1e-5
NEG_INF = -1e30
LOG2_E = 1.4426950408889634

LANES = 128
MIB = 1024 * 1024

NA_QROWS = 4
NA_WROWS = NA_QROWS + NA_KR

IN_TM, IN_TN = 1024, 512
MERGE_TM, MERGE_TN = 1024, 512
OPROJ_TM = 256
ROUTE_TM = 512
DISPATCH_TM = 256
EXPERT_TM, EXPERT_SUB, EXPERT_TF = 1024, 256, 512
COMBINE_TM = 128


def _params(semantics, vmem_mib):
    return pltpu.CompilerParams(dimension_semantics=semantics, vmem_limit_bytes=vmem_mib * MIB)


def _sigmoid(x):
    return 1.0 / (1.0 + jnp.exp(-x))


def _ada_kernel(c_ref, w_ref, b_ref, o_ref):
    c = c_ref[...]
    s = c * _sigmoid(c)
    o_ref[...] = jnp.dot(s.astype(BF16), w_ref[...].astype(BF16), preferred_element_type=F32) + b_ref[...]


def _ada(c_rows, w, b):
    r, d = c_rows.shape
    n = w.shape[1]
    tn = 1024
    return pl.pallas_call(
        _ada_kernel,
        out_shape=jax.ShapeDtypeStruct((r, n), F32),
        grid=(n // tn,),
        in_specs=[
            pl.BlockSpec((r, d), lambda j: (0, 0)),
            pl.BlockSpec((d, tn), lambda j: (0, j)),
            pl.BlockSpec((1, tn), lambda j: (0, j)),
        ],
        out_specs=pl.BlockSpec((r, tn), lambda j: (0, j)),
        compiler_params=_params(("arbitrary",), 40),
        name="ada",
    )(c_rows, w, b)


def _inproj_kernel(cols_ref, x_ref, g_ref, sh_ref, sc_ref, w_ref, o_ref, h_scr):
    @pl.when(pl.program_id(2) == 0)
    def _():
        x = x_ref[0]
        ms = jnp.mean(x * x, axis=-1, keepdims=True)
        y = x * lax.rsqrt(ms + NORM_EPS) * g_ref[...]
        h_scr[...] = (y * (1.0 + sc_ref[0]) + sh_ref[0]).astype(BF16)

    o_ref[0] = jnp.dot(h_scr[...], w_ref[...], preferred_element_type=F32).astype(o_ref.dtype)


def _inproj(x, g, shift, scale, w, col_tiles, tm):
    b, n, d = x.shape
    tn = IN_TN
    nct = len(col_tiles)
    cols = jnp.asarray(np.asarray(col_tiles, np.int32))
    return pl.pallas_call(
        _inproj_kernel,
        out_shape=jax.ShapeDtypeStruct((b, n, nct * tn), BF16),
        grid_spec=pltpu.PrefetchScalarGridSpec(
            num_scalar_prefetch=1,
            grid=(b, n // tm, nct),
            in_specs=[
                pl.BlockSpec((1, tm, d), lambda bi, i, j, c: (bi, i, 0)),
                pl.BlockSpec((1, d), lambda bi, i, j, c: (0, 0)),
                pl.BlockSpec((1, 1, d), lambda bi, i, j, c: (bi, 0, 0)),
                pl.BlockSpec((1, 1, d), lambda bi, i, j, c: (bi, 0, 0)),
                pl.BlockSpec((d, tn), lambda bi, i, j, c: (0, c[j])),
            ],
            out_specs=pl.BlockSpec((1, tm, tn), lambda bi, i, j, c: (bi, i, j)),
            scratch_shapes=[pltpu.VMEM((tm, d), BF16)],
        ),
        compiler_params=_params(("arbitrary", "arbitrary", "arbitrary"), 48),
        name="inproj",
    )(cols, x, g, shift, scale, w)


def _swap_halves(x):
    lane = lax.broadcasted_iota(jnp.int32, x.shape, 1)
    return jnp.where(lane % 64 < 32, pltpu.roll(x, 96, 1), pltpu.roll(x, 32, 1))


def _ret_kernel(lgf_ref, lgb_ref, q_ref, k_ref, v_ref, g_ref, kc_ref, vc_ref, cos_ref, sin_ref,
                o_ref, ks_scr, rf_scr, rb_scr, ub_scr):
    h = pl.program_id(1)
    c = RET_CHUNK
    n = q_ref.shape[1]
    nc = n // c
    ncc = kc_ref.shape[1] // c
    lgf = lgf_ref[h]
    lgb = lgb_ref[h]
    k_scale = RET_DK ** -0.5

    pos_c = lax.broadcasted_iota(jnp.int32, (c, 1), 0).astype(F32)
    zeta_f = jnp.exp((c - 1.0 - pos_c) * lgf)
    zeta_b = jnp.exp(pos_c * lgb)
    xi_f = jnp.exp((pos_c + 1.0) * lgf)
    xi_b = jnp.exp((c - pos_c) * lgb)
    one = jnp.ones((1, 1), F32)
    gc_f = jnp.exp(one * (c * lgf))
    gc_b = jnp.exp(one * (c * lgb))
    ii = lax.broadcasted_iota(jnp.int32, (c, c), 0)
    jj = lax.broadcasted_iota(jnp.int32, (c, c), 1)
    diff = (ii - jj).astype(F32)
    dmask = (jnp.where(diff >= 0, jnp.exp(jnp.maximum(diff, 0.0) * lgf), 0.0)
             + jnp.where(diff <= 0, jnp.exp(jnp.maximum(-diff, 0.0) * lgb), 0.0))

    def ktv(k_bf, v_f32, zeta):
        return jnp.dot(k_bf.astype(F32).T.astype(BF16), (v_f32 * zeta).astype(BF16), preferred_element_type=F32)

    r_f = jnp.zeros((RET_DK, RET_DV), F32)
    for i in range(ncc):
        kc = (kc_ref[0, i * c:(i + 1) * c, :].astype(F32) * k_scale).astype(BF16)
        r_f = gc_f * r_f + ktv(kc, vc_ref[0, i * c:(i + 1) * c, :].astype(F32), zeta_f)
    r_b = jnp.zeros((RET_DK, RET_DV), F32)
    for i in reversed(range(ncc)):
        kc = (kc_ref[0, i * c:(i + 1) * c, :].astype(F32) * k_scale).astype(BF16)
        r_b = gc_b * r_b + ktv(kc, vc_ref[0, i * c:(i + 1) * c, :].astype(F32), zeta_b)

    def rope(x, rows):
        x = x.astype(F32)
        return x * cos_ref[rows, :] + _swap_halves(x) * sin_ref[rows, :]

    def chunk_updates(i, r):
        rows = pl.ds(pl.multiple_of(i * c, c), c)
        kb = (rope(k_ref[0, rows, :], rows) * k_scale).astype(BF16)
        ks_scr[rows, :] = kb
        kt = kb.astype(F32).T.astype(BF16)
        v = v_ref[0, rows, :].astype(F32)
        rf_scr[i] = r.astype(BF16)
        ub_scr[i] = jnp.dot(kt, (v * zeta_b).astype(BF16), preferred_element_type=F32)
        return gc_f * r + jnp.dot(kt, (v * zeta_f).astype(BF16), preferred_element_type=F32)

    lax.fori_loop(0, nc, chunk_updates, r_f, unroll=8)

    def bwd_scan(t, r):
        i = nc - 1 - t
        rb_scr[i] = r.astype(BF16)
        return gc_b * r + ub_scr[i]

    lax.fori_loop(0, nc, bwd_scan, r_b, unroll=2)

    def out_chunk(i, carry):
        rows = pl.ds(pl.multiple_of(i * c, c), c)
        q = rope(q_ref[0, rows, :], rows)
        kb = ks_scr[rows, :]
        vb = v_ref[0, rows, :]
        s = lax.dot_general(q.astype(BF16), kb, (((1,), (1,)), ((), ())), preferred_element_type=F32) * dmask
        o = jnp.dot(s.astype(BF16), vb, preferred_element_type=F32)
        o += jnp.dot((q * xi_f).astype(BF16), rf_scr[i], preferred_element_type=F32)
        o += jnp.dot((q * xi_b).astype(BF16), rb_scr[i], preferred_element_type=F32)
        mu = jnp.mean(o, axis=-1, keepdims=True)
        d = o - mu
        var = jnp.mean(d * d, axis=-1, keepdims=True)
        on = d * lax.rsqrt(var + GN_EPS)
        g = g_ref[0, rows, :].astype(F32)
        o_ref[0, rows, :] = (g * _sigmoid(g) * on).astype(o_ref.dtype)
        return carry

    lax.fori_loop(0, nc, out_chunk, 0, unroll=8)


def _retention(proj, projc, lgf, lgb, cos_t, sin_t):
    b, s, _ = proj.shape
    l = projc.shape[1]
    hq = RET_HEADS
    v_off = 2 * hq * RET_DK // RET_DV
    g_off = v_off + hq
    cv_off = hq * RET_DK // RET_DV
    smem = pl.BlockSpec(memory_space=pltpu.SMEM)
    return pl.pallas_call(
        _ret_kernel,
        out_shape=jax.ShapeDtypeStruct((b, s, hq * RET_DV), BF16),
        grid=(b, hq),
        in_specs=[
            smem, smem,
            pl.BlockSpec((1, s, RET_DK), lambda bi, h: (bi, 0, h)),
            pl.BlockSpec((1, s, RET_DK), lambda bi, h: (bi, 0, hq + h)),
            pl.BlockSpec((1, s, RET_DV), lambda bi, h: (bi, 0, v_off + h)),
            pl.BlockSpec((1, s, RET_DV), lambda bi, h: (bi, 0, g_off + h)),
            pl.BlockSpec((1, l, RET_DK), lambda bi, h: (bi, 0, h)),
            pl.BlockSpec((1, l, RET_DV), lambda bi, h: (bi, 0, cv_off + h)),
            pl.BlockSpec((s, RET_DK), lambda bi, h: (0, 0)),
            pl.BlockSpec((s, RET_DK), lambda bi, h: (0, 0)),
        ],
        out_specs=pl.BlockSpec((1, s, RET_DV), lambda bi, h: (bi, 0, h)),
        scratch_shapes=[
            pltpu.VMEM((s, RET_DK), BF16),
            pltpu.VMEM((s // RET_CHUNK, RET_DK, RET_DV), BF16),
            pltpu.VMEM((s // RET_CHUNK, RET_DK, RET_DV), BF16),
            pltpu.VMEM((s // RET_CHUNK, RET_DK, RET_DV), F32),
        ],
        compiler_params=_params(("arbitrary", "arbitrary"), 48),
        name="ret",
    )(lgf, lgb, proj, proj, proj, proj, projc, projc, cos_t, sin_t)


def _rope_tables(n):
    t = jnp.arange(n)
    row = (t // GRID_W).astype(F32)
    col = (t % GRID_W).astype(F32)
    inv = ROPE_BASE ** (-jnp.arange(0, ROPE_AXIS_DIM, 2, dtype=F32) / ROPE_AXIS_DIM)
    ar = row[:, None] * inv
    ac = col[:, None] * inv
    cos_t = jnp.concatenate([jnp.cos(ar), jnp.cos(ar), jnp.cos(ac), jnp.cos(ac)], axis=1)
    sin_t = jnp.concatenate([-jnp.sin(ar), jnp.sin(ar), -jnp.sin(ac), jnp.sin(ac)], axis=1)
    return cos_t, sin_t


def _natt_kernel(q_ref, k_ref, v_ref, kc_ref, vc_ref, pair_ref, o_ref, bias_scr):
    @pl.when(pl.program_id(1) == 0)
    def _():
        _natt_fill_bias(pair_ref, bias_scr)

    n = q_ref.shape[1]
    rows = n // GRID_W
    n_blk = rows // NA_QROWS
    nq = NA_QROWS * GRID_W
    nk = NA_WROWS * GRID_W
    scale = NA_DH ** -0.5
    kc = kc_ref[0]
    vc = vc_ref[0]
    nt = (((1,), (1,)), ((), ()))

    def block(blk, carry):
        ws = jnp.clip(blk * NA_QROWS - NA_KR // 2, 0, rows - NA_WROWS)
        variant = jnp.where(blk == 0, 0, jnp.where(blk == n_blk - 1, 2, 1))
        qrows = pl.ds(pl.multiple_of(blk * nq, nq), nq)
        krows = pl.ds(pl.multiple_of(ws * GRID_W, GRID_W), nk)
        q = q_ref[0, qrows, :]
        s_loc = lax.dot_general(q, k_ref[0, krows, :], nt, preferred_element_type=F32) + bias_scr[variant]
        s_ctx = lax.dot_general(q, kc, nt, preferred_element_type=F32)
        m = jnp.maximum(jnp.max(s_loc, axis=-1, keepdims=True), jnp.max(s_ctx, axis=-1, keepdims=True))
        p_loc = jnp.exp2((s_loc - m) * (scale * LOG2_E))
        p_ctx = jnp.exp2((s_ctx - m) * (scale * LOG2_E))
        denom = jnp.sum(p_loc, axis=-1, keepdims=True) + jnp.sum(p_ctx, axis=-1, keepdims=True)
        o = jnp.dot(p_loc.astype(BF16), v_ref[0, krows, :], preferred_element_type=F32)
        o += jnp.dot(p_ctx.astype(BF16), vc, preferred_element_type=F32)
        o_ref[0, qrows, :] = (o / denom).astype(o_ref.dtype)
        return carry

    lax.fori_loop(0, n_blk, block, 0, unroll=4)


def _natt_row_offsets():
    rq = np.arange(NA_QROWS)[:, None]
    wr = np.arange(NA_WROWS)[None, :]
    dr0 = np.where(wr < NA_KR, wr - rq + NA_KR - 1, -1)
    dr1 = np.where((wr >= rq) & (wr < rq + NA_KR), wr - rq + NA_KR // 2 - 1, -1)
    dr2 = np.where(wr >= NA_QROWS, wr - rq - 1, -1)
    return np.stack([dr0, dr1, dr2])


def _natt_fill_bias(pair_ref, bias_scr):
    w = GRID_W
    dr = _natt_row_offsets()
    neg = jnp.full((w, 2 * w), NEG_INF, F32)
    lane_row = lax.broadcasted_iota(jnp.int32, (w, NA_WROWS * w), 1) // w
    for kind in range(dr.shape[0]):
        for rq in range(NA_QROWS):
            tiles = []
            for p in range(NA_WROWS // 2):
                a, b_ = int(dr[kind, rq, 2 * p]), int(dr[kind, rq, 2 * p + 1])
                if a < 0 and b_ < 0:
                    tiles.append(neg)
                else:
                    tiles.append(pair_ref[0, b_ if b_ >= 0 else a + 1])
            strip = jnp.concatenate(tiles, axis=1)
            valid = np.nonzero(dr[kind, rq] >= 0)[0]
            keep = (lane_row >= int(valid[0])) & (lane_row <= int(valid[-1]))
            bias_scr[kind, rq * w:(rq + 1) * w, :] = jnp.where(keep, strip, NEG_INF)


def _natt_pair_tiles(rpb):
    w = GRID_W
    qc = np.arange(w)[:, None]
    kcol = np.arange(w)[None, :]
    cs = np.clip(qc - NA_KC // 2, 0, w - NA_KC)
    col_ok = (kcol >= cs) & (kcol < cs + NA_KC)
    dc_idx = np.clip(kcol - qc, -(NA_KC - 1), NA_KC - 1) + NA_KC - 1
    t = jnp.where(col_ok[None, None], rpb.astype(F32)[:, :, dc_idx] * NA_DH ** 0.5, NEG_INF)
    neg = jnp.full((rpb.shape[0], 1, w, w), NEG_INF, F32)
    return jnp.concatenate([jnp.concatenate([neg, t], axis=1), jnp.concatenate([t, neg], axis=1)], axis=3)


def _natt(proj, projc, pairs):
    b, s, _ = proj.shape
    l = projc.shape[1]
    nh = NA_HEADS
    q_off = (2 * RET_HEADS * RET_DK + 2 * RET_HEADS * RET_DV) // NA_DH
    k_off = q_off + nh
    v_off = k_off + nh
    ck_off = (RET_HEADS * RET_DK + RET_HEADS * RET_DV) // NA_DH
    cv_off = ck_off + nh
    nq = NA_QROWS * GRID_W
    nk = NA_WROWS * GRID_W
    return pl.pallas_call(
        _natt_kernel,
        out_shape=jax.ShapeDtypeStruct((b, s, nh * NA_DH), BF16),
        grid=(nh, b),
        in_specs=[
            pl.BlockSpec((1, s, NA_DH), lambda h, bi: (bi, 0, q_off + h)),
            pl.BlockSpec((1, s, NA_DH), lambda h, bi: (bi, 0, k_off + h)),
            pl.BlockSpec((1, s, NA_DH), lambda h, bi: (bi, 0, v_off + h)),
            pl.BlockSpec((1, l, NA_DH), lambda h, bi: (bi, 0, ck_off + h)),
            pl.BlockSpec((1, l, NA_DH), lambda h, bi: (bi, 0, cv_off + h)),
            pl.BlockSpec((1, 2 * NA_KR, GRID_W, 2 * GRID_W), lambda h, bi: (h, 0, 0, 0)),
        ],
        out_specs=pl.BlockSpec((1, s, NA_DH), lambda h, bi: (bi, 0, h)),
        scratch_shapes=[pltpu.VMEM((3, nq, nk), F32)],
        compiler_params=_params(("arbitrary", "arbitrary"), 48),
        name="natt",
    )(proj, proj, proj, projc, projc, pairs)


def _merge_kernel(a_ref, n_ref, ga_ref, gb_ref, wa_ref, wb_ref, o_ref):
    ra = jnp.dot(a_ref[...], wa_ref[...], preferred_element_type=F32)
    rn = jnp.dot(n_ref[...], wb_ref[...], preferred_element_type=F32)
    o_ref[...] = (_sigmoid(ga_ref[...].astype(F32)) * ra + _sigmoid(gb_ref[...].astype(F32)) * rn).astype(o_ref.dtype)


def _merge(ret_in, na_in, proj2d, w_pa, w_pb):
    t, d = ret_in.shape
    tm, tn = MERGE_TM, MERGE_TN
    ga_off = (proj2d.shape[1] - 2 * d) // tn
    gb_off = (proj2d.shape[1] - d) // tn
    return pl.pallas_call(
        _merge_kernel,
        out_shape=jax.ShapeDtypeStruct((t, d), BF16),
        grid=(t // tm, d // tn),
        in_specs=[
            pl.BlockSpec((tm, ret_in.shape[1]), lambda i, j: (i, 0)),
            pl.BlockSpec((tm, na_in.shape[1]), lambda i, j: (i, 0)),
            pl.BlockSpec((tm, tn), lambda i, j: (i, ga_off + j)),
            pl.BlockSpec((tm, tn), lambda i, j: (i, gb_off + j)),
            pl.BlockSpec((w_pa.shape[0], tn), lambda i, j: (0, j)),
            pl.BlockSpec((w_pb.shape[0], tn), lambda i, j: (0, j)),
        ],
        out_specs=pl.BlockSpec((tm, tn), lambda i, j: (i, j)),
        compiler_params=_params(("arbitrary", "arbitrary"), 48),
        name="merge",
    )(ret_in, na_in, proj2d, proj2d, w_pa, w_pb)


def _pack_bf16_pairs(lo, hi):
    lo_bits = lax.bitcast_convert_type(lo.astype(F32), jnp.uint32)
    hi_bits = lax.bitcast_convert_type(hi.astype(F32), jnp.uint32)
    return (lo_bits >> 16) | (hi_bits & jnp.uint32(0xFFFF0000))


def _unpack_bf16_pairs(words):
    lo = lax.bitcast_convert_type(words << 16, F32).astype(BF16)
    hi = lax.bitcast_convert_type(words & jnp.uint32(0xFFFF0000), F32).astype(BF16)
    return lo, hi


def _oproj_kernel(m_ref, x_ref, g1_ref, ng_ref, sh_ref, sc_ref, wo_ref, wrh_ref, wrl_ref, br_ref,
                  x1_ref, h2_ref, lg_ref):
    y = jnp.dot(m_ref[...], wo_ref[...], preferred_element_type=F32)
    x1 = x_ref[0] + g1_ref[0] * y
    x1_ref[0] = x1
    ms = jnp.mean(x1 * x1, axis=-1, keepdims=True)
    h2 = x1 * lax.rsqrt(ms + NORM_EPS) * ng_ref[...]
    h2 = h2 * (1.0 + sc_ref[0]) + sh_ref[0]
    h2_hi = h2.astype(BF16)
    half = h2.shape[1] // 2
    h2_ref[...] = _pack_bf16_pairs(h2_hi[:, :half], h2_hi[:, half:])
    h2_lo = (h2 - h2_hi.astype(F32)).astype(BF16)
    lg = jnp.dot(h2_hi, wrh_ref[...], preferred_element_type=F32)
    lg += jnp.dot(h2_lo, wrh_ref[...], preferred_element_type=F32)
    lg += jnp.dot(h2_hi, wrl_ref[...], preferred_element_type=F32)
    lg_ref[...] = lg + br_ref[...]


def _oproj(m, x, g1, norm_g, sh2, sc2, w_o, w_r, b_r):
    b, s, d = x.shape
    tm = OPROJ_TM
    spt = s // tm
    w_r_hi = w_r.astype(BF16)
    w_r_lo = (w_r - w_r_hi.astype(F32)).astype(BF16)
    return pl.pallas_call(
        _oproj_kernel,
        out_shape=(
            jax.ShapeDtypeStruct((b, s, d), F32),
            jax.ShapeDtypeStruct((b * s, d // 2), jnp.uint32),
            jax.ShapeDtypeStruct((b * s, LANES), F32),
        ),
        grid=(b, spt),
        in_specs=[
            pl.BlockSpec((tm, d), lambda bi, i: (bi * spt + i, 0)),
            pl.BlockSpec((1, tm, d), lambda bi, i: (bi, i, 0)),
            pl.BlockSpec((1, 1, d), lambda bi, i: (bi, 0, 0)),
            pl.BlockSpec((1, d), lambda bi, i: (0, 0)),
            pl.BlockSpec((1, 1, d), lambda bi, i: (bi, 0, 0)),
            pl.BlockSpec((1, 1, d), lambda bi, i: (bi, 0, 0)),
            pl.BlockSpec((d, d), lambda bi, i: (0, 0)),
            pl.BlockSpec((d, LANES), lambda bi, i: (0, 0)),
            pl.BlockSpec((d, LANES), lambda bi, i: (0, 0)),
            pl.BlockSpec((1, LANES), lambda bi, i: (0, 0)),
        ],
        out_specs=(
            pl.BlockSpec((1, tm, d), lambda bi, i: (bi, i, 0)),
            pl.BlockSpec((tm, d // 2), lambda bi, i: (bi * spt + i, 0)),
            pl.BlockSpec((tm, LANES), lambda bi, i: (bi * spt + i, 0)),
        ),
        compiler_params=_params(("arbitrary", "arbitrary"), 48),
        name="oproj",
    )(m, x, g1, norm_g, sh2, sc2, w_o, w_r_hi, w_r_lo, b_r)


def _route_kernel(lg_ref, e_ref, w_ref, r_ref, cnt_ref, run_scr):
    i = pl.program_id(0)
    tm = lg_ref.shape[0]

    @pl.when(i == 0)
    def _():
        run_scr[...] = jnp.zeros_like(run_scr)

    l = lg_ref[...]
    lane = lax.broadcasted_iota(jnp.int32, l.shape, 1)
    vals, idxs, hots = [], [], []
    for _ in range(TOP_K):
        m = jnp.max(l, axis=-1, keepdims=True)
        idx = jnp.min(jnp.where(l == m, lane, LANES), axis=-1, keepdims=True)
        hot = lane == idx
        l = jnp.where(hot, -jnp.inf, l)
        vals.append(m)
        idxs.append(idx)
        hots.append(hot)
    exps = [jnp.exp(v - vals[0]) for v in vals]
    tot = exps[0]
    for e in exps[1:]:
        tot = tot + e

    member = hots[0]
    for hot in hots[1:]:
        member = member | hot
    member = member.astype(F32)
    ri = lax.broadcasted_iota(jnp.int32, (tm, tm), 0)
    ci = lax.broadcasted_iota(jnp.int32, (tm, tm), 1)
    lower = (ci < ri).astype(BF16)
    before = jnp.dot(lower, member.astype(BF16), preferred_element_type=F32) + run_scr[...]

    e_out = jnp.zeros(l.shape, jnp.int32)
    w_out = jnp.zeros(l.shape, F32)
    r_out = jnp.zeros(l.shape, jnp.int32)
    for k in range(TOP_K):
        rank = jnp.sum(jnp.where(hots[k], before, 0.0), axis=-1, keepdims=True).astype(jnp.int32)
        e_out = jnp.where(lane == k, idxs[k], e_out)
        w_out = jnp.where(lane == k, exps[k] / tot, w_out)
        r_out = jnp.where(lane == k, rank, r_out)
    e_ref[...] = e_out
    w_ref[...] = w_out
    r_ref[...] = r_out
    run_scr[...] += jnp.sum(member, axis=0, keepdims=True)
    cnt_ref[...] = run_scr[...].astype(jnp.int32)


def _route(logits):
    t = logits.shape[0]
    tm = ROUTE_TM
    row = pl.BlockSpec((tm, LANES), lambda i: (i, 0))
    return pl.pallas_call(
        _route_kernel,
        out_shape=(
            jax.ShapeDtypeStruct((t, LANES), jnp.int32),
            jax.ShapeDtypeStruct((t, LANES), F32),
            jax.ShapeDtypeStruct((t, LANES), jnp.int32),
            jax.ShapeDtypeStruct((1, LANES), jnp.int32),
        ),
        grid=(t // tm,),
        in_specs=[row],
        out_specs=(row, row, row, pl.BlockSpec((1, LANES), lambda i: (0, 0))),
        scratch_shapes=[pltpu.VMEM((1, LANES), F32)],
        compiler_params=_params(("arbitrary",), 32),
        name="route",
    )(logits)


def _dispatch_kernel(dest_ref, h_ref, xb_ref, sem):
    tm = h_ref.shape[0]

    def row_copy(t, k):
        return pltpu.make_async_copy(h_ref.at[pl.ds(t, 1)], xb_ref.at[pl.ds(dest_ref[TOP_K * t + k], 1)], sem)

    def issue(t, carry):
        for k in range(TOP_K):
            row_copy(t, k).start()
        return carry

    lax.fori_loop(0, tm, issue, 0)
    for k in range(TOP_K):
        pltpu.make_async_copy(h_ref, xb_ref.at[pl.ds(0, tm)], sem).wait()


def _dispatch(h2, dest_flat):
    t, d = h2.shape
    tm = DISPATCH_TM
    return pl.pallas_call(
        _dispatch_kernel,
        out_shape=jax.ShapeDtypeStruct((t * TOP_K, d), h2.dtype),
        grid=(t // tm,),
        in_specs=[
            pl.BlockSpec((tm * TOP_K,), lambda i: (i,), memory_space=pltpu.SMEM),
            pl.BlockSpec((tm, d), lambda i: (i, 0)),
        ],
        out_specs=pl.BlockSpec(memory_space=pl.ANY),
        scratch_shapes=[pltpu.SemaphoreType.DMA],
        compiler_params=_params(("arbitrary",), 32),
        name="dispatch",
    )(dest_flat, h2)


def _expert_kernel(tile_ref, exp_ref, lo_ref, hi_ref, first_ref,
                   x_ref, wg_ref, bg_ref, wu_ref, bu_ref, wd_ref, bd_ref, o_ref, xs_scr):
    w = pl.program_id(0)
    j = pl.program_id(1)
    lo = lo_ref[w]
    hi = hi_ref[w]
    sub = EXPERT_SUB
    half = x_ref.shape[1]

    @pl.when(j == 0)
    def _():
        x_lo, x_hi = _unpack_bf16_pairs(x_ref[...])
        xs_scr[:, :half] = x_lo
        xs_scr[:, half:] = x_hi

    @pl.when((j == 0) & (first_ref[w] == 1))
    def _():
        o_ref[...] = jnp.zeros_like(o_ref)

    def ffn_tile(s):
        rows = slice(s * sub, (s + 1) * sub)
        xs = xs_scr[rows, :]
        gate = jnp.dot(xs, wg_ref[0].astype(BF16), preferred_element_type=F32) + bg_ref[0]
        up = jnp.dot(xs, wu_ref[0].astype(BF16), preferred_element_type=F32) + bu_ref[0]
        gate = jnp.minimum(gate, SWIGLU_LIMIT)
        up = jnp.clip(up, -SWIGLU_LIMIT, SWIGLU_LIMIT)
        act = gate * _sigmoid(SWIGLU_ALPHA * gate) * (up + 1.0)
        row = lax.broadcasted_iota(jnp.int32, (sub, 1), 0) + s * sub
        mine = (row >= lo) & (row < hi)
        act = jnp.where(mine, act, 0.0).astype(BF16)
        y = jnp.dot(act, wd_ref[0].astype(BF16), preferred_element_type=F32)
        o_ref[rows, :] += y + jnp.where(mine & (j == 0), bd_ref[0], 0.0)

    for p in range(x_ref.shape[0] // (2 * sub)):
        s0, s1 = 2 * p, 2 * p + 1
        act0 = (lo < (s0 + 1) * sub) & (hi > s0 * sub)
        act1 = (lo < (s1 + 1) * sub) & (hi > s1 * sub)

        @pl.when(act0 & act1)
        def _(s0=s0, s1=s1):
            ffn_tile(s0)
            ffn_tile(s1)

        @pl.when(act0 & jnp.logical_not(act1))
        def _(s0=s0):
            ffn_tile(s0)

        @pl.when(act1 & jnp.logical_not(act0))
        def _(s1=s1):
            ffn_tile(s1)


def _experts(xb, items, w_gate, b_gate, w_up, b_up, w_down, b_down):
    r, half = xb.shape
    e, d, f = w_gate.shape
    tm, tf = EXPERT_TM, EXPERT_TF
    nj = f // tf
    tile, expert, lo, hi, first = items
    n_items = tile.shape[0]

    def jj(j, w, hi_ref, lo_ref):
        return jnp.where(hi_ref[w] > lo_ref[w], j, nj - 1)

    return pl.pallas_call(
        _expert_kernel,
        out_shape=jax.ShapeDtypeStruct((r, d), F32),
        grid_spec=pltpu.PrefetchScalarGridSpec(
            num_scalar_prefetch=5,
            grid=(n_items, nj),
            in_specs=[
                pl.BlockSpec((tm, half), lambda w, j, ti, ex, lo_, hi_, fi: (ti[w], 0)),
                pl.BlockSpec((1, d, tf), lambda w, j, ti, ex, lo_, hi_, fi: (ex[w], 0, jj(j, w, hi_, lo_))),
                pl.BlockSpec((1, 1, tf), lambda w, j, ti, ex, lo_, hi_, fi: (ex[w], 0, jj(j, w, hi_, lo_))),
                pl.BlockSpec((1, d, tf), lambda w, j, ti, ex, lo_, hi_, fi: (ex[w], 0, jj(j, w, hi_, lo_))),
                pl.BlockSpec((1, 1, tf), lambda w, j, ti, ex, lo_, hi_, fi: (ex[w], 0, jj(j, w, hi_, lo_))),
                pl.BlockSpec((1, tf, d), lambda w, j, ti, ex, lo_, hi_, fi: (ex[w], jj(j, w, hi_, lo_), 0)),
                pl.BlockSpec((1, 1, d), lambda w, j, ti, ex, lo_, hi_, fi: (ex[w], 0, 0)),
            ],
            out_specs=pl.BlockSpec((tm, d), lambda w, j, ti, ex, lo_, hi_, fi: (ti[w], 0)),
            scratch_shapes=[pltpu.VMEM((tm, d), BF16)],
        ),
        compiler_params=_params(("arbitrary", "arbitrary"), 60),
        name="experts",
    )(tile, expert, lo, hi, first, xb, w_gate, b_gate.reshape(e, 1, f), w_up, b_up.reshape(e, 1, f),
      w_down, b_down.reshape(e, 1, d))


def _work_items(counts, n_rows):
    tm = EXPERT_TM
    n_tiles = n_rows // tm
    n_items = n_tiles + N_EXPERTS - 1
    cum = jnp.cumsum(counts)
    start = cum - counts
    tile_lo = jnp.arange(n_tiles, dtype=jnp.int32) * tm
    e_lo = jnp.searchsorted(cum, tile_lo, side="right").astype(jnp.int32)
    e_hi = jnp.searchsorted(cum, tile_lo + tm - 1, side="right").astype(jnp.int32)
    per_tile = e_hi - e_lo + 1
    off = jnp.cumsum(per_tile) - per_tile
    total = jnp.sum(per_tile)
    w = jnp.arange(n_items, dtype=jnp.int32)
    valid = w < total
    tile = jnp.clip(jnp.searchsorted(off, w, side="right").astype(jnp.int32) - 1, 0, n_tiles - 1)
    expert = jnp.where(valid, e_lo[tile] + w - off[tile], e_hi[n_tiles - 1])
    tile = jnp.where(valid, tile, n_tiles - 1)
    lo = jnp.clip(start[expert] - tile * tm, 0, tm)
    hi = jnp.clip(cum[expert] - tile * tm, 0, tm)
    hi = jnp.where(valid, jnp.maximum(hi, lo), lo)
    first = (valid & (w == off[tile])).astype(jnp.int32)
    return tile, expert.astype(jnp.int32), lo.astype(jnp.int32), hi.astype(jnp.int32), first, start


def _combine_kernel(dest_ref, dnext_ref, yb_ref, w_ref, x1_ref, g2_ref, fg_ref, o_ref, buf, sem):
    i = pl.program_id(0)
    tm = x1_ref.shape[0]
    slot = i % 2

    def gather(d_ref, into):
        def issue(t, carry):
            for k in range(TOP_K):
                pltpu.make_async_copy(yb_ref.at[pl.ds(d_ref[TOP_K * t + k], 1)], buf.at[into, k, pl.ds(t, 1)],
                                      sem.at[into]).start()
            return carry

        lax.fori_loop(0, tm, issue, 0)

    @pl.when(i == 0)
    def _():
        gather(dest_ref, 0)

    @pl.when(i + 1 < pl.num_programs(0))
    def _():
        gather(dnext_ref, 1 - slot)

    for k in range(TOP_K):
        pltpu.make_async_copy(yb_ref.at[pl.ds(0, tm)], buf.at[slot, k], sem.at[slot]).wait()

    wts = w_ref[...]
    moe = buf[slot, 0] * wts[:, 0:1]
    for k in range(1, TOP_K):
        moe += buf[slot, k] * wts[:, k:k + 1]
    x2 = x1_ref[...] + g2_ref[0] * moe
    ms = jnp.mean(x2 * x2, axis=-1, keepdims=True)
    o_ref[...] = x2 * lax.rsqrt(ms + NORM_EPS) * fg_ref[...]


def _combine(yb, dest_flat, top_w, x1, g2, final_g):
    b, s, d = x1.shape
    tm = COMBINE_TM
    spt = s // tm
    n = b * spt
    out = pl.pallas_call(
        _combine_kernel,
        out_shape=jax.ShapeDtypeStruct((b * s, d), F32),
        grid=(n,),
        in_specs=[
            pl.BlockSpec((tm * TOP_K,), lambda i: (i,), memory_space=pltpu.SMEM),
            pl.BlockSpec((tm * TOP_K,), lambda i: (jnp.minimum(i + 1, n - 1),), memory_space=pltpu.SMEM),
            pl.BlockSpec(memory_space=pl.ANY),
            pl.BlockSpec((tm, LANES), lambda i: (i, 0)),
            pl.BlockSpec((tm, d), lambda i: (i, 0)),
            pl.BlockSpec((1, 1, d), lambda i: (i // spt, 0, 0)),
            pl.BlockSpec((1, d), lambda i: (0, 0)),
        ],
        out_specs=pl.BlockSpec((tm, d), lambda i: (i, 0)),
        scratch_shapes=[pltpu.VMEM((2, TOP_K, tm, d), F32), pltpu.SemaphoreType.DMA((2,))],
        compiler_params=_params(("arbitrary",), 40),
        name="combine",
    )(dest_flat, dest_flat, yb, top_w, x1.reshape(b * s, d), g2, final_g)
    return out.reshape(b, s, d)


def kernel(x, c, ctx, c_ctx, ada_w, ada_b, norm1_g, norm2_g, w_in, w_pa, w_pb, w_o, ret_decay_fwd, ret_decay_bwd,
           na_rpb, w_router, b_router, w_gate, b_gate, w_up, b_up, w_down, b_down, final_g):
    assert ada_w.shape[0] == 1, "single layer"
    b, s, d = x.shape
    l = ctx.shape[1]
    in_w = w_in.shape[2]
    rows = s // GRID_W
    assert s % IN_TM == 0 and l % RET_CHUNK == 0 and rows >= NA_WROWS and rows % NA_QROWS == 0

    c_rows = jnp.zeros((16, d), F32).at[:b].set(c).at[b].set(c_ctx)
    mod = _ada(c_rows, ada_w[0], ada_b[0][None, :])
    sh1, sc1, g1, sh2, sc2, g2 = [mod[:b, None, i * d:(i + 1) * d] for i in range(6)]
    shc1 = jnp.broadcast_to(mod[b, 0 * d:1 * d][None, None, :], (b, 1, d))
    scc1 = jnp.broadcast_to(mod[b, 1 * d:2 * d][None, None, :], (b, 1, d))

    w_in_bf = w_in[0].astype(BF16)
    n1 = norm1_g[0][None, :]
    proj = _inproj(x, n1, sh1, sc1, w_in_bf, tuple(range(in_w // IN_TN)), IN_TM)
    qk_w = RET_HEADS * RET_DK
    v_w = RET_HEADS * RET_DV
    na_w = NA_HEADS * NA_DH
    ctx_cols = tuple(range(qk_w // IN_TN, (2 * qk_w + v_w) // IN_TN)) + tuple(
        range((2 * qk_w + 2 * v_w + na_w) // IN_TN, (2 * qk_w + 2 * v_w + 3 * na_w) // IN_TN))
    projc = _inproj(ctx, n1, shc1, scc1, w_in_bf, ctx_cols, l)

    lgf = jax.nn.log_sigmoid(ret_decay_fwd[0].astype(F32))
    lgb = jax.nn.log_sigmoid(ret_decay_bwd[0].astype(F32))
    cos_t, sin_t = _rope_tables(s)
    ret_in = _retention(proj, projc, lgf, lgb, cos_t, sin_t)
    na_in = _natt(proj, projc, _natt_pair_tiles(na_rpb[0]))

    t = b * s
    m = _merge(ret_in.reshape(t, -1), na_in.reshape(t, -1), proj.reshape(t, in_w),
               w_pa[0].astype(BF16), w_pb[0].astype(BF16))

    w_r = jnp.zeros((d, LANES), F32).at[:, :N_EXPERTS].set(w_router[0])
    b_r = jnp.full((1, LANES), NEG_INF, F32).at[0, :N_EXPERTS].set(b_router[0])
    x1, h2, logits = _oproj(m, x, g1, norm2_g[0][None, :], sh2, sc2, w_o[0].astype(BF16), w_r, b_r)

    top_e, top_w, rank, counts = _route(logits)
    counts = counts[0, :N_EXPERTS]
    items = _work_items(counts, t * TOP_K)
    start = items[5]
    dest = (start[top_e[:, :TOP_K]] + rank[:, :TOP_K]).reshape(-1)

    xb = _dispatch(h2, dest)
    yb = _experts(xb, items[:5], w_gate[0], b_gate[0], w_up[0], b_up[0], w_down[0], b_down[0])
    return _combine(yb, dest, top_w, x1, g2, final_g[None, :])
```

```python
import functools

import jax
import jax.numpy as jnp
import numpy as np
from jax import lax
from jax.experimental import pallas as pl
from jax.experimental.pallas import tpu as pltpu

F32 = jnp.float32
BF16 = jnp.bfloat16

GRID_W = 64
RET_HEADS = 8
RET_DK = 128
RET_DV = 256
RET_CHUNK = 128
ROPE_AXIS_DIM = RET_DK // 2
ROPE_BASE = 10000.0
NA_HEADS = 16
NA_DH = 128
NA_KR = 8
NA_KC = 16
N_EXPERTS = 32
TOP_K = 4
SWIGLU_ALPHA = 1.702
SWIGLU_LIMIT = 7.0
NORM_EPS = 1e-6
GN_EPS = 1e-5
NEG_INF = -1e30
LOG2_E = 1.4426950408889634

LANES = 128
MIB = 1024 * 1024

NA_QROWS = 4
NA_WROWS = NA_QROWS + NA_KR

IN_TM, IN_TN = 1024, 512
MERGE_TM, MERGE_TN = 1024, 512
OPROJ_TM = 256
ROUTE_TM = 512
DISPATCH_TM = 256
EXPERT_TM, EXPERT_SUB, EXPERT_TF = 1024, 256, 512
COMBINE_TM = 128
COMBINE_SLOTS = 3


def _params(semantics, vmem_mib):
    return pltpu.CompilerParams(dimension_semantics=semantics, vmem_limit_bytes=vmem_mib * MIB)


def _sigmoid(x):
    return 1.0 / (1.0 + jnp.exp(-x))


def _ada_kernel(c_ref, w_ref, b_ref, o_ref):
    c = c_ref[...]
    s = c * _sigmoid(c)
    o_ref[...] = jnp.dot(s.astype(BF16), w_ref[...].astype(BF16), preferred_element_type=F32) + b_ref[...]


def _ada(c_rows, w, b):
    r, d = c_rows.shape
    n = w.shape[1]
    tn = 1024
    return pl.pallas_call(
        _ada_kernel,
        out_shape=jax.ShapeDtypeStruct((r, n), F32),
        grid=(n // tn,),
        in_specs=[
            pl.BlockSpec((r, d), lambda j: (0, 0)),
            pl.BlockSpec((d, tn), lambda j: (0, j)),
            pl.BlockSpec((1, tn), lambda j: (0, j)),
        ],
        out_specs=pl.BlockSpec((r, tn), lambda j: (0, j)),
        compiler_params=_params(("arbitrary",), 40),
        name="ada",
    )(c_rows, w, b)


def _inproj_kernel(cols_ref, x_ref, g_ref, sh_ref, sc_ref, w_ref, o_ref, h_scr):
    @pl.when(pl.program_id(2) == 0)
    def _():
        x = x_ref[0]
        ms = jnp.mean(x * x, axis=-1, keepdims=True)
        y = x * lax.rsqrt(ms + NORM_EPS) * g_ref[...]
        h_scr[...] = (y * (1.0 + sc_ref[0]) + sh_ref[0]).astype(BF16)

    o_ref[0] = jnp.dot(h_scr[...], w_ref[...], preferred_element_type=F32).astype(o_ref.dtype)


def _inproj(x, g, shift, scale, w, col_tiles, tm):
    b, n, d = x.shape
    tn = IN_TN
    nct = len(col_tiles)
    cols = jnp.asarray(np.asarray(col_tiles, np.int32))
    return pl.pallas_call(
        _inproj_kernel,
        out_shape=jax.ShapeDtypeStruct((b, n, nct * tn), BF16),
        grid_spec=pltpu.PrefetchScalarGridSpec(
            num_scalar_prefetch=1,
            grid=(b, n // tm, nct),
            in_specs=[
                pl.BlockSpec((1, tm, d), lambda bi, i, j, c: (bi, i, 0)),
                pl.BlockSpec((1, d), lambda bi, i, j, c: (0, 0)),
                pl.BlockSpec((1, 1, d), lambda bi, i, j, c: (bi, 0, 0)),
                pl.BlockSpec((1, 1, d), lambda bi, i, j, c: (bi, 0, 0)),
                pl.BlockSpec((d, tn), lambda bi, i, j, c: (0, c[j])),
            ],
            out_specs=pl.BlockSpec((1, tm, tn), lambda bi, i, j, c: (bi, i, j)),
            scratch_shapes=[pltpu.VMEM((tm, d), BF16)],
        ),
        compiler_params=_params(("arbitrary", "arbitrary", "arbitrary"), 48),
        name="inproj",
    )(cols, x, g, shift, scale, w)


def _swap_halves(x):
    lane = lax.broadcasted_iota(jnp.int32, x.shape, 1)
    return jnp.where(lane % 64 < 32, pltpu.roll(x, 96, 1), pltpu.roll(x, 32, 1))


def _ret_kernel(lgf_ref, lgb_ref, q_ref, k_ref, v_ref, g_ref, kc_ref, vc_ref, cos_ref, sin_ref,
                o_ref, ks_scr, rf_scr, rb_scr, ub_scr):
    h = pl.program_id(1)
    c = RET_CHUNK
    n = q_ref.shape[1]
    nc = n // c
    ncc = kc_ref.shape[1] // c
    lgf = lgf_ref[h]
    lgb = lgb_ref[h]
    k_scale = RET_DK ** -0.5

    pos_c = lax.broadcasted_iota(jnp.int32, (c, 1), 0).astype(F32)
    zeta_f = jnp.exp((c - 1.0 - pos_c) * lgf)
    zeta_b = jnp.exp(pos_c * lgb)
    xi_f = jnp.exp((pos_c + 1.0) * lgf)
    xi_b = jnp.exp((c - pos_c) * lgb)
    one = jnp.ones((1, 1), F32)
    gc_f = jnp.exp(one * (c * lgf))
    gc_b = jnp.exp(one * (c * lgb))
    ii = lax.broadcasted_iota(jnp.int32, (c, c), 0)
    jj = lax.broadcasted_iota(jnp.int32, (c, c), 1)
    diff = (ii - jj).astype(F32)
    dmask = (jnp.where(diff >= 0, jnp.exp(jnp.maximum(diff, 0.0) * lgf), 0.0)
             + jnp.where(diff <= 0, jnp.exp(jnp.maximum(-diff, 0.0) * lgb), 0.0))

    def ktv(k_bf, v_f32, zeta):
        return jnp.dot(k_bf.astype(F32).T.astype(BF16), (v_f32 * zeta).astype(BF16), preferred_element_type=F32)

    r_f = jnp.zeros((RET_DK, RET_DV), F32)
    for i in range(ncc):
        kc = (kc_ref[0, i * c:(i + 1) * c, :].astype(F32) * k_scale).astype(BF16)
        r_f = gc_f * r_f + ktv(kc, vc_ref[0, i * c:(i + 1) * c, :].astype(F32), zeta_f)
    r_b = jnp.zeros((RET_DK, RET_DV), F32)
    for i in reversed(range(ncc)):
        kc = (kc_ref[0, i * c:(i + 1) * c, :].astype(F32) * k_scale).astype(BF16)
        r_b = gc_b * r_b + ktv(kc, vc_ref[0, i * c:(i + 1) * c, :].astype(F32), zeta_b)

    def rope(x, rows):
        x = x.astype(F32)
        return x * cos_ref[rows, :] + _swap_halves(x) * sin_ref[rows, :]

    def chunk_updates(i, r):
        rows = pl.ds(pl.multiple_of(i * c, c), c)
        kb = (rope(k_ref[0, rows, :], rows) * k_scale).astype(BF16)
        ks_scr[rows, :] = kb
        kt = kb.astype(F32).T.astype(BF16)
        v = v_ref[0, rows, :].astype(F32)
        rf_scr[i] = r.astype(BF16)
        ub_scr[i] = jnp.dot(kt, (v * zeta_b).astype(BF16), preferred_element_type=F32)
        return gc_f * r + jnp.dot(kt, (v * zeta_f).astype(BF16), preferred_element_type=F32)

    lax.fori_loop(0, nc, chunk_updates, r_f, unroll=8)

    def bwd_scan(t, r):
        i = nc - 1 - t
        rb_scr[i] = r.astype(BF16)
        return gc_b * r + ub_scr[i]

    lax.fori_loop(0, nc, bwd_scan, r_b, unroll=2)

    def out_chunk(i, carry):
        rows = pl.ds(pl.multiple_of(i * c, c), c)
        q = rope(q_ref[0, rows, :], rows)
        kb = ks_scr[rows, :]
        vb = v_ref[0, rows, :]
        s = lax.dot_general(q.astype(BF16), kb, (((1,), (1,)), ((), ())), preferred_element_type=F32) * dmask
        o = jnp.dot(s.astype(BF16), vb, preferred_element_type=F32)
        o += jnp.dot((q * xi_f).astype(BF16), rf_scr[i], preferred_element_type=F32)
        o += jnp.dot((q * xi_b).astype(BF16), rb_scr[i], preferred_element_type=F32)
        mu = jnp.mean(o, axis=-1, keepdims=True)
        d = o - mu
        var = jnp.mean(d * d, axis=-1, keepdims=True)
        on = d * lax.rsqrt(var + GN_EPS)
        g = g_ref[0, rows, :].astype(F32)
        o_ref[0, rows, :] = (g * _sigmoid(g) * on).astype(o_ref.dtype)
        return carry

    lax.fori_loop(0, nc, out_chunk, 0, unroll=8)


def _retention(proj, projc, lgf, lgb, cos_t, sin_t):
    b, s, _ = proj.shape
    l = projc.shape[1]
    hq = RET_HEADS
    v_off = 2 * hq * RET_DK // RET_DV
    g_off = v_off + hq
    cv_off = hq * RET_DK // RET_DV
    smem = pl.BlockSpec(memory_space=pltpu.SMEM)
    return pl.pallas_call(
        _ret_kernel,
        out_shape=jax.ShapeDtypeStruct((b, s, hq * RET_DV), BF16),
        grid=(b, hq),
        in_specs=[
            smem, smem,
            pl.BlockSpec((1, s, RET_DK), lambda bi, h: (bi, 0, h)),
            pl.BlockSpec((1, s, RET_DK), lambda bi, h: (bi, 0, hq + h)),
            pl.BlockSpec((1, s, RET_DV), lambda bi, h: (bi, 0, v_off + h)),
            pl.BlockSpec((1, s, RET_DV), lambda bi, h: (bi, 0, g_off + h)),
            pl.BlockSpec((1, l, RET_DK), lambda bi, h: (bi, 0, h)),
            pl.BlockSpec((1, l, RET_DV), lambda bi, h: (bi, 0, cv_off + h)),
            pl.BlockSpec((s, RET_DK), lambda bi, h: (0, 0)),
            pl.BlockSpec((s, RET_DK), lambda bi, h: (0, 0)),
        ],
        out_specs=pl.BlockSpec((1, s, RET_DV), lambda bi, h: (bi, 0, h)),
        scratch_shapes=[
            pltpu.VMEM((s, RET_DK), BF16),
            pltpu.VMEM((s // RET_CHUNK, RET_DK, RET_DV), BF16),
            pltpu.VMEM((s // RET_CHUNK, RET_DK, RET_DV), BF16),
            pltpu.VMEM((s // RET_CHUNK, RET_DK, RET_DV), F32),
        ],
        compiler_params=_params(("arbitrary", "arbitrary"), 48),
        name="ret",
    )(lgf, lgb, proj, proj, proj, proj, projc, projc, cos_t, sin_t)


def _rope_tables(n):
    t = jnp.arange(n)
    row = (t // GRID_W).astype(F32)
    col = (t % GRID_W).astype(F32)
    inv = ROPE_BASE ** (-jnp.arange(0, ROPE_AXIS_DIM, 2, dtype=F32) / ROPE_AXIS_DIM)
    ar = row[:, None] * inv
    ac = col[:, None] * inv
    cos_t = jnp.concatenate([jnp.cos(ar), jnp.cos(ar), jnp.cos(ac), jnp.cos(ac)], axis=1)
    sin_t = jnp.concatenate([-jnp.sin(ar), jnp.sin(ar), -jnp.sin(ac), jnp.sin(ac)], axis=1)
    return cos_t, sin_t


def _natt_kernel(q_ref, k_ref, v_ref, kc_ref, vc_ref, pair_ref, o_ref, bias_scr):
    @pl.when(pl.program_id(1) == 0)
    def _():
        _natt_fill_bias(pair_ref, bias_scr)

    n = q_ref.shape[1]
    rows = n // GRID_W
    n_blk = rows // NA_QROWS
    nq = NA_QROWS * GRID_W
    nk = NA_WROWS * GRID_W
    scale = NA_DH ** -0.5
    kc = kc_ref[0]
    vc = vc_ref[0]
    nt = (((1,), (1,)), ((), ()))

    def block(blk, carry):
        ws = jnp.clip(blk * NA_QROWS - NA_KR // 2, 0, rows - NA_WROWS)
        variant = jnp.where(blk == 0, 0, jnp.where(blk == n_blk - 1, 2, 1))
        qrows = pl.ds(pl.multiple_of(blk * nq, nq), nq)
        krows = pl.ds(pl.multiple_of(ws * GRID_W, GRID_W), nk)
        q = q_ref[0, qrows, :]
        s_loc = lax.dot_general(q, k_ref[0, krows, :], nt, preferred_element_type=F32) + bias_scr[variant]
        s_ctx = lax.dot_general(q, kc, nt, preferred_element_type=F32)
        m = jnp.maximum(jnp.max(s_loc, axis=-1, keepdims=True), jnp.max(s_ctx, axis=-1, keepdims=True))
        p_loc = jnp.exp2((s_loc - m) * (scale * LOG2_E))
        p_ctx = jnp.exp2((s_ctx - m) * (scale * LOG2_E))
        denom = jnp.sum(p_loc, axis=-1, keepdims=True) + jnp.sum(p_ctx, axis=-1, keepdims=True)
        o = jnp.dot(p_loc.astype(BF16), v_ref[0, krows, :], preferred_element_type=F32)
        o += jnp.dot(p_ctx.astype(BF16), vc, preferred_element_type=F32)
        o_ref[0, qrows, :] = (o / denom).astype(o_ref.dtype)
        return carry

    lax.fori_loop(0, n_blk, block, 0, unroll=4)


def _natt_row_offsets():
    rq = np.arange(NA_QROWS)[:, None]
    wr = np.arange(NA_WROWS)[None, :]
    dr0 = np.where(wr < NA_KR, wr - rq + NA_KR - 1, -1)
    dr1 = np.where((wr >= rq) & (wr < rq + NA_KR), wr - rq + NA_KR // 2 - 1, -1)
    dr2 = np.where(wr >= NA_QROWS, wr - rq - 1, -1)
    return np.stack([dr0, dr1, dr2])


def _natt_fill_bias(pair_ref, bias_scr):
    w = GRID_W
    dr = _natt_row_offsets()
    neg = jnp.full((w, 2 * w), NEG_INF, F32)
    lane_row = lax.broadcasted_iota(jnp.int32, (w, NA_WROWS * w), 1) // w
    for kind in range(dr.shape[0]):
        for rq in range(NA_QROWS):
            tiles = []
            for p in range(NA_WROWS // 2):
                a, b_ = int(dr[kind, rq, 2 * p]), int(dr[kind, rq, 2 * p + 1])
                if a < 0 and b_ < 0:
                    tiles.append(neg)
                else:
                    tiles.append(pair_ref[0, b_ if b_ >= 0 else a + 1])
            strip = jnp.concatenate(tiles, axis=1)
            valid = np.nonzero(dr[kind, rq] >= 0)[0]
            keep = (lane_row >= int(valid[0])) & (lane_row <= int(valid[-1]))
            bias_scr[kind, rq * w:(rq + 1) * w, :] = jnp.where(keep, strip, NEG_INF)


def _natt_pair_tiles(rpb):
    w = GRID_W
    qc = np.arange(w)[:, None]
    kcol = np.arange(w)[None, :]
    cs = np.clip(qc - NA_KC // 2, 0, w - NA_KC)
    col_ok = (kcol >= cs) & (kcol < cs + NA_KC)
    dc_idx = np.clip(kcol - qc, -(NA_KC - 1), NA_KC - 1) + NA_KC - 1
    t = jnp.where(col_ok[None, None], rpb.astype(F32)[:, :, dc_idx] * NA_DH ** 0.5, NEG_INF)
    neg = jnp.full((rpb.shape[0], 1, w, w), NEG_INF, F32)
    return jnp.concatenate([jnp.concatenate([neg, t], axis=1), jnp.concatenate([t, neg], axis=1)], axis=3)


def _natt(proj, projc, pairs):
    b, s, _ = proj.shape
    l = projc.shape[1]
    nh = NA_HEADS
    q_off = (2 * RET_HEADS * RET_DK + 2 * RET_HEADS * RET_DV) // NA_DH
    k_off = q_off + nh
    v_off = k_off + nh
    ck_off = (RET_HEADS * RET_DK + RET_HEADS * RET_DV) // NA_DH
    cv_off = ck_off + nh
    nq = NA_QROWS * GRID_W
    nk = NA_WROWS * GRID_W
    return pl.pallas_call(
        _natt_kernel,
        out_shape=jax.ShapeDtypeStruct((b, s, nh * NA_DH), BF16),
        grid=(nh, b),
        in_specs=[
            pl.BlockSpec((1, s, NA_DH), lambda h, bi: (bi, 0, q_off + h)),
            pl.BlockSpec((1, s, NA_DH), lambda h, bi: (bi, 0, k_off + h)),
            pl.BlockSpec((1, s, NA_DH), lambda h, bi: (bi, 0, v_off + h)),
            pl.BlockSpec((1, l, NA_DH), lambda h, bi: (bi, 0, ck_off + h)),
            pl.BlockSpec((1, l, NA_DH), lambda h, bi: (bi, 0, cv_off + h)),
            pl.BlockSpec((1, 2 * NA_KR, GRID_W, 2 * GRID_W), lambda h, bi: (h, 0, 0, 0)),
        ],
        out_specs=pl.BlockSpec((1, s, NA_DH), lambda h, bi: (bi, 0, h)),
        scratch_shapes=[pltpu.VMEM((3, nq, nk), F32)],
        compiler_params=_params(("arbitrary", "arbitrary"), 48),
        name="natt",
    )(proj, proj, proj, projc, projc, pairs)


def _merge_kernel(a_ref, n_ref, ga_ref, gb_ref, wa_ref, wb_ref, o_ref):
    ra = jnp.dot(a_ref[...], wa_ref[...], preferred_element_type=F32)
    rn = jnp.dot(n_ref[...], wb_ref[...], preferred_element_type=F32)
    o_ref[...] = (_sigmoid(ga_ref[...].astype(F32)) * ra + _sigmoid(gb_ref[...].astype(F32)) * rn).astype(o_ref.dtype)


def _merge(ret_in, na_in, proj2d, w_pa, w_pb):
    t, d = ret_in.shape
    tm, tn = MERGE_TM, MERGE_TN
    ga_off = (proj2d.shape[1] - 2 * d) // tn
    gb_off = (proj2d.shape[1] - d) // tn
    return pl.pallas_call(
        _merge_kernel,
        out_shape=jax.ShapeDtypeStruct((t, d), BF16),
        grid=(t // tm, d // tn),
        in_specs=[
            pl.BlockSpec((tm, ret_in.shape[1]), lambda i, j: (i, 0)),
            pl.BlockSpec((tm, na_in.shape[1]), lambda i, j: (i, 0)),
            pl.BlockSpec((tm, tn), lambda i, j: (i, ga_off + j)),
            pl.BlockSpec((tm, tn), lambda i, j: (i, gb_off + j)),
            pl.BlockSpec((w_pa.shape[0], tn), lambda i, j: (0, j)),
            pl.BlockSpec((w_pb.shape[0], tn), lambda i, j: (0, j)),
        ],
        out_specs=pl.BlockSpec((tm, tn), lambda i, j: (i, j)),
        compiler_params=_params(("arbitrary", "arbitrary"), 48),
        name="merge",
    )(ret_in, na_in, proj2d, proj2d, w_pa, w_pb)


def _pack_bf16_pairs(lo, hi):
    lo_bits = lax.bitcast_convert_type(lo.astype(F32), jnp.uint32)
    hi_bits = lax.bitcast_convert_type(hi.astype(F32), jnp.uint32)
    return (lo_bits >> 16) | (hi_bits & jnp.uint32(0xFFFF0000))


def _unpack_bf16_pairs(words):
    lo = lax.bitcast_convert_type(words << 16, F32).astype(BF16)
    hi = lax.bitcast_convert_type(words & jnp.uint32(0xFFFF0000), F32).astype(BF16)
    return lo, hi


def _oproj_kernel(m_ref, x_ref, g1_ref, ng_ref, sh_ref, sc_ref, wo_ref, wrh_ref, wrl_ref, br_ref,
                  x1_ref, h2_ref, lg_ref):
    y = jnp.dot(m_ref[...], wo_ref[...], preferred_element_type=F32)
    x1 = x_ref[0] + g1_ref[0] * y
    x1_ref[0] = x1
    ms = jnp.mean(x1 * x1, axis=-1, keepdims=True)
    h2 = x1 * lax.rsqrt(ms + NORM_EPS) * ng_ref[...]
    h2 = h2 * (1.0 + sc_ref[0]) + sh_ref[0]
    h2_hi = h2.astype(BF16)
    half = h2.shape[1] // 2
    h2_ref[...] = _pack_bf16_pairs(h2_hi[:, :half], h2_hi[:, half:])
    h2_lo = (h2 - h2_hi.astype(F32)).astype(BF16)
    lg = jnp.dot(h2_hi, wrh_ref[...], preferred_element_type=F32)
    lg += jnp.dot(h2_lo, wrh_ref[...], preferred_element_type=F32)
    lg += jnp.dot(h2_hi, wrl_ref[...], preferred_element_type=F32)
    lg_ref[...] = lg + br_ref[...]


def _oproj(m, x, g1, norm_g, sh2, sc2, w_o, w_r, b_r):
    b, s, d = x.shape
    tm = OPROJ_TM
    spt = s // tm
    w_r_hi = w_r.astype(BF16)
    w_r_lo = (w_r - w_r_hi.astype(F32)).astype(BF16)
    return pl.pallas_call(
        _oproj_kernel,
        out_shape=(
            jax.ShapeDtypeStruct((b, s, d), F32),
            jax.ShapeDtypeStruct((b * s, d // 2), jnp.uint32),
            jax.ShapeDtypeStruct((b * s, LANES), F32),
        ),
        grid=(b, spt),
        in_specs=[
            pl.BlockSpec((tm, d), lambda bi, i: (bi * spt + i, 0)),
            pl.BlockSpec((1, tm, d), lambda bi, i: (bi, i, 0)),
            pl.BlockSpec((1, 1, d), lambda bi, i: (bi, 0, 0)),
            pl.BlockSpec((1, d), lambda bi, i: (0, 0)),
            pl.BlockSpec((1, 1, d), lambda bi, i: (bi, 0, 0)),
            pl.BlockSpec((1, 1, d), lambda bi, i: (bi, 0, 0)),
            pl.BlockSpec((d, d), lambda bi, i: (0, 0)),
            pl.BlockSpec((d, LANES), lambda bi, i: (0, 0)),
            pl.BlockSpec((d, LANES), lambda bi, i: (0, 0)),
            pl.BlockSpec((1, LANES), lambda bi, i: (0, 0)),
        ],
        out_specs=(
            pl.BlockSpec((1, tm, d), lambda bi, i: (bi, i, 0)),
            pl.BlockSpec((tm, d // 2), lambda bi, i: (bi * spt + i, 0)),
            pl.BlockSpec((tm, LANES), lambda bi, i: (bi * spt + i, 0)),
        ),
        compiler_params=_params(("arbitrary", "arbitrary"), 48),
        name="oproj",
    )(m, x, g1, norm_g, sh2, sc2, w_o, w_r_hi, w_r_lo, b_r)


def _route_kernel(lg_ref, e_ref, w_ref, r_ref, cnt_ref, run_scr):
    i = pl.program_id(0)
    tm = lg_ref.shape[0]

    @pl.when(i == 0)
    def _():
        run_scr[...] = jnp.zeros_like(run_scr)

    l = lg_ref[...]
    lane = lax.broadcasted_iota(jnp.int32, l.shape, 1)
    vals, idxs, hots = [], [], []
    for _ in range(TOP_K):
        m = jnp.max(l, axis=-1, keepdims=True)
        idx = jnp.min(jnp.where(l == m, lane, LANES), axis=-1, keepdims=True)
        hot = lane == idx
        l = jnp.where(hot, -jnp.inf, l)
        vals.append(m)
        idxs.append(idx)
        hots.append(hot)
    exps = [jnp.exp(v - vals[0]) for v in vals]
    tot = exps[0]
    for e in exps[1:]:
        tot = tot + e

    member = hots[0]
    for hot in hots[1:]:
        member = member | hot
    member = member.astype(F32)
    ri = lax.broadcasted_iota(jnp.int32, (tm, tm), 0)
    ci = lax.broadcasted_iota(jnp.int32, (tm, tm), 1)
    lower = (ci < ri).astype(BF16)
    before = jnp.dot(lower, member.astype(BF16), preferred_element_type=F32) + run_scr[...]

    e_out = jnp.zeros(l.shape, jnp.int32)
    w_out = jnp.zeros(l.shape, F32)
    r_out = jnp.zeros(l.shape, jnp.int32)
    for k in range(TOP_K):
        rank = jnp.sum(jnp.where(hots[k], before, 0.0), axis=-1, keepdims=True).astype(jnp.int32)
        e_out = jnp.where(lane == k, idxs[k], e_out)
        w_out = jnp.where(lane == k, exps[k] / tot, w_out)
        r_out = jnp.where(lane == k, rank, r_out)
    e_ref[...] = e_out[:, :TOP_K]
    w_ref[...] = w_out
    r_ref[...] = r_out[:, :TOP_K]
    run_scr[...] += jnp.sum(member, axis=0, keepdims=True)
    cnt_ref[...] = run_scr[...].astype(jnp.int32)


def _route(logits):
    t = logits.shape[0]
    tm = ROUTE_TM
    row = pl.BlockSpec((tm, LANES), lambda i: (i, 0))
    narrow = pl.BlockSpec((tm, TOP_K), lambda i: (i, 0))
    return pl.pallas_call(
        _route_kernel,
        out_shape=(
            jax.ShapeDtypeStruct((t, TOP_K), jnp.int32),
            jax.ShapeDtypeStruct((t, LANES), F32),
            jax.ShapeDtypeStruct((t, TOP_K), jnp.int32),
            jax.ShapeDtypeStruct((1, LANES), jnp.int32),
        ),
        grid=(t // tm,),
        in_specs=[row],
        out_specs=(narrow, row, narrow, pl.BlockSpec((1, LANES), lambda i: (0, 0))),
        scratch_shapes=[pltpu.VMEM((1, LANES), F32)],
        compiler_params=_params(("arbitrary",), 32),
        name="route",
    )(logits)


def _dispatch_kernel(dest_ref, h_ref, xb_ref, sem):
    tm = h_ref.shape[0]

    for t in range(tm):
        for k in range(TOP_K):
            pltpu.make_async_copy(h_ref.at[pl.ds(t, 1)], xb_ref.at[pl.ds(dest_ref[TOP_K * t + k], 1)], sem).start()
    for k in range(TOP_K):
        pltpu.make_async_copy(h_ref, xb_ref.at[pl.ds(0, tm)], sem).wait()


def _dispatch(h2, dest_flat):
    t, d = h2.shape
    tm = DISPATCH_TM
    return pl.pallas_call(
        _dispatch_kernel,
        out_shape=jax.ShapeDtypeStruct((t * TOP_K, d), h2.dtype),
        grid=(t // tm,),
        in_specs=[
            pl.BlockSpec((tm * TOP_K,), lambda i: (i,), memory_space=pltpu.SMEM),
            pl.BlockSpec((tm, d), lambda i: (i, 0)),
        ],
        out_specs=pl.BlockSpec(memory_space=pl.ANY),
        scratch_shapes=[pltpu.SemaphoreType.DMA],
        compiler_params=_params(("arbitrary",), 32),
        name="dispatch",
    )(dest_flat, h2)


def _expert_kernel(tile_ref, exp_ref, lo_ref, hi_ref, first_ref,
                   x_ref, wg_ref, bg_ref, wu_ref, bu_ref, wd_ref, bd_ref, o_ref, xs_scr):
    w = pl.program_id(0)
    j = pl.program_id(1)
    lo = lo_ref[w]
    hi = hi_ref[w]
    sub = EXPERT_SUB
    half = x_ref.shape[1]

    @pl.when(j == 0)
    def _():
        x_lo, x_hi = _unpack_bf16_pairs(x_ref[...])
        xs_scr[:, :half] = x_lo
        xs_scr[:, half:] = x_hi

    @pl.when((j == 0) & (first_ref[w] == 1))
    def _():
        o_ref[...] = jnp.zeros_like(o_ref)

    def ffn_tile(s):
        rows = slice(s * sub, (s + 1) * sub)
        xs = xs_scr[rows, :]
        gate = jnp.dot(xs, wg_ref[0].astype(BF16), preferred_element_type=F32) + bg_ref[0]
        up = jnp.dot(xs, wu_ref[0].astype(BF16), preferred_element_type=F32) + bu_ref[0]
        gate = jnp.minimum(gate, SWIGLU_LIMIT)
        up = jnp.clip(up, -SWIGLU_LIMIT, SWIGLU_LIMIT)
        act = gate * _sigmoid(SWIGLU_ALPHA * gate) * (up + 1.0)
        row = lax.broadcasted_iota(jnp.int32, (sub, 1), 0) + s * sub
        mine = (row >= lo) & (row < hi)
        act = jnp.where(mine, act, 0.0).astype(BF16)
        y = jnp.dot(act, wd_ref[0].astype(BF16), preferred_element_type=F32)
        o_ref[rows, :] += y + jnp.where(mine & (j == 0), bd_ref[0], 0.0)

    for p in range(x_ref.shape[0] // (2 * sub)):
        s0, s1 = 2 * p, 2 * p + 1
        act0 = (lo < (s0 + 1) * sub) & (hi > s0 * sub)
        act1 = (lo < (s1 + 1) * sub) & (hi > s1 * sub)

        @pl.when(act0 & act1)
        def _(s0=s0, s1=s1):
            ffn_tile(s0)
            ffn_tile(s1)

        @pl.when(act0 & jnp.logical_not(act1))
        def _(s0=s0):
            ffn_tile(s0)

        @pl.when(act1 & jnp.logical_not(act0))
        def _(s1=s1):
            ffn_tile(s1)


def _experts(xb, items, w_gate, b_gate, w_up, b_up, w_down, b_down):
    r, half = xb.shape
    e, d, f = w_gate.shape
    tm, tf = EXPERT_TM, EXPERT_TF
    nj = f // tf
    tile, expert, lo, hi, first = items
    n_items = tile.shape[0]

    def jj(j, w, hi_ref, lo_ref):
        return jnp.where(hi_ref[w] > lo_ref[w], j, nj - 1)

    return pl.pallas_call(
        _expert_kernel,
        out_shape=jax.ShapeDtypeStruct((r, d), F32),
        grid_spec=pltpu.PrefetchScalarGridSpec(
            num_scalar_prefetch=5,
            grid=(n_items, nj),
            in_specs=[
                pl.BlockSpec((tm, half), lambda w, j, ti, ex, lo_, hi_, fi: (ti[w], 0)),
                pl.BlockSpec((1, d, tf), lambda w, j, ti, ex, lo_, hi_, fi: (ex[w], 0, jj(j, w, hi_, lo_))),
                pl.BlockSpec((1, 1, tf), lambda w, j, ti, ex, lo_, hi_, fi: (ex[w], 0, jj(j, w, hi_, lo_))),
                pl.BlockSpec((1, d, tf), lambda w, j, ti, ex, lo_, hi_, fi: (ex[w], 0, jj(j, w, hi_, lo_))),
                pl.BlockSpec((1, 1, tf), lambda w, j, ti, ex, lo_, hi_, fi: (ex[w], 0, jj(j, w, hi_, lo_))),
                pl.BlockSpec((1, tf, d), lambda w, j, ti, ex, lo_, hi_, fi: (ex[w], jj(j, w, hi_, lo_), 0)),
                pl.BlockSpec((1, 1, d), lambda w, j, ti, ex, lo_, hi_, fi: (ex[w], 0, 0)),
            ],
            out_specs=pl.BlockSpec((tm, d), lambda w, j, ti, ex, lo_, hi_, fi: (ti[w], 0)),
            scratch_shapes=[pltpu.VMEM((tm, d), BF16)],
        ),
        compiler_params=_params(("arbitrary", "arbitrary"), 60),
        name="experts",
    )(tile, expert, lo, hi, first, xb, w_gate, b_gate.reshape(e, 1, f), w_up, b_up.reshape(e, 1, f),
      w_down, b_down.reshape(e, 1, d))


def _work_items(counts, n_rows):
    tm = EXPERT_TM
    n_tiles = n_rows // tm
    n_items = n_tiles + N_EXPERTS - 1
    cum = jnp.cumsum(counts)
    start = cum - counts
    tile_lo = jnp.arange(n_tiles, dtype=jnp.int32) * tm
    e_lo = jnp.searchsorted(cum, tile_lo, side="right").astype(jnp.int32)
    e_hi = jnp.searchsorted(cum, tile_lo + tm - 1, side="right").astype(jnp.int32)
    per_tile = e_hi - e_lo + 1
    off = jnp.cumsum(per_tile) - per_tile
    total = jnp.sum(per_tile)
    w = jnp.arange(n_items, dtype=jnp.int32)
    valid = w < total
    tile = jnp.clip(jnp.searchsorted(off, w, side="right").astype(jnp.int32) - 1, 0, n_tiles - 1)
    expert = jnp.where(valid, e_lo[tile] + w - off[tile], e_hi[n_tiles - 1])
    tile = jnp.where(valid, tile, n_tiles - 1)
    lo = jnp.clip(start[expert] - tile * tm, 0, tm)
    hi = jnp.clip(cum[expert] - tile * tm, 0, tm)
    hi = jnp.where(valid, jnp.maximum(hi, lo), lo)
    first = (valid & (w == off[tile])).astype(jnp.int32)
    return tile, expert.astype(jnp.int32), lo.astype(jnp.int32), hi.astype(jnp.int32), first, start


def _combine_kernel(dest_ref, dnext_ref, dnext2_ref, yb_ref, w_ref, x1_ref, g2_ref, fg_ref, o_ref, buf, sem):
    i = pl.program_id(0)
    n = pl.num_programs(0)
    tm = x1_ref.shape[0]
    slot = i % COMBINE_SLOTS

    def gather(d_ref, into):
        for t in range(tm):
            for k in range(TOP_K):
                pltpu.make_async_copy(yb_ref.at[pl.ds(d_ref[TOP_K * t + k], 1)], buf.at[into, k, pl.ds(t, 1)],
                                      sem.at[into]).start()

    def reduce():
        wts = w_ref[...]
        moe = buf[slot, 0] * wts[:, 0:1]
        for k in range(1, TOP_K):
            moe += buf[slot, k] * wts[:, k:k + 1]
        x2 = x1_ref[...] + g2_ref[0] * moe
        ms = jnp.mean(x2 * x2, axis=-1, keepdims=True)
        o_ref[...] = x2 * lax.rsqrt(ms + NORM_EPS) * fg_ref[...]

    @pl.when(i == 0)
    def _():
        gather(dest_ref, 0)
        gather(dnext_ref, 1)

    for k in range(TOP_K):
        pltpu.make_async_copy(yb_ref.at[pl.ds(0, tm)], buf.at[slot, k], sem.at[slot]).wait()

    @pl.when(i + 2 < n)
    def _():
        gather(dnext2_ref, (i + 2) % COMBINE_SLOTS)
        reduce()

    @pl.when(i + 2 >= n)
    def _():
        reduce()


def _combine(yb, dest_flat, top_w, x1, g2, final_g):
    b, s, d = x1.shape
    tm = COMBINE_TM
    spt = s // tm
    n = b * spt
    assert n >= COMBINE_SLOTS
    out = pl.pallas_call(
        _combine_kernel,
        out_shape=jax.ShapeDtypeStruct((b * s, d), F32),
        grid=(n,),
        in_specs=[
            pl.BlockSpec((tm * TOP_K,), lambda i: (i,), memory_space=pltpu.SMEM),
            pl.BlockSpec((tm * TOP_K,), lambda i: (jnp.minimum(i + 1, n - 1),), memory_space=pltpu.SMEM),
            pl.BlockSpec((tm * TOP_K,), lambda i: (jnp.minimum(i + 2, n - 1),), memory_space=pltpu.SMEM),
            pl.BlockSpec(memory_space=pl.ANY),
            pl.BlockSpec((tm, LANES), lambda i: (i, 0)),
            pl.BlockSpec((tm, d), lambda i: (i, 0)),
            pl.BlockSpec((1, 1, d), lambda i: (i // spt, 0, 0)),
            pl.BlockSpec((1, d), lambda i: (0, 0)),
        ],
        out_specs=pl.BlockSpec((tm, d), lambda i: (i, 0)),
        scratch_shapes=[pltpu.VMEM((COMBINE_SLOTS, TOP_K, tm, d), F32), pltpu.SemaphoreType.DMA((COMBINE_SLOTS,))],
        compiler_params=_params(("arbitrary",), 40),
        name="combine",
    )(dest_flat, dest_flat, dest_flat, yb, top_w, x1.reshape(b * s, d), g2, final_g)
    return out.reshape(b, s, d)


def kernel(x, c, ctx, c_ctx, ada_w, ada_b, norm1_g, norm2_g, w_in, w_pa, w_pb, w_o, ret_decay_fwd, ret_decay_bwd,
           na_rpb, w_router, b_router, w_gate, b_gate, w_up, b_up, w_down, b_down, final_g):
    assert ada_w.shape[0] == 1, "single layer"
    b, s, d = x.shape
    l = ctx.shape[1]
    in_w = w_in.shape[2]
    rows = s // GRID_W
    assert s % IN_TM == 0 and l % RET_CHUNK == 0 and rows >= NA_WROWS and rows % NA_QROWS == 0

    c_rows = jnp.zeros((16, d), F32).at[:b].set(c).at[b].set(c_ctx)
    mod = _ada(c_rows, ada_w[0], ada_b[0][None, :])
    sh1, sc1, g1, sh2, sc2, g2 = [mod[:b, None, i * d:(i + 1) * d] for i in range(6)]
    shc1 = mod[b, 0 * d:1 * d][None, None, :]
    scc1 = mod[b, 1 * d:2 * d][None, None, :]

    w_in_bf = w_in[0].astype(BF16)
    n1 = norm1_g[0][None, :]
    proj = _inproj(x, n1, sh1, sc1, w_in_bf, tuple(range(in_w // IN_TN)), IN_TM)
    qk_w = RET_HEADS * RET_DK
    v_w = RET_HEADS * RET_DV
    na_w = NA_HEADS * NA_DH
    ctx_cols = tuple(range(qk_w // IN_TN, (2 * qk_w + v_w) // IN_TN)) + tuple(
        range((2 * qk_w + 2 * v_w + na_w) // IN_TN, (2 * qk_w + 2 * v_w + 3 * na_w) // IN_TN))
    projc = _inproj(ctx.reshape(1, b * l, d), n1, shc1, scc1, w_in_bf, ctx_cols, b * l).reshape(b, l, -1)

    lgf = jax.nn.log_sigmoid(ret_decay_fwd[0].astype(F32))
    lgb = jax.nn.log_sigmoid(ret_decay_bwd[0].astype(F32))
    cos_t, sin_t = _rope_tables(s)
    ret_in = _retention(proj, projc, lgf, lgb, cos_t, sin_t)
    na_in = _natt(proj, projc, _natt_pair_tiles(na_rpb[0]))

    t = b * s
    m = _merge(ret_in.reshape(t, -1), na_in.reshape(t, -1), proj.reshape(t, in_w),
               w_pa[0].astype(BF16), w_pb[0].astype(BF16))

    w_r = jnp.zeros((d, LANES), F32).at[:, :N_EXPERTS].set(w_router[0])
    b_r = jnp.full((1, LANES), NEG_INF, F32).at[0, :N_EXPERTS].set(b_router[0])
    x1, h2, logits = _oproj(m, x, g1, norm2_g[0][None, :], sh2, sc2, w_o[0].astype(BF16), w_r, b_r)

    top_e, top_w, rank, counts = _route(logits)
    counts = counts[0, :N_EXPERTS]
    items = _work_items(counts, t * TOP_K)
    start = items[5]
    is_e = top_e[:, :, None] == jnp.arange(N_EXPERTS, dtype=jnp.int32)
    dest = (jnp.sum(jnp.where(is_e, start.astype(jnp.int32), 0), axis=-1) + rank).reshape(-1)

    xb = _dispatch(h2, dest)
    yb = _experts(xb, items[:5], w_gate[0], b_gate[0], w_up[0], b_up[0], w_down[0], b_down[0])
    return _combine(yb, dest, top_w, x1, g2, final_g[None, :])
```

```python
import functools

import jax
import jax.numpy as jnp
import numpy as np
from jax import lax
from jax.experimental import pallas as pl
from jax.experimental.pallas import tpu as pltpu

F32 = jnp.float32
BF16 = jnp.bfloat16

GRID_W = 64
RET_HEADS = 8
RET_DK = 128
RET_DV = 256
RET_CHUNK = 128
ROPE_AXIS_DIM = RET_DK // 2
ROPE_BASE = 10000.0
NA_HEADS = 16
NA_DH = 128
NA_KR = 8
NA_KC = 16
N_EXPERTS = 32
TOP_K = 4
SWIGLU_ALPHA = 1.702
SWIGLU_LIMIT = 7.0
NORM_EPS = 1e-6
GN_EPS = 1e-5
NEG_INF = -1e30
LOG2_E = 1.4426950408889634

LANES = 128
MIB = 1024 * 1024

NA_QROWS = 4
NA_WROWS = NA_QROWS + NA_KR

IN_TM, IN_TN = 1024, 512
MERGE_TM, MERGE_TN = 1024, 512
OPROJ_TM = 256
ROUTE_TM = 512
DISPATCH_TM = 256
EXPERT_TM, EXPERT_SUB, EXPERT_TF = 1024, 256, 512
COMBINE_TM = 128
COMBINE_SLOTS = 3


def _params(semantics, vmem_mib):
    return pltpu.CompilerParams(dimension_semantics=semantics, vmem_limit_bytes=vmem_mib * MIB)


def _sigmoid(x):
    return 1.0 / (1.0 + jnp.exp(-x))


def _ada_kernel(c_ref, w_ref, b_ref, o_ref):
    c = c_ref[...]
    s = c * _sigmoid(c)
    o_ref[...] = jnp.dot(s.astype(BF16), w_ref[...].astype(BF16), preferred_element_type=F32) + b_ref[...]


def _ada(c_rows, w, b):
    r, d = c_rows.shape
    n = w.shape[1]
    tn = 1024
    return pl.pallas_call(
        _ada_kernel,
        out_shape=jax.ShapeDtypeStruct((r, n), F32),
        grid=(n // tn,),
        in_specs=[
            pl.BlockSpec((r, d), lambda j: (0, 0)),
            pl.BlockSpec((d, tn), lambda j: (0, j)),
            pl.BlockSpec((1, tn), lambda j: (0, j)),
        ],
        out_specs=pl.BlockSpec((r, tn), lambda j: (0, j)),
        compiler_params=_params(("arbitrary",), 40),
        name="ada",
    )(c_rows, w, b)


def _inproj_kernel(cols_ref, x_ref, g_ref, sh_ref, sc_ref, w_ref, o_ref, h_scr):
    @pl.when(pl.program_id(2) == 0)
    def _():
        x = x_ref[0]
        ms = jnp.mean(x * x, axis=-1, keepdims=True)
        y = x * lax.rsqrt(ms + NORM_EPS) * g_ref[...]
        h_scr[...] = (y * (1.0 + sc_ref[0]) + sh_ref[0]).astype(BF16)

    o_ref[0] = jnp.dot(h_scr[...], w_ref[...].astype(BF16), preferred_element_type=F32).astype(o_ref.dtype)


def _inproj(x, g, shift, scale, w, col_tiles, tm):
    b, n, d = x.shape
    tn = IN_TN
    nct = len(col_tiles)
    cols = jnp.asarray(np.asarray(col_tiles, np.int32))
    return pl.pallas_call(
        _inproj_kernel,
        out_shape=jax.ShapeDtypeStruct((b, n, nct * tn), BF16),
        grid_spec=pltpu.PrefetchScalarGridSpec(
            num_scalar_prefetch=1,
            grid=(b, n // tm, nct),
            in_specs=[
                pl.BlockSpec((1, tm, d), lambda bi, i, j, c: (bi, i, 0)),
                pl.BlockSpec((1, d), lambda bi, i, j, c: (0, 0)),
                pl.BlockSpec((1, 1, d), lambda bi, i, j, c: (bi, 0, 0)),
                pl.BlockSpec((1, 1, d), lambda bi, i, j, c: (bi, 0, 0)),
                pl.BlockSpec((d, tn), lambda bi, i, j, c: (0, c[j])),
            ],
            out_specs=pl.BlockSpec((1, tm, tn), lambda bi, i, j, c: (bi, i, j)),
            scratch_shapes=[pltpu.VMEM((tm, d), BF16)],
        ),
        compiler_params=_params(("arbitrary", "arbitrary", "arbitrary"), 48),
        name="inproj",
    )(cols, x, g, shift, scale, w)


def _swap_halves(x):
    lane = lax.broadcasted_iota(jnp.int32, x.shape, 1)
    return jnp.where(lane % 64 < 32, pltpu.roll(x, 96, 1), pltpu.roll(x, 32, 1))


def _ret_kernel(lgf_ref, lgb_ref, q_ref, k_ref, v_ref, g_ref, kc_ref, vc_ref, cos_ref, sin_ref,
                o_ref, ks_scr, rf_scr, rb_scr, ub_scr):
    h = pl.program_id(1)
    c = RET_CHUNK
    n = q_ref.shape[1]
    nc = n // c
    ncc = kc_ref.shape[1] // c
    lgf = lgf_ref[h]
    lgb = lgb_ref[h]
    k_scale = RET_DK ** -0.5

    pos_c = lax.broadcasted_iota(jnp.int32, (c, 1), 0).astype(F32)
    zeta_f = jnp.exp((c - 1.0 - pos_c) * lgf)
    zeta_b = jnp.exp(pos_c * lgb)
    xi_f = jnp.exp((pos_c + 1.0) * lgf)
    xi_b = jnp.exp((c - pos_c) * lgb)
    one = jnp.ones((1, 1), F32)
    gc_f = jnp.exp(one * (c * lgf))
    gc_b = jnp.exp(one * (c * lgb))
    ii = lax.broadcasted_iota(jnp.int32, (c, c), 0)
    jj = lax.broadcasted_iota(jnp.int32, (c, c), 1)
    diff = (ii - jj).astype(F32)
    dmask = (jnp.where(diff >= 0, jnp.exp(jnp.maximum(diff, 0.0) * lgf), 0.0)
             + jnp.where(diff <= 0, jnp.exp(jnp.maximum(-diff, 0.0) * lgb), 0.0))

    def ktv(k_bf, v_f32, zeta):
        return jnp.dot(k_bf.astype(F32).T.astype(BF16), (v_f32 * zeta).astype(BF16), preferred_element_type=F32)

    r_f = jnp.zeros((RET_DK, RET_DV), F32)
    for i in range(ncc):
        kc = (kc_ref[0, i * c:(i + 1) * c, :].astype(F32) * k_scale).astype(BF16)
        r_f = gc_f * r_f + ktv(kc, vc_ref[0, i * c:(i + 1) * c, :].astype(F32), zeta_f)
    r_b = jnp.zeros((RET_DK, RET_DV), F32)
    for i in reversed(range(ncc)):
        kc = (kc_ref[0, i * c:(i + 1) * c, :].astype(F32) * k_scale).astype(BF16)
        r_b = gc_b * r_b + ktv(kc, vc_ref[0, i * c:(i + 1) * c, :].astype(F32), zeta_b)

    def rope(x, rows):
        x = x.astype(F32)
        return x * cos_ref[rows, :] + _swap_halves(x) * sin_ref[rows, :]

    def chunk_updates(i, r):
        rows = pl.ds(pl.multiple_of(i * c, c), c)
        kb = (rope(k_ref[0, rows, :], rows) * k_scale).astype(BF16)
        ks_scr[rows, :] = kb
        kt = kb.astype(F32).T.astype(BF16)
        v = v_ref[0, rows, :].astype(F32)
        rf_scr[i] = r.astype(BF16)
        ub_scr[i] = jnp.dot(kt, (v * zeta_b).astype(BF16), preferred_element_type=F32)
        return gc_f * r + jnp.dot(kt, (v * zeta_f).astype(BF16), preferred_element_type=F32)

    lax.fori_loop(0, nc, chunk_updates, r_f, unroll=8)

    def bwd_scan(t, r):
        i = nc - 1 - t
        rb_scr[i] = r.astype(BF16)
        return gc_b * r + ub_scr[i]

    lax.fori_loop(0, nc, bwd_scan, r_b, unroll=2)

    def out_chunk(i, carry):
        rows = pl.ds(pl.multiple_of(i * c, c), c)
        q = rope(q_ref[0, rows, :], rows)
        kb = ks_scr[rows, :]
        vb = v_ref[0, rows, :]
        s = lax.dot_general(q.astype(BF16), kb, (((1,), (1,)), ((), ())), preferred_element_type=F32) * dmask
        o = jnp.dot(s.astype(BF16), vb, preferred_element_type=F32)
        o += jnp.dot((q * xi_f).astype(BF16), rf_scr[i], preferred_element_type=F32)
        o += jnp.dot((q * xi_b).astype(BF16), rb_scr[i], preferred_element_type=F32)
        mu = jnp.mean(o, axis=-1, keepdims=True)
        d = o - mu
        var = jnp.mean(d * d, axis=-1, keepdims=True)
        on = d * lax.rsqrt(var + GN_EPS)
        g = g_ref[0, rows, :].astype(F32)
        o_ref[0, rows, :] = (g * _sigmoid(g) * on).astype(o_ref.dtype)
        return carry

    lax.fori_loop(0, nc, out_chunk, 0, unroll=8)


def _retention(proj, projc, lgf, lgb, cos_t, sin_t):
    b, s, _ = proj.shape
    l = projc.shape[1]
    hq = RET_HEADS
    v_off = 2 * hq * RET_DK // RET_DV
    g_off = v_off + hq
    cv_off = hq * RET_DK // RET_DV
    smem = pl.BlockSpec(memory_space=pltpu.SMEM)
    return pl.pallas_call(
        _ret_kernel,
        out_shape=jax.ShapeDtypeStruct((b, s, hq * RET_DV), BF16),
        grid=(b, hq),
        in_specs=[
            smem, smem,
            pl.BlockSpec((1, s, RET_DK), lambda bi, h: (bi, 0, h)),
            pl.BlockSpec((1, s, RET_DK), lambda bi, h: (bi, 0, hq + h)),
            pl.BlockSpec((1, s, RET_DV), lambda bi, h: (bi, 0, v_off + h)),
            pl.BlockSpec((1, s, RET_DV), lambda bi, h: (bi, 0, g_off + h)),
            pl.BlockSpec((1, l, RET_DK), lambda bi, h: (bi, 0, h)),
            pl.BlockSpec((1, l, RET_DV), lambda bi, h: (bi, 0, cv_off + h)),
            pl.BlockSpec((s, RET_DK), lambda bi, h: (0, 0)),
            pl.BlockSpec((s, RET_DK), lambda bi, h: (0, 0)),
        ],
        out_specs=pl.BlockSpec((1, s, RET_DV), lambda bi, h: (bi, 0, h)),
        scratch_shapes=[
            pltpu.VMEM((s, RET_DK), BF16),
            pltpu.VMEM((s // RET_CHUNK, RET_DK, RET_DV), BF16),
            pltpu.VMEM((s // RET_CHUNK, RET_DK, RET_DV), BF16),
            pltpu.VMEM((s // RET_CHUNK, RET_DK, RET_DV), F32),
        ],
        compiler_params=_params(("arbitrary", "arbitrary"), 48),
        name="ret",
    )(lgf, lgb, proj, proj, proj, proj, projc, projc, cos_t, sin_t)


def _rope_tables(n):
    rows = n // GRID_W
    inv = ROPE_BASE ** (-jnp.arange(0, ROPE_AXIS_DIM, 2, dtype=F32) / ROPE_AXIS_DIM)
    ar = jnp.arange(rows, dtype=F32)[:, None] * inv
    ac = jnp.arange(GRID_W, dtype=F32)[:, None] * inv
    f = inv.shape[0]

    def per_token(row_part, col_part):
        row_part = jnp.broadcast_to(row_part[:, None, :], (rows, GRID_W, 2 * f))
        col_part = jnp.broadcast_to(col_part[None, :, :], (rows, GRID_W, 2 * f))
        return jnp.concatenate([row_part, col_part], axis=2).reshape(n, 4 * f)

    cos_t = per_token(jnp.concatenate([jnp.cos(ar), jnp.cos(ar)], axis=1),
                      jnp.concatenate([jnp.cos(ac), jnp.cos(ac)], axis=1))
    sin_t = per_token(jnp.concatenate([-jnp.sin(ar), jnp.sin(ar)], axis=1),
                      jnp.concatenate([-jnp.sin(ac), jnp.sin(ac)], axis=1))
    return cos_t, sin_t


def _natt_kernel(q_ref, k_ref, v_ref, kc_ref, vc_ref, pair_ref, o_ref, bias_scr):
    @pl.when(pl.program_id(1) == 0)
    def _():
        _natt_fill_bias(pair_ref, bias_scr)

    n = q_ref.shape[1]
    rows = n // GRID_W
    n_blk = rows // NA_QROWS
    nq = NA_QROWS * GRID_W
    nk = NA_WROWS * GRID_W
    scale = NA_DH ** -0.5
    kc = kc_ref[0]
    vc = vc_ref[0]
    nt = (((1,), (1,)), ((), ()))

    def block(blk, carry):
        ws = jnp.clip(blk * NA_QROWS - NA_KR // 2, 0, rows - NA_WROWS)
        variant = jnp.where(blk == 0, 0, jnp.where(blk == n_blk - 1, 2, 1))
        qrows = pl.ds(pl.multiple_of(blk * nq, nq), nq)
        krows = pl.ds(pl.multiple_of(ws * GRID_W, GRID_W), nk)
        q = q_ref[0, qrows, :]
        s_loc = lax.dot_general(q, k_ref[0, krows, :], nt, preferred_element_type=F32) + bias_scr[variant]
        s_ctx = lax.dot_general(q, kc, nt, preferred_element_type=F32)
        m = jnp.maximum(jnp.max(s_loc, axis=-1, keepdims=True), jnp.max(s_ctx, axis=-1, keepdims=True))
        p_loc = jnp.exp2((s_loc - m) * (scale * LOG2_E))
        p_ctx = jnp.exp2((s_ctx - m) * (scale * LOG2_E))
        denom = jnp.sum(p_loc, axis=-1, keepdims=True) + jnp.sum(p_ctx, axis=-1, keepdims=True)
        o = jnp.dot(p_loc.astype(BF16), v_ref[0, krows, :], preferred_element_type=F32)
        o += jnp.dot(p_ctx.astype(BF16), vc, preferred_element_type=F32)
        o_ref[0, qrows, :] = (o / denom).astype(o_ref.dtype)
        return carry

    lax.fori_loop(0, n_blk, block, 0, unroll=4)


def _natt_row_offsets():
    rq = np.arange(NA_QROWS)[:, None]
    wr = np.arange(NA_WROWS)[None, :]
    dr0 = np.where(wr < NA_KR, wr - rq + NA_KR - 1, -1)
    dr1 = np.where((wr >= rq) & (wr < rq + NA_KR), wr - rq + NA_KR // 2 - 1, -1)
    dr2 = np.where(wr >= NA_QROWS, wr - rq - 1, -1)
    return np.stack([dr0, dr1, dr2])


def _natt_fill_bias(pair_ref, bias_scr):
    w = GRID_W
    dr = _natt_row_offsets()
    neg = jnp.full((w, 2 * w), NEG_INF, F32)
    lane_row = lax.broadcasted_iota(jnp.int32, (w, NA_WROWS * w), 1) // w
    for kind in range(dr.shape[0]):
        for rq in range(NA_QROWS):
            tiles = []
            for p in range(NA_WROWS // 2):
                a, b_ = int(dr[kind, rq, 2 * p]), int(dr[kind, rq, 2 * p + 1])
                if a < 0 and b_ < 0:
                    tiles.append(neg)
                else:
                    tiles.append(pair_ref[0, b_ if b_ >= 0 else a + 1])
            strip = jnp.concatenate(tiles, axis=1)
            valid = np.nonzero(dr[kind, rq] >= 0)[0]
            keep = (lane_row >= int(valid[0])) & (lane_row <= int(valid[-1]))
            bias_scr[kind, rq * w:(rq + 1) * w, :] = jnp.where(keep, strip, NEG_INF)


def _natt_pair_tiles(rpb):
    w = GRID_W
    qc = np.arange(w)[:, None]
    kcol = np.arange(w)[None, :]
    cs = np.clip(qc - NA_KC // 2, 0, w - NA_KC)
    col_ok = (kcol >= cs) & (kcol < cs + NA_KC)
    dc_idx = np.clip(kcol - qc, -(NA_KC - 1), NA_KC - 1) + NA_KC - 1
    pick = jnp.asarray(dc_idx[None] == np.arange(2 * NA_KC - 1)[:, None, None])
    t = jnp.sum(jnp.where(pick[None, None], rpb.astype(F32)[:, :, :, None, None], 0.0), axis=2)
    t = jnp.where(col_ok[None, None], t * NA_DH ** 0.5, NEG_INF)
    neg = jnp.full((rpb.shape[0], 1, w, w), NEG_INF, F32)
    return jnp.concatenate([jnp.concatenate([neg, t], axis=1), jnp.concatenate([t, neg], axis=1)], axis=3)


def _natt(proj, projc, pairs):
    b, s, _ = proj.shape
    l = projc.shape[1]
    nh = NA_HEADS
    q_off = (2 * RET_HEADS * RET_DK + 2 * RET_HEADS * RET_DV) // NA_DH
    k_off = q_off + nh
    v_off = k_off + nh
    ck_off = (RET_HEADS * RET_DK + RET_HEADS * RET_DV) // NA_DH
    cv_off = ck_off + nh
    nq = NA_QROWS * GRID_W
    nk = NA_WROWS * GRID_W
    return pl.pallas_call(
        _natt_kernel,
        out_shape=jax.ShapeDtypeStruct((b, s, nh * NA_DH), BF16),
        grid=(nh, b),
        in_specs=[
            pl.BlockSpec((1, s, NA_DH), lambda h, bi: (bi, 0, q_off + h)),
            pl.BlockSpec((1, s, NA_DH), lambda h, bi: (bi, 0, k_off + h)),
            pl.BlockSpec((1, s, NA_DH), lambda h, bi: (bi, 0, v_off + h)),
            pl.BlockSpec((1, l, NA_DH), lambda h, bi: (bi, 0, ck_off + h)),
            pl.BlockSpec((1, l, NA_DH), lambda h, bi: (bi, 0, cv_off + h)),
            pl.BlockSpec((1, 2 * NA_KR, GRID_W, 2 * GRID_W), lambda h, bi: (h, 0, 0, 0)),
        ],
        out_specs=pl.BlockSpec((1, s, NA_DH), lambda h, bi: (bi, 0, h)),
        scratch_shapes=[pltpu.VMEM((3, nq, nk), F32)],
        compiler_params=_params(("arbitrary", "arbitrary"), 48),
        name="natt",
    )(proj, proj, proj, projc, projc, pairs)


def _merge_kernel(a_ref, n_ref, ga_ref, gb_ref, wa_ref, wb_ref, o_ref):
    ra = jnp.dot(a_ref[...], wa_ref[...].astype(BF16), preferred_element_type=F32)
    rn = jnp.dot(n_ref[...], wb_ref[...].astype(BF16), preferred_element_type=F32)
    o_ref[...] = (_sigmoid(ga_ref[...].astype(F32)) * ra + _sigmoid(gb_ref[...].astype(F32)) * rn).astype(o_ref.dtype)


def _merge(ret_in, na_in, proj2d, w_pa, w_pb):
    t, d = ret_in.shape
    tm, tn = MERGE_TM, MERGE_TN
    ga_off = (proj2d.shape[1] - 2 * d) // tn
    gb_off = (proj2d.shape[1] - d) // tn
    return pl.pallas_call(
        _merge_kernel,
        out_shape=jax.ShapeDtypeStruct((t, d), BF16),
        grid=(t // tm, d // tn),
        in_specs=[
            pl.BlockSpec((tm, ret_in.shape[1]), lambda i, j: (i, 0)),
            pl.BlockSpec((tm, na_in.shape[1]), lambda i, j: (i, 0)),
            pl.BlockSpec((tm, tn), lambda i, j: (i, ga_off + j)),
            pl.BlockSpec((tm, tn), lambda i, j: (i, gb_off + j)),
            pl.BlockSpec((w_pa.shape[0], tn), lambda i, j: (0, j)),
            pl.BlockSpec((w_pb.shape[0], tn), lambda i, j: (0, j)),
        ],
        out_specs=pl.BlockSpec((tm, tn), lambda i, j: (i, j)),
        compiler_params=_params(("arbitrary", "arbitrary"), 48),
        name="merge",
    )(ret_in, na_in, proj2d, proj2d, w_pa, w_pb)


def _pack_bf16_pairs(lo, hi):
    lo_bits = lax.bitcast_convert_type(lo.astype(F32), jnp.uint32)
    hi_bits = lax.bitcast_convert_type(hi.astype(F32), jnp.uint32)
    return (lo_bits >> 16) | (hi_bits & jnp.uint32(0xFFFF0000))


def _unpack_bf16_pairs(words):
    lo = lax.bitcast_convert_type(words << 16, F32).astype(BF16)
    hi = lax.bitcast_convert_type(words & jnp.uint32(0xFFFF0000), F32).astype(BF16)
    return lo, hi


def _oproj_kernel(m_ref, x_ref, g1_ref, ng_ref, sh_ref, sc_ref, wo_ref, wrh_ref, wrl_ref, br_ref,
                  x1_ref, h2_ref, lg_ref):
    y = jnp.dot(m_ref[...], wo_ref[...], preferred_element_type=F32)
    x1 = x_ref[0] + g1_ref[0] * y
    x1_ref[0] = x1
    ms = jnp.mean(x1 * x1, axis=-1, keepdims=True)
    h2 = x1 * lax.rsqrt(ms + NORM_EPS) * ng_ref[...]
    h2 = h2 * (1.0 + sc_ref[0]) + sh_ref[0]
    h2_hi = h2.astype(BF16)
    half = h2.shape[1] // 2
    h2_ref[...] = _pack_bf16_pairs(h2_hi[:, :half], h2_hi[:, half:])
    h2_lo = (h2 - h2_hi.astype(F32)).astype(BF16)
    lg = jnp.dot(h2_hi, wrh_ref[...], preferred_element_type=F32)
    lg += jnp.dot(h2_lo, wrh_ref[...], preferred_element_type=F32)
    lg += jnp.dot(h2_hi, wrl_ref[...], preferred_element_type=F32)
    lg_ref[...] = lg + br_ref[...]


def _oproj(m, x, g1, norm_g, sh2, sc2, w_o, w_r, b_r):
    b, s, d = x.shape
    tm = OPROJ_TM
    spt = s // tm
    w_r_hi = w_r.astype(BF16)
    w_r_lo = (w_r - w_r_hi.astype(F32)).astype(BF16)
    return pl.pallas_call(
        _oproj_kernel,
        out_shape=(
            jax.ShapeDtypeStruct((b, s, d), F32),
            jax.ShapeDtypeStruct((b * s, d // 2), jnp.uint32),
            jax.ShapeDtypeStruct((b * s, LANES), F32),
        ),
        grid=(b, spt),
        in_specs=[
            pl.BlockSpec((tm, d), lambda bi, i: (bi * spt + i, 0)),
            pl.BlockSpec((1, tm, d), lambda bi, i: (bi, i, 0)),
            pl.BlockSpec((1, 1, d), lambda bi, i: (bi, 0, 0)),
            pl.BlockSpec((1, d), lambda bi, i: (0, 0)),
            pl.BlockSpec((1, 1, d), lambda bi, i: (bi, 0, 0)),
            pl.BlockSpec((1, 1, d), lambda bi, i: (bi, 0, 0)),
            pl.BlockSpec((d, d), lambda bi, i: (0, 0)),
            pl.BlockSpec((d, LANES), lambda bi, i: (0, 0)),
            pl.BlockSpec((d, LANES), lambda bi, i: (0, 0)),
            pl.BlockSpec((1, LANES), lambda bi, i: (0, 0)),
        ],
        out_specs=(
            pl.BlockSpec((1, tm, d), lambda bi, i: (bi, i, 0)),
            pl.BlockSpec((tm, d // 2), lambda bi, i: (bi * spt + i, 0)),
            pl.BlockSpec((tm, LANES), lambda bi, i: (bi * spt + i, 0)),
        ),
        compiler_params=_params(("arbitrary", "arbitrary"), 48),
        name="oproj",
    )(m, x, g1, norm_g, sh2, sc2, w_o, w_r_hi, w_r_lo, b_r)


def _route_kernel(lg_ref, e_ref, w_ref, r_ref, cnt_ref, run_scr):
    i = pl.program_id(0)
    tm = lg_ref.shape[0]

    @pl.when(i == 0)
    def _():
        run_scr[...] = jnp.zeros_like(run_scr)

    l = lg_ref[...]
    lane = lax.broadcasted_iota(jnp.int32, l.shape, 1)
    vals, idxs, hots = [], [], []
    for _ in range(TOP_K):
        m = jnp.max(l, axis=-1, keepdims=True)
        idx = jnp.min(jnp.where(l == m, lane, LANES), axis=-1, keepdims=True)
        hot = lane == idx
        l = jnp.where(hot, -jnp.inf, l)
        vals.append(m)
        idxs.append(idx)
        hots.append(hot)
    exps = [jnp.exp(v - vals[0]) for v in vals]
    tot = exps[0]
    for e in exps[1:]:
        tot = tot + e

    member = hots[0]
    for hot in hots[1:]:
        member = member | hot
    member = member.astype(F32)
    ri = lax.broadcasted_iota(jnp.int32, (tm, tm), 0)
    ci = lax.broadcasted_iota(jnp.int32, (tm, tm), 1)
    lower = (ci < ri).astype(BF16)
    before = jnp.dot(lower, member.astype(BF16), preferred_element_type=F32) + run_scr[...]

    e_out = jnp.zeros(l.shape, jnp.int32)
    w_out = jnp.zeros(l.shape, F32)
    r_out = jnp.zeros(l.shape, jnp.int32)
    for k in range(TOP_K):
        rank = jnp.sum(jnp.where(hots[k], before, 0.0), axis=-1, keepdims=True).astype(jnp.int32)
        e_out = jnp.where(lane == k, idxs[k], e_out)
        w_out = jnp.where(lane == k, exps[k] / tot, w_out)
        r_out = jnp.where(lane == k, rank, r_out)
    e_ref[...] = e_out[:, :TOP_K]
    w_ref[...] = w_out
    r_ref[...] = r_out[:, :TOP_K]
    run_scr[...] += jnp.sum(member, axis=0, keepdims=True)
    cnt_ref[...] = run_scr[...].astype(jnp.int32)


def _route(logits):
    t = logits.shape[0]
    tm = ROUTE_TM
    row = pl.BlockSpec((tm, LANES), lambda i: (i, 0))
    narrow = pl.BlockSpec((tm, TOP_K), lambda i: (i, 0))
    return pl.pallas_call(
        _route_kernel,
        out_shape=(
            jax.ShapeDtypeStruct((t, TOP_K), jnp.int32),
            jax.ShapeDtypeStruct((t, LANES), F32),
            jax.ShapeDtypeStruct((t, TOP_K), jnp.int32),
            jax.ShapeDtypeStruct((1, LANES), jnp.int32),
        ),
        grid=(t // tm,),
        in_specs=[row],
        out_specs=(narrow, row, narrow, pl.BlockSpec((1, LANES), lambda i: (0, 0))),
        scratch_shapes=[pltpu.VMEM((1, LANES), F32)],
        compiler_params=_params(("arbitrary",), 32),
        name="route",
    )(logits)


def _dispatch_kernel(dest_ref, h_ref, xb_ref, sem):
    tm = h_ref.shape[0]

    for t in range(tm):
        for k in range(TOP_K):
            pltpu.make_async_copy(h_ref.at[pl.ds(t, 1)], xb_ref.at[pl.ds(dest_ref[TOP_K * t + k], 1)], sem).start()
    for k in range(TOP_K):
        pltpu.make_async_copy(h_ref, xb_ref.at[pl.ds(0, tm)], sem).wait()


def _dispatch(h2, dest_flat):
    t, d = h2.shape
    tm = DISPATCH_TM
    return pl.pallas_call(
        _dispatch_kernel,
        out_shape=jax.ShapeDtypeStruct((t * TOP_K, d), h2.dtype),
        grid=(t // tm,),
        in_specs=[
            pl.BlockSpec((tm * TOP_K,), lambda i: (i,), memory_space=pltpu.SMEM),
            pl.BlockSpec((tm, d), lambda i: (i, 0)),
        ],
        out_specs=pl.BlockSpec(memory_space=pl.ANY),
        scratch_shapes=[pltpu.SemaphoreType.DMA],
        compiler_params=_params(("arbitrary",), 32),
        name="dispatch",
    )(dest_flat, h2)


def _expert_kernel(tile_ref, exp_ref, lo_ref, hi_ref, first_ref,
                   x_ref, wg_ref, bg_ref, wu_ref, bu_ref, wd_ref, bd_ref, o_ref, xs_scr):
    w = pl.program_id(0)
    j = pl.program_id(1)
    lo = lo_ref[w]
    hi = hi_ref[w]
    sub = EXPERT_SUB
    half = x_ref.shape[1]

    @pl.when(j == 0)
    def _():
        x_lo, x_hi = _unpack_bf16_pairs(x_ref[...])
        xs_scr[:, :half] = x_lo
        xs_scr[:, half:] = x_hi

    @pl.when((j == 0) & (first_ref[w] == 1))
    def _():
        o_ref[...] = jnp.zeros_like(o_ref)

    def ffn_tile(s):
        rows = slice(s * sub, (s + 1) * sub)
        xs = xs_scr[rows, :]
        gate = jnp.dot(xs, wg_ref[0].astype(BF16), preferred_element_type=F32) + bg_ref[0]
        up = jnp.dot(xs, wu_ref[0].astype(BF16), preferred_element_type=F32) + bu_ref[0]
        gate = jnp.minimum(gate, SWIGLU_LIMIT)
        up = jnp.clip(up, -SWIGLU_LIMIT, SWIGLU_LIMIT)
        act = gate * _sigmoid(SWIGLU_ALPHA * gate) * (up + 1.0)
        row = lax.broadcasted_iota(jnp.int32, (sub, 1), 0) + s * sub
        mine = (row >= lo) & (row < hi)
        act = jnp.where(mine, act, 0.0).astype(BF16)
        y = jnp.dot(act, wd_ref[0].astype(BF16), preferred_element_type=F32)
        o_ref[rows, :] += y + jnp.where(mine & (j == 0), bd_ref[0], 0.0)

    n_sub = x_ref.shape[0] // sub
    active = [(lo < (s + 1) * sub) & (hi > s * sub) for s in range(n_sub)]
    whole = functools.reduce(jnp.logical_and, active)

    @pl.when(whole)
    def _():
        for s in range(n_sub):
            ffn_tile(s)

    partial = jnp.logical_not(whole)
    for p in range(n_sub // 2):
        s0, s1 = 2 * p, 2 * p + 1

        @pl.when(partial & active[s0] & active[s1])
        def _(s0=s0, s1=s1):
            ffn_tile(s0)
            ffn_tile(s1)

        @pl.when(partial & active[s0] & jnp.logical_not(active[s1]))
        def _(s0=s0):
            ffn_tile(s0)

        @pl.when(partial & active[s1] & jnp.logical_not(active[s0]))
        def _(s1=s1):
            ffn_tile(s1)


def _experts(xb, items, w_gate, b_gate, w_up, b_up, w_down, b_down):
    r, half = xb.shape
    e, d, f = w_gate.shape
    tm, tf = EXPERT_TM, EXPERT_TF
    nj = f // tf
    tile, expert, lo, hi, first = items
    n_items = tile.shape[0]

    def jj(j, w, hi_ref, lo_ref):
        return jnp.where(hi_ref[w] > lo_ref[w], j, nj - 1)

    return pl.pallas_call(
        _expert_kernel,
        out_shape=jax.ShapeDtypeStruct((r, d), F32),
        grid_spec=pltpu.PrefetchScalarGridSpec(
            num_scalar_prefetch=5,
            grid=(n_items, nj),
            in_specs=[
                pl.BlockSpec((tm, half), lambda w, j, ti, ex, lo_, hi_, fi: (ti[w], 0)),
                pl.BlockSpec((1, d, tf), lambda w, j, ti, ex, lo_, hi_, fi: (ex[w], 0, jj(j, w, hi_, lo_))),
                pl.BlockSpec((1, 1, tf), lambda w, j, ti, ex, lo_, hi_, fi: (ex[w], 0, jj(j, w, hi_, lo_))),
                pl.BlockSpec((1, d, tf), lambda w, j, ti, ex, lo_, hi_, fi: (ex[w], 0, jj(j, w, hi_, lo_))),
                pl.BlockSpec((1, 1, tf), lambda w, j, ti, ex, lo_, hi_, fi: (ex[w], 0, jj(j, w, hi_, lo_))),
                pl.BlockSpec((1, tf, d), lambda w, j, ti, ex, lo_, hi_, fi: (ex[w], jj(j, w, hi_, lo_), 0)),
                pl.BlockSpec((1, 1, d), lambda w, j, ti, ex, lo_, hi_, fi: (ex[w], 0, 0)),
            ],
            out_specs=pl.BlockSpec((tm, d), lambda w, j, ti, ex, lo_, hi_, fi: (ti[w], 0)),
            scratch_shapes=[pltpu.VMEM((tm, d), BF16)],
        ),
        compiler_params=_params(("arbitrary", "arbitrary"), 60),
        name="experts",
    )(tile, expert, lo, hi, first, xb, w_gate, b_gate.reshape(e, 1, f), w_up, b_up.reshape(e, 1, f),
      w_down, b_down.reshape(e, 1, d))


def _work_items(counts, n_rows):
    tm = EXPERT_TM
    n_tiles = n_rows // tm
    n_items = n_tiles + N_EXPERTS - 1
    cum = jnp.cumsum(counts)
    start = cum - counts
    tile_lo = jnp.arange(n_tiles, dtype=jnp.int32) * tm
    e_lo = jnp.searchsorted(cum, tile_lo, side="right").astype(jnp.int32)
    e_hi = jnp.searchsorted(cum, tile_lo + tm - 1, side="right").astype(jnp.int32)
    per_tile = e_hi - e_lo + 1
    off = jnp.cumsum(per_tile) - per_tile
    total = jnp.sum(per_tile)
    w = jnp.arange(n_items, dtype=jnp.int32)
    valid = w < total
    tile = jnp.clip(jnp.searchsorted(off, w, side="right").astype(jnp.int32) - 1, 0, n_tiles - 1)
    expert = jnp.where(valid, e_lo[tile] + w - off[tile], e_hi[n_tiles - 1])
    tile = jnp.where(valid, tile, n_tiles - 1)
    lo = jnp.clip(start[expert] - tile * tm, 0, tm)
    hi = jnp.clip(cum[expert] - tile * tm, 0, tm)
    hi = jnp.where(valid, jnp.maximum(hi, lo), lo)
    first = (valid & (w == off[tile])).astype(jnp.int32)
    return tile, expert.astype(jnp.int32), lo.astype(jnp.int32), hi.astype(jnp.int32), first, start


def _combine_kernel(dest_ref, dnext_ref, dnext2_ref, yb_ref, w_ref, x1_ref, g2_ref, fg_ref, o_ref, buf, sem):
    i = pl.program_id(0)
    n = pl.num_programs(0)
    tm = x1_ref.shape[0]
    slot = i % COMBINE_SLOTS

    def gather(d_ref, into):
        for t in range(tm):
            for k in range(TOP_K):
                pltpu.make_async_copy(yb_ref.at[pl.ds(d_ref[TOP_K * t + k], 1)], buf.at[into, k, pl.ds(t, 1)],
                                      sem.at[into]).start()

    def reduce():
        wts = w_ref[...]
        moe = buf[slot, 0] * wts[:, 0:1]
        for k in range(1, TOP_K):
            moe += buf[slot, k] * wts[:, k:k + 1]
        x2 = x1_ref[...] + g2_ref[0] * moe
        ms = jnp.mean(x2 * x2, axis=-1, keepdims=True)
        o_ref[...] = x2 * lax.rsqrt(ms + NORM_EPS) * fg_ref[...]

    @pl.when(i == 0)
    def _():
        gather(dest_ref, 0)
        gather(dnext_ref, 1)

    for k in range(TOP_K):
        pltpu.make_async_copy(yb_ref.at[pl.ds(0, tm)], buf.at[slot, k], sem.at[slot]).wait()

    @pl.when(i + 2 < n)
    def _():
        gather(dnext2_ref, (i + 2) % COMBINE_SLOTS)
        reduce()

    @pl.when(i + 2 >= n)
    def _():
        reduce()


def _combine(yb, dest_flat, top_w, x1, g2, final_g):
    b, s, d = x1.shape
    tm = COMBINE_TM
    spt = s // tm
    n = b * spt
    assert n >= COMBINE_SLOTS
    out = pl.pallas_call(
        _combine_kernel,
        out_shape=jax.ShapeDtypeStruct((b * s, d), F32),
        grid=(n,),
        in_specs=[
            pl.BlockSpec((tm * TOP_K,), lambda i: (i,), memory_space=pltpu.SMEM),
            pl.BlockSpec((tm * TOP_K,), lambda i: (jnp.minimum(i + 1, n - 1),), memory_space=pltpu.SMEM),
            pl.BlockSpec((tm * TOP_K,), lambda i: (jnp.minimum(i + 2, n - 1),), memory_space=pltpu.SMEM),
            pl.BlockSpec(memory_space=pl.ANY),
            pl.BlockSpec((tm, LANES), lambda i: (i, 0)),
            pl.BlockSpec((tm, d), lambda i: (i, 0)),
            pl.BlockSpec((1, 1, d), lambda i: (i // spt, 0, 0)),
            pl.BlockSpec((1, d), lambda i: (0, 0)),
        ],
        out_specs=pl.BlockSpec((tm, d), lambda i: (i, 0)),
        scratch_shapes=[pltpu.VMEM((COMBINE_SLOTS, TOP_K, tm, d), F32), pltpu.SemaphoreType.DMA((COMBINE_SLOTS,))],
        compiler_params=_params(("arbitrary",), 40),
        name="combine",
    )(dest_flat, dest_flat, dest_flat, yb, top_w, x1.reshape(b * s, d), g2, final_g)
    return out.reshape(b, s, d)


def kernel(x, c, ctx, c_ctx, ada_w, ada_b, norm1_g, norm2_g, w_in, w_pa, w_pb, w_o, ret_decay_fwd, ret_decay_bwd,
           na_rpb, w_router, b_router, w_gate, b_gate, w_up, b_up, w_down, b_down, final_g):
    assert ada_w.shape[0] == 1, "single layer"
    b, s, d = x.shape
    l = ctx.shape[1]
    in_w = w_in.shape[2]
    rows = s // GRID_W
    assert s % IN_TM == 0 and l % RET_CHUNK == 0 and rows >= NA_WROWS and rows % NA_QROWS == 0

    c_rows = jnp.zeros((16, d), F32).at[:b].set(c).at[b].set(c_ctx)
    mod = _ada(c_rows, ada_w[0], ada_b[0][None, :])
    sh1, sc1, g1, sh2, sc2, g2 = [mod[:b, None, i * d:(i + 1) * d] for i in range(6)]
    shc1 = mod[b, 0 * d:1 * d][None, None, :]
    scc1 = mod[b, 1 * d:2 * d][None, None, :]

    w_in0 = w_in[0]
    n1 = norm1_g[0][None, :]
    proj = _inproj(x, n1, sh1, sc1, w_in0, tuple(range(in_w // IN_TN)), IN_TM)
    qk_w = RET_HEADS * RET_DK
    v_w = RET_HEADS * RET_DV
    na_w = NA_HEADS * NA_DH
    ctx_cols = tuple(range(qk_w // IN_TN, (2 * qk_w + v_w) // IN_TN)) + tuple(
        range((2 * qk_w + 2 * v_w + na_w) // IN_TN, (2 * qk_w + 2 * v_w + 3 * na_w) // IN_TN))
    projc = _inproj(ctx.reshape(1, b * l, d), n1, shc1, scc1, w_in0, ctx_cols, b * l).reshape(b, l, -1)

    lgf = jax.nn.log_sigmoid(ret_decay_fwd[0].astype(F32))
    lgb = jax.nn.log_sigmoid(ret_decay_bwd[0].astype(F32))
    cos_t, sin_t = _rope_tables(s)
    ret_in = _retention(proj, projc, lgf, lgb, cos_t, sin_t)
    na_in = _natt(proj, projc, _natt_pair_tiles(na_rpb[0]))

    t = b * s
    m = _merge(ret_in.reshape(t, -1), na_in.reshape(t, -1), proj.reshape(t, in_w),
               w_pa[0], w_pb[0])

    w_r = jnp.zeros((d, LANES), F32).at[:, :N_EXPERTS].set(w_router[0])
    b_r = jnp.full((1, LANES), NEG_INF, F32).at[0, :N_EXPERTS].set(b_router[0])
    x1, h2, logits = _oproj(m, x, g1, norm2_g[0][None, :], sh2, sc2, w_o[0].astype(BF16), w_r, b_r)

    top_e, top_w, rank, counts = _route(logits)
    counts = counts[0, :N_EXPERTS]
    items = _work_items(counts, t * TOP_K)
    start = items[5]
    is_e = top_e[:, :, None] == jnp.arange(N_EXPERTS, dtype=jnp.int32)
    dest = (jnp.sum(jnp.where(is_e, start.astype(jnp.int32), 0), axis=-1) + rank).reshape(-1)

    xb = _dispatch(h2, dest)
    yb = _experts(xb, items[:5], w_gate[0], b_gate[0], w_up[0], b_up[0], w_down[0], b_down[0])
    return _combine(yb, dest, top_w, x1, g2, final_g[None, :])
```

```python
import functools

import jax
import jax.numpy as jnp
import numpy as np
from jax import lax
from jax.experimental import pallas as pl
from jax.experimental.pallas import tpu as pltpu

F32 = jnp.float32
BF16 = jnp.bfloat16

GRID_W = 64
RET_HEADS = 8
RET_DK = 128
RET_DV = 256
RET_CHUNK = 128
ROPE_AXIS_DIM = RET_DK // 2
ROPE_BASE = 10000.0
NA_HEADS = 16
NA_DH = 128
NA_KR = 8
NA_KC = 16
N_EXPERTS = 32
TOP_K = 4
SWIGLU_ALPHA = 1.702
SWIGLU_LIMIT = 7.0
NORM_EPS = 1e-6
GN_EPS = 1e-5
NEG_INF = -1e30
LOG2_E = 1.4426950408889634

LANES = 128
MIB = 1024 * 1024

NA_QROWS = 4
NA_WROWS = NA_QROWS + NA_KR

IN_TM, IN_TN = 1024, 512
MERGE_TM, MERGE_TN = 1024, 512
OPROJ_TM = 256
ROUTE_TM = 512
DISPATCH_TM = 256
EXPERT_TM, EXPERT_SUB, EXPERT_TF = 1024, 256, 512
COMBINE_TM = 128
COMBINE_SLOTS = 3


def _params(semantics, vmem_mib):
    return pltpu.CompilerParams(dimension_semantics=semantics, vmem_limit_bytes=vmem_mib * MIB)


def _sigmoid(x):
    return 1.0 / (1.0 + jnp.exp(-x))


def _ada_kernel(c_ref, w_ref, b_ref, o_ref):
    c = c_ref[...]
    s = c * _sigmoid(c)
    o_ref[...] = jnp.dot(s.astype(BF16), w_ref[...].astype(BF16), preferred_element_type=F32) + b_ref[...]


def _ada(c_rows, w, b):
    r, d = c_rows.shape
    n = w.shape[1]
    tn = 1024
    return pl.pallas_call(
        _ada_kernel,
        out_shape=jax.ShapeDtypeStruct((r, n), F32),
        grid=(n // tn,),
        in_specs=[
            pl.BlockSpec((r, d), lambda j: (0, 0)),
            pl.BlockSpec((d, tn), lambda j: (0, j)),
            pl.BlockSpec((1, tn), lambda j: (0, j)),
        ],
        out_specs=pl.BlockSpec((r, tn), lambda j: (0, j)),
        compiler_params=_params(("arbitrary",), 40),
        name="ada",
    )(c_rows, w, b)


def _inproj_kernel(cols_ref, x_ref, g_ref, sh_ref, sc_ref, w_ref, o_ref, h_scr):
    @pl.when(pl.program_id(2) == 0)
    def _():
        x = x_ref[0]
        ms = jnp.mean(x * x, axis=-1, keepdims=True)
        y = x * lax.rsqrt(ms + NORM_EPS) * g_ref[...]
        h_scr[...] = (y * (1.0 + sc_ref[0]) + sh_ref[0]).astype(BF16)

    o_ref[0] = jnp.dot(h_scr[...], w_ref[...].astype(BF16), preferred_element_type=F32).astype(o_ref.dtype)


def _inproj(x, g, shift, scale, w, col_tiles, tm):
    b, n, d = x.shape
    tn = IN_TN
    nct = len(col_tiles)
    cols = jnp.asarray(np.asarray(col_tiles, np.int32))
    return pl.pallas_call(
        _inproj_kernel,
        out_shape=jax.ShapeDtypeStruct((b, n, nct * tn), BF16),
        grid_spec=pltpu.PrefetchScalarGridSpec(
            num_scalar_prefetch=1,
            grid=(b, n // tm, nct),
            in_specs=[
                pl.BlockSpec((1, tm, d), lambda bi, i, j, c: (bi, i, 0)),
                pl.BlockSpec((1, d), lambda bi, i, j, c: (0, 0)),
                pl.BlockSpec((1, 1, d), lambda bi, i, j, c: (bi, 0, 0)),
                pl.BlockSpec((1, 1, d), lambda bi, i, j, c: (bi, 0, 0)),
                pl.BlockSpec((d, tn), lambda bi, i, j, c: (0, c[j])),
            ],
            out_specs=pl.BlockSpec((1, tm, tn), lambda bi, i, j, c: (bi, i, j)),
            scratch_shapes=[pltpu.VMEM((tm, d), BF16)],
        ),
        compiler_params=_params(("arbitrary", "arbitrary", "arbitrary"), 48),
        name="inproj",
    )(cols, x, g, shift, scale, w)


def _swap_halves(x):
    lane = lax.broadcasted_iota(jnp.int32, x.shape, 1)
    return jnp.where(lane % 64 < 32, pltpu.roll(x, 96, 1), pltpu.roll(x, 32, 1))


def _ret_kernel(lgf_ref, lgb_ref, q_ref, k_ref, v_ref, g_ref, kc_ref, vc_ref, cos_ref, sin_ref,
                o_ref, ks_scr, rf_scr, rb_scr, ub_scr):
    h = pl.program_id(1)
    c = RET_CHUNK
    n = q_ref.shape[1]
    nc = n // c
    ncc = kc_ref.shape[1] // c
    lgf = lgf_ref[h]
    lgb = lgb_ref[h]
    k_scale = RET_DK ** -0.5

    pos_c = lax.broadcasted_iota(jnp.int32, (c, 1), 0).astype(F32)
    zeta_f = jnp.exp((c - 1.0 - pos_c) * lgf)
    zeta_b = jnp.exp(pos_c * lgb)
    xi_f = jnp.exp((pos_c + 1.0) * lgf)
    xi_b = jnp.exp((c - pos_c) * lgb)
    one = jnp.ones((1, 1), F32)
    gc_f = jnp.exp(one * (c * lgf))
    gc_b = jnp.exp(one * (c * lgb))
    ii = lax.broadcasted_iota(jnp.int32, (c, c), 0)
    jj = lax.broadcasted_iota(jnp.int32, (c, c), 1)
    diff = (ii - jj).astype(F32)
    dmask = (jnp.where(diff >= 0, jnp.exp(jnp.maximum(diff, 0.0) * lgf), 0.0)
             + jnp.where(diff <= 0, jnp.exp(jnp.maximum(-diff, 0.0) * lgb), 0.0))

    def ktv(k_bf, v_f32, zeta):
        return jnp.dot(k_bf.astype(F32).T.astype(BF16), (v_f32 * zeta).astype(BF16), preferred_element_type=F32)

    r_f = jnp.zeros((RET_DK, RET_DV), F32)
    for i in range(ncc):
        kc = (kc_ref[0, i * c:(i + 1) * c, :].astype(F32) * k_scale).astype(BF16)
        r_f = gc_f * r_f + ktv(kc, vc_ref[0, i * c:(i + 1) * c, :].astype(F32), zeta_f)
    r_b = jnp.zeros((RET_DK, RET_DV), F32)
    for i in reversed(range(ncc)):
        kc = (kc_ref[0, i * c:(i + 1) * c, :].astype(F32) * k_scale).astype(BF16)
        r_b = gc_b * r_b + ktv(kc, vc_ref[0, i * c:(i + 1) * c, :].astype(F32), zeta_b)

    def rope(x, rows):
        x = x.astype(F32)
        return x * cos_ref[rows, :] + _swap_halves(x) * sin_ref[rows, :]

    def chunk_updates(i, r):
        rows = pl.ds(pl.multiple_of(i * c, c), c)
        kb = (rope(k_ref[0, rows, :], rows) * k_scale).astype(BF16)
        ks_scr[rows, :] = kb
        kt = kb.astype(F32).T.astype(BF16)
        v = v_ref[0, rows, :].astype(F32)
        rf_scr[i] = r.astype(BF16)
        ub_scr[i] = jnp.dot(kt, (v * zeta_b).astype(BF16), preferred_element_type=F32)
        return gc_f * r + jnp.dot(kt, (v * zeta_f).astype(BF16), preferred_element_type=F32)

    lax.fori_loop(0, nc, chunk_updates, r_f, unroll=8)

    def bwd_scan(t, r):
        i = nc - 1 - t
        rb_scr[i] = r.astype(BF16)
        return gc_b * r + ub_scr[i]

    lax.fori_loop(0, nc, bwd_scan, r_b, unroll=2)

    def out_chunk(i, carry):
        rows = pl.ds(pl.multiple_of(i * c, c), c)
        q = rope(q_ref[0, rows, :], rows)
        kb = ks_scr[rows, :]
        vb = v_ref[0, rows, :]
        s = lax.dot_general(q.astype(BF16), kb, (((1,), (1,)), ((), ())), preferred_element_type=F32) * dmask
        o = jnp.dot(s.astype(BF16), vb, preferred_element_type=F32)
        o += jnp.dot((q * xi_f).astype(BF16), rf_scr[i], preferred_element_type=F32)
        o += jnp.dot((q * xi_b).astype(BF16), rb_scr[i], preferred_element_type=F32)
        mu = jnp.mean(o, axis=-1, keepdims=True)
        d = o - mu
        var = jnp.mean(d * d, axis=-1, keepdims=True)
        on = d * lax.rsqrt(var + GN_EPS)
        g = g_ref[0, rows, :].astype(F32)
        o_ref[0, rows, :] = (g * _sigmoid(g) * on).astype(o_ref.dtype)
        return carry

    lax.fori_loop(0, nc, out_chunk, 0, unroll=8)


def _retention(proj, projc, lgf, lgb, cos_t, sin_t):
    b, s, _ = proj.shape
    l = projc.shape[1]
    hq = RET_HEADS
    v_off = 2 * hq * RET_DK // RET_DV
    g_off = v_off + hq
    cv_off = hq * RET_DK // RET_DV
    smem = pl.BlockSpec(memory_space=pltpu.SMEM)
    return pl.pallas_call(
        _ret_kernel,
        out_shape=jax.ShapeDtypeStruct((b, s, hq * RET_DV), BF16),
        grid=(b, hq),
        in_specs=[
            smem, smem,
            pl.BlockSpec((1, s, RET_DK), lambda bi, h: (bi, 0, h)),
            pl.BlockSpec((1, s, RET_DK), lambda bi, h: (bi, 0, hq + h)),
            pl.BlockSpec((1, s, RET_DV), lambda bi, h: (bi, 0, v_off + h)),
            pl.BlockSpec((1, s, RET_DV), lambda bi, h: (bi, 0, g_off + h)),
            pl.BlockSpec((1, l, RET_DK), lambda bi, h: (bi, 0, h)),
            pl.BlockSpec((1, l, RET_DV), lambda bi, h: (bi, 0, cv_off + h)),
            pl.BlockSpec((s, RET_DK), lambda bi, h: (0, 0)),
            pl.BlockSpec((s, RET_DK), lambda bi, h: (0, 0)),
        ],
        out_specs=pl.BlockSpec((1, s, RET_DV), lambda bi, h: (bi, 0, h)),
        scratch_shapes=[
            pltpu.VMEM((s, RET_DK), BF16),
            pltpu.VMEM((s // RET_CHUNK, RET_DK, RET_DV), BF16),
            pltpu.VMEM((s // RET_CHUNK, RET_DK, RET_DV), BF16),
            pltpu.VMEM((s // RET_CHUNK, RET_DK, RET_DV), F32),
        ],
        compiler_params=_params(("arbitrary", "arbitrary"), 48),
        name="ret",
    )(lgf, lgb, proj, proj, proj, proj, projc, projc, cos_t, sin_t)


def _rope_tables(n):
    rows = n // GRID_W
    inv = ROPE_BASE ** (-jnp.arange(0, ROPE_AXIS_DIM, 2, dtype=F32) / ROPE_AXIS_DIM)
    ar = jnp.arange(rows, dtype=F32)[:, None] * inv
    ac = jnp.arange(GRID_W, dtype=F32)[:, None] * inv
    f = inv.shape[0]

    def per_token(row_part, col_part):
        row_part = jnp.broadcast_to(row_part[:, None, :], (rows, GRID_W, 2 * f))
        col_part = jnp.broadcast_to(col_part[None, :, :], (rows, GRID_W, 2 * f))
        return jnp.concatenate([row_part, col_part], axis=2).reshape(n, 4 * f)

    cos_t = per_token(jnp.concatenate([jnp.cos(ar), jnp.cos(ar)], axis=1),
                      jnp.concatenate([jnp.cos(ac), jnp.cos(ac)], axis=1))
    sin_t = per_token(jnp.concatenate([-jnp.sin(ar), jnp.sin(ar)], axis=1),
                      jnp.concatenate([-jnp.sin(ac), jnp.sin(ac)], axis=1))
    return cos_t, sin_t


def _natt_kernel(q_ref, k_ref, v_ref, kc_ref, vc_ref, pair_ref, o_ref, bias_scr, v1_scr, vc1_scr):
    @pl.when(pl.program_id(1) == 0)
    def _():
        _natt_fill_bias(pair_ref, bias_scr)

    n = q_ref.shape[1]
    rows = n // GRID_W
    n_blk = rows // NA_QROWS
    nq = NA_QROWS * GRID_W
    nk = NA_WROWS * GRID_W
    scale = NA_DH ** -0.5
    kc = kc_ref[0]
    nt = (((1,), (1,)), ((), ()))
    v1_scr[:, :NA_DH] = v_ref[0]
    v1_scr[:, NA_DH:] = jnp.ones((n, NA_DH), BF16)
    vc1_scr[:, :NA_DH] = vc_ref[0]
    vc1_scr[:, NA_DH:] = jnp.ones((vc_ref.shape[1], NA_DH), BF16)

    def block(blk, carry):
        ws = jnp.clip(blk * NA_QROWS - NA_KR // 2, 0, rows - NA_WROWS)
        variant = jnp.where(blk == 0, 0, jnp.where(blk == n_blk - 1, 2, 1))
        qrows = pl.ds(pl.multiple_of(blk * nq, nq), nq)
        krows = pl.ds(pl.multiple_of(ws * GRID_W, GRID_W), nk)
        q = q_ref[0, qrows, :]
        s_loc = lax.dot_general(q, k_ref[0, krows, :], nt, preferred_element_type=F32) + bias_scr[variant]
        s_ctx = lax.dot_general(q, kc, nt, preferred_element_type=F32)
        m = jnp.maximum(jnp.max(s_loc, axis=-1, keepdims=True), jnp.max(s_ctx, axis=-1, keepdims=True))
        p_loc = jnp.exp2((s_loc - m) * (scale * LOG2_E))
        p_ctx = jnp.exp2((s_ctx - m) * (scale * LOG2_E))
        o = jnp.dot(p_loc.astype(BF16), v1_scr[krows, :], preferred_element_type=F32)
        o += jnp.dot(p_ctx.astype(BF16), vc1_scr[...], preferred_element_type=F32)
        o_ref[0, qrows, :] = (o[:, :NA_DH] / o[:, NA_DH:]).astype(o_ref.dtype)
        return carry

    lax.fori_loop(0, n_blk, block, 0, unroll=4)


def _natt_row_offsets():
    rq = np.arange(NA_QROWS)[:, None]
    wr = np.arange(NA_WROWS)[None, :]
    dr0 = np.where(wr < NA_KR, wr - rq + NA_KR - 1, -1)
    dr1 = np.where((wr >= rq) & (wr < rq + NA_KR), wr - rq + NA_KR // 2 - 1, -1)
    dr2 = np.where(wr >= NA_QROWS, wr - rq - 1, -1)
    return np.stack([dr0, dr1, dr2])


def _natt_fill_bias(pair_ref, bias_scr):
    w = GRID_W
    dr = _natt_row_offsets()
    neg = jnp.full((w, 2 * w), NEG_INF, F32)
    lane_row = lax.broadcasted_iota(jnp.int32, (w, NA_WROWS * w), 1) // w
    for kind in range(dr.shape[0]):
        for rq in range(NA_QROWS):
            tiles = []
            for p in range(NA_WROWS // 2):
                a, b_ = int(dr[kind, rq, 2 * p]), int(dr[kind, rq, 2 * p + 1])
                if a < 0 and b_ < 0:
                    tiles.append(neg)
                else:
                    tiles.append(pair_ref[0, b_ if b_ >= 0 else a + 1])
            strip = jnp.concatenate(tiles, axis=1)
            valid = np.nonzero(dr[kind, rq] >= 0)[0]
            keep = (lane_row >= int(valid[0])) & (lane_row <= int(valid[-1]))
            bias_scr[kind, rq * w:(rq + 1) * w, :] = jnp.where(keep, strip, NEG_INF)


def _natt_pair_tiles(rpb):
    w = GRID_W
    qc = np.arange(w)[:, None]
    kcol = np.arange(w)[None, :]
    cs = np.clip(qc - NA_KC // 2, 0, w - NA_KC)
    col_ok = (kcol >= cs) & (kcol < cs + NA_KC)
    dc_idx = np.clip(kcol - qc, -(NA_KC - 1), NA_KC - 1) + NA_KC - 1
    pick = (dc_idx.reshape(1, -1) == np.arange(2 * NA_KC - 1)[:, None]).astype(np.float32)
    nh, ndr, ndc = rpb.shape
    t = jnp.dot(rpb.astype(F32).reshape(nh * ndr, ndc), jnp.asarray(pick), precision=lax.Precision.HIGHEST)
    t = jnp.where(col_ok[None, None], t.reshape(nh, ndr, w, w) * NA_DH ** 0.5, NEG_INF)
    neg = jnp.full((rpb.shape[0], 1, w, w), NEG_INF, F32)
    return jnp.concatenate([jnp.concatenate([neg, t], axis=1), jnp.concatenate([t, neg], axis=1)], axis=3)


def _natt(proj, projc, pairs):
    b, s, _ = proj.shape
    l = projc.shape[1]
    nh = NA_HEADS
    q_off = (2 * RET_HEADS * RET_DK + 2 * RET_HEADS * RET_DV) // NA_DH
    k_off = q_off + nh
    v_off = k_off + nh
    ck_off = (RET_HEADS * RET_DK + RET_HEADS * RET_DV) // NA_DH
    cv_off = ck_off + nh
    nq = NA_QROWS * GRID_W
    nk = NA_WROWS * GRID_W
    return pl.pallas_call(
        _natt_kernel,
        out_shape=jax.ShapeDtypeStruct((b, s, nh * NA_DH), BF16),
        grid=(nh, b),
        in_specs=[
            pl.BlockSpec((1, s, NA_DH), lambda h, bi: (bi, 0, q_off + h)),
            pl.BlockSpec((1, s, NA_DH), lambda h, bi: (bi, 0, k_off + h)),
            pl.BlockSpec((1, s, NA_DH), lambda h, bi: (bi, 0, v_off + h)),
            pl.BlockSpec((1, l, NA_DH), lambda h, bi: (bi, 0, ck_off + h)),
            pl.BlockSpec((1, l, NA_DH), lambda h, bi: (bi, 0, cv_off + h)),
            pl.BlockSpec((1, 2 * NA_KR, GRID_W, 2 * GRID_W), lambda h, bi: (h, 0, 0, 0)),
        ],
        out_specs=pl.BlockSpec((1, s, NA_DH), lambda h, bi: (bi, 0, h)),
        scratch_shapes=[pltpu.VMEM((3, nq, nk), F32), pltpu.VMEM((s, 2 * NA_DH), BF16),
                        pltpu.VMEM((l, 2 * NA_DH), BF16)],
        compiler_params=_params(("arbitrary", "arbitrary"), 48),
        name="natt",
    )(proj, proj, proj, projc, projc, pairs)


def _merge_kernel(a_ref, n_ref, ga_ref, gb_ref, wa_ref, wb_ref, o_ref):
    ra = jnp.dot(a_ref[...], wa_ref[...].astype(BF16), preferred_element_type=F32)
    rn = jnp.dot(n_ref[...], wb_ref[...].astype(BF16), preferred_element_type=F32)
    o_ref[...] = (_sigmoid(ga_ref[...].astype(F32)) * ra + _sigmoid(gb_ref[...].astype(F32)) * rn).astype(o_ref.dtype)


def _merge(ret_in, na_in, proj2d, w_pa, w_pb):
    t, d = ret_in.shape
    tm, tn = MERGE_TM, MERGE_TN
    ga_off = (proj2d.shape[1] - 2 * d) // tn
    gb_off = (proj2d.shape[1] - d) // tn
    return pl.pallas_call(
        _merge_kernel,
        out_shape=jax.ShapeDtypeStruct((t, d), BF16),
        grid=(t // tm, d // tn),
        in_specs=[
            pl.BlockSpec((tm, ret_in.shape[1]), lambda i, j: (i, 0)),
            pl.BlockSpec((tm, na_in.shape[1]), lambda i, j: (i, 0)),
            pl.BlockSpec((tm, tn), lambda i, j: (i, ga_off + j)),
            pl.BlockSpec((tm, tn), lambda i, j: (i, gb_off + j)),
            pl.BlockSpec((w_pa.shape[0], tn), lambda i, j: (0, j)),
            pl.BlockSpec((w_pb.shape[0], tn), lambda i, j: (0, j)),
        ],
        out_specs=pl.BlockSpec((tm, tn), lambda i, j: (i, j)),
        compiler_params=_params(("arbitrary", "arbitrary"), 48),
        name="merge",
    )(ret_in, na_in, proj2d, proj2d, w_pa, w_pb)


def _pack_bf16_pairs(lo, hi):
    lo_bits = lax.bitcast_convert_type(lo.astype(F32), jnp.uint32)
    hi_bits = lax.bitcast_convert_type(hi.astype(F32), jnp.uint32)
    return (lo_bits >> 16) | (hi_bits & jnp.uint32(0xFFFF0000))


def _unpack_bf16_pairs(words):
    lo = lax.bitcast_convert_type(words << 16, F32).astype(BF16)
    hi = lax.bitcast_convert_type(words & jnp.uint32(0xFFFF0000), F32).astype(BF16)
    return lo, hi


def _oproj_kernel(m_ref, x_ref, g1_ref, ng_ref, sh_ref, sc_ref, wo_ref, wrh_ref, wrl_ref, br_ref,
                  x1_ref, h2_ref, lg_ref):
    y = jnp.dot(m_ref[...], wo_ref[...], preferred_element_type=F32)
    x1 = x_ref[0] + g1_ref[0] * y
    x1_ref[0] = x1
    ms = jnp.mean(x1 * x1, axis=-1, keepdims=True)
    h2 = x1 * lax.rsqrt(ms + NORM_EPS) * ng_ref[...]
    h2 = h2 * (1.0 + sc_ref[0]) + sh_ref[0]
    h2_hi = h2.astype(BF16)
    half = h2.shape[1] // 2
    h2_ref[...] = _pack_bf16_pairs(h2_hi[:, :half], h2_hi[:, half:])
    h2_lo = (h2 - h2_hi.astype(F32)).astype(BF16)
    lg = jnp.dot(h2_hi, wrh_ref[...], preferred_element_type=F32)
    lg += jnp.dot(h2_lo, wrh_ref[...], preferred_element_type=F32)
    lg += jnp.dot(h2_hi, wrl_ref[...], preferred_element_type=F32)
    lg_ref[...] = lg + br_ref[...]


def _oproj(m, x, g1, norm_g, sh2, sc2, w_o, w_r, b_r):
    b, s, d = x.shape
    tm = OPROJ_TM
    spt = s // tm
    w_r_hi = w_r.astype(BF16)
    w_r_lo = (w_r - w_r_hi.astype(F32)).astype(BF16)
    return pl.pallas_call(
        _oproj_kernel,
        out_shape=(
            jax.ShapeDtypeStruct((b, s, d), F32),
            jax.ShapeDtypeStruct((b * s, d // 2), jnp.uint32),
            jax.ShapeDtypeStruct((b * s, LANES), F32),
        ),
        grid=(b, spt),
        in_specs=[
            pl.BlockSpec((tm, d), lambda bi, i: (bi * spt + i, 0)),
            pl.BlockSpec((1, tm, d), lambda bi, i: (bi, i, 0)),
            pl.BlockSpec((1, 1, d), lambda bi, i: (bi, 0, 0)),
            pl.BlockSpec((1, d), lambda bi, i: (0, 0)),
            pl.BlockSpec((1, 1, d), lambda bi, i: (bi, 0, 0)),
            pl.BlockSpec((1, 1, d), lambda bi, i: (bi, 0, 0)),
            pl.BlockSpec((d, d), lambda bi, i: (0, 0)),
            pl.BlockSpec((d, LANES), lambda bi, i: (0, 0)),
            pl.BlockSpec((d, LANES), lambda bi, i: (0, 0)),
            pl.BlockSpec((1, LANES), lambda bi, i: (0, 0)),
        ],
        out_specs=(
            pl.BlockSpec((1, tm, d), lambda bi, i: (bi, i, 0)),
            pl.BlockSpec((tm, d // 2), lambda bi, i: (bi * spt + i, 0)),
            pl.BlockSpec((tm, LANES), lambda bi, i: (bi * spt + i, 0)),
        ),
        compiler_params=_params(("arbitrary", "arbitrary"), 48),
        name="oproj",
    )(m, x, g1, norm_g, sh2, sc2, w_o, w_r_hi, w_r_lo, b_r)


def _route_kernel(lg_ref, e_ref, w_ref, r_ref, cnt_ref, run_scr):
    i = pl.program_id(0)
    tm = lg_ref.shape[0]

    @pl.when(i == 0)
    def _():
        run_scr[...] = jnp.zeros_like(run_scr)

    l = lg_ref[...]
    lane = lax.broadcasted_iota(jnp.int32, l.shape, 1)
    vals, idxs, hots = [], [], []
    for _ in range(TOP_K):
        m = jnp.max(l, axis=-1, keepdims=True)
        idx = jnp.min(jnp.where(l == m, lane, LANES), axis=-1, keepdims=True)
        hot = lane == idx
        l = jnp.where(hot, -jnp.inf, l)
        vals.append(m)
        idxs.append(idx)
        hots.append(hot)
    exps = [jnp.exp(v - vals[0]) for v in vals]
    tot = exps[0]
    for e in exps[1:]:
        tot = tot + e

    member = hots[0]
    for hot in hots[1:]:
        member = member | hot
    member = member.astype(F32)
    ri = lax.broadcasted_iota(jnp.int32, (tm, tm), 0)
    ci = lax.broadcasted_iota(jnp.int32, (tm, tm), 1)
    lower = (ci < ri).astype(BF16)
    before = jnp.dot(lower, member.astype(BF16), preferred_element_type=F32) + run_scr[...]

    e_out = jnp.zeros(l.shape, jnp.int32)
    w_out = jnp.zeros(l.shape, F32)
    r_out = jnp.zeros(l.shape, jnp.int32)
    for k in range(TOP_K):
        rank = jnp.sum(jnp.where(hots[k], before, 0.0), axis=-1, keepdims=True).astype(jnp.int32)
        e_out = jnp.where(lane == k, idxs[k], e_out)
        w_out = jnp.where(lane == k, exps[k] / tot, w_out)
        r_out = jnp.where(lane == k, rank, r_out)
    e_ref[...] = e_out[:, :TOP_K]
    w_ref[...] = w_out
    r_ref[...] = r_out[:, :TOP_K]
    run_scr[...] += jnp.sum(member, axis=0, keepdims=True)
    cnt_ref[...] = run_scr[...].astype(jnp.int32)


def _route(logits):
    t = logits.shape[0]
    tm = ROUTE_TM
    row = pl.BlockSpec((tm, LANES), lambda i: (i, 0))
    narrow = pl.BlockSpec((tm, TOP_K), lambda i: (i, 0))
    return pl.pallas_call(
        _route_kernel,
        out_shape=(
            jax.ShapeDtypeStruct((t, TOP_K), jnp.int32),
            jax.ShapeDtypeStruct((t, LANES), F32),
            jax.ShapeDtypeStruct((t, TOP_K), jnp.int32),
            jax.ShapeDtypeStruct((1, LANES), jnp.int32),
        ),
        grid=(t // tm,),
        in_specs=[row],
        out_specs=(narrow, row, narrow, pl.BlockSpec((1, LANES), lambda i: (0, 0))),
        scratch_shapes=[pltpu.VMEM((1, LANES), F32)],
        compiler_params=_params(("arbitrary",), 32),
        name="route",
    )(logits)


def _dispatch_kernel(dest_ref, h_ref, xb_ref, sem):
    tm = h_ref.shape[0]

    for t in range(tm):
        for k in range(TOP_K):
            pltpu.make_async_copy(h_ref.at[pl.ds(t, 1)], xb_ref.at[pl.ds(dest_ref[TOP_K * t + k], 1)], sem).start()
    for k in range(TOP_K):
        pltpu.make_async_copy(h_ref, xb_ref.at[pl.ds(0, tm)], sem).wait()


def _dispatch(h2, dest_flat):
    t, d = h2.shape
    tm = DISPATCH_TM
    return pl.pallas_call(
        _dispatch_kernel,
        out_shape=jax.ShapeDtypeStruct((t * TOP_K, d), h2.dtype),
        grid=(t // tm,),
        in_specs=[
            pl.BlockSpec((tm * TOP_K,), lambda i: (i,), memory_space=pltpu.SMEM),
            pl.BlockSpec((tm, d), lambda i: (i, 0)),
        ],
        out_specs=pl.BlockSpec(memory_space=pl.ANY),
        scratch_shapes=[pltpu.SemaphoreType.DMA],
        compiler_params=_params(("arbitrary",), 32),
        name="dispatch",
    )(dest_flat, h2)


def _expert_kernel(tile_ref, exp_ref, lo_ref, hi_ref, first_ref,
                   x_ref, wg_ref, bg_ref, wu_ref, bu_ref, wd_ref, bd_ref, o_ref, xs_scr):
    w = pl.program_id(0)
    j = pl.program_id(1)
    lo = lo_ref[w]
    hi = hi_ref[w]
    sub = EXPERT_SUB
    half = x_ref.shape[1]

    @pl.when((j == 0) & (first_ref[w] == 1))
    def _():
        x_lo, x_hi = _unpack_bf16_pairs(x_ref[...])
        xs_scr[:, :half] = x_lo
        xs_scr[:, half:] = x_hi
        o_ref[...] = jnp.zeros_like(o_ref)

    def ffn_tile(s):
        rows = slice(s * sub, (s + 1) * sub)
        xs = xs_scr[rows, :]
        gate = jnp.dot(xs, wg_ref[0].astype(BF16), preferred_element_type=F32) + bg_ref[0]
        up = jnp.dot(xs, wu_ref[0].astype(BF16), preferred_element_type=F32) + bu_ref[0]
        gate = jnp.minimum(gate, SWIGLU_LIMIT)
        up = jnp.clip(up, -SWIGLU_LIMIT, SWIGLU_LIMIT)
        act = gate * _sigmoid(SWIGLU_ALPHA * gate) * (up + 1.0)
        row = lax.broadcasted_iota(jnp.int32, (sub, 1), 0) + s * sub
        mine = (row >= lo) & (row < hi)
        act = jnp.where(mine, act, 0.0).astype(BF16)
        y = jnp.dot(act, wd_ref[0].astype(BF16), preferred_element_type=F32)
        o_ref[rows, :] += y + jnp.where(mine & (j == 0), bd_ref[0], 0.0)

    n_sub = x_ref.shape[0] // sub
    active = [(lo < (s + 1) * sub) & (hi > s * sub) for s in range(n_sub)]
    whole = functools.reduce(jnp.logical_and, active)

    @pl.when(whole)
    def _():
        for s in range(n_sub):
            ffn_tile(s)

    partial = jnp.logical_not(whole)
    for p in range(n_sub // 2):
        s0, s1 = 2 * p, 2 * p + 1

        @pl.when(partial & active[s0] & active[s1])
        def _(s0=s0, s1=s1):
            ffn_tile(s0)
            ffn_tile(s1)

        @pl.when(partial & active[s0] & jnp.logical_not(active[s1]))
        def _(s0=s0):
            ffn_tile(s0)

        @pl.when(partial & active[s1] & jnp.logical_not(active[s0]))
        def _(s1=s1):
            ffn_tile(s1)


def _experts(xb, items, w_gate, b_gate, w_up, b_up, w_down, b_down):
    r, half = xb.shape
    e, d, f = w_gate.shape
    tm, tf = EXPERT_TM, EXPERT_TF
    nj = f // tf
    tile, expert, lo, hi, first = items
    n_items = tile.shape[0]

    def jj(j, w, hi_ref, lo_ref):
        return jnp.where(hi_ref[w] > lo_ref[w], j, nj - 1)

    return pl.pallas_call(
        _expert_kernel,
        out_shape=jax.ShapeDtypeStruct((r, d), F32),
        grid_spec=pltpu.PrefetchScalarGridSpec(
            num_scalar_prefetch=5,
            grid=(n_items, nj),
            in_specs=[
                pl.BlockSpec((tm, half), lambda w, j, ti, ex, lo_, hi_, fi: (ti[w], 0)),
                pl.BlockSpec((1, d, tf), lambda w, j, ti, ex, lo_, hi_, fi: (ex[w], 0, jj(j, w, hi_, lo_))),
                pl.BlockSpec((1, 1, tf), lambda w, j, ti, ex, lo_, hi_, fi: (ex[w], 0, jj(j, w, hi_, lo_))),
                pl.BlockSpec((1, d, tf), lambda w, j, ti, ex, lo_, hi_, fi: (ex[w], 0, jj(j, w, hi_, lo_))),
                pl.BlockSpec((1, 1, tf), lambda w, j, ti, ex, lo_, hi_, fi: (ex[w], 0, jj(j, w, hi_, lo_))),
                pl.BlockSpec((1, tf, d), lambda w, j, ti, ex, lo_, hi_, fi: (ex[w], jj(j, w, hi_, lo_), 0)),
                pl.BlockSpec((1, 1, d), lambda w, j, ti, ex, lo_, hi_, fi: (ex[w], 0, 0)),
            ],
            out_specs=pl.BlockSpec((tm, d), lambda w, j, ti, ex, lo_, hi_, fi: (ti[w], 0)),
            scratch_shapes=[pltpu.VMEM((tm, d), BF16)],
        ),
        compiler_params=_params(("arbitrary", "arbitrary"), 60),
        name="experts",
    )(tile, expert, lo, hi, first, xb, w_gate, b_gate.reshape(e, 1, f), w_up, b_up.reshape(e, 1, f),
      w_down, b_down.reshape(e, 1, d))


def _work_items(counts, n_rows):
    tm = EXPERT_TM
    n_tiles = n_rows // tm
    n_items = n_tiles + N_EXPERTS - 1
    cum = jnp.cumsum(counts)
    start = cum - counts
    tile_lo = jnp.arange(n_tiles, dtype=jnp.int32) * tm
    e_lo = jnp.searchsorted(cum, tile_lo, side="right").astype(jnp.int32)
    e_hi = jnp.searchsorted(cum, tile_lo + tm - 1, side="right").astype(jnp.int32)
    per_tile = e_hi - e_lo + 1
    off = jnp.cumsum(per_tile) - per_tile
    total = jnp.sum(per_tile)
    w = jnp.arange(n_items, dtype=jnp.int32)
    valid = w < total
    tile = jnp.clip(jnp.searchsorted(off, w, side="right").astype(jnp.int32) - 1, 0, n_tiles - 1)
    expert = jnp.where(valid, e_lo[tile] + w - off[tile], e_hi[n_tiles - 1])
    tile = jnp.where(valid, tile, n_tiles - 1)
    lo = jnp.clip(start[expert] - tile * tm, 0, tm)
    hi = jnp.clip(cum[expert] - tile * tm, 0, tm)
    hi = jnp.where(valid, jnp.maximum(hi, lo), lo)
    first = (valid & (w == off[tile])).astype(jnp.int32)
    return tile, expert.astype(jnp.int32), lo.astype(jnp.int32), hi.astype(jnp.int32), first, start


def _combine_kernel(dest_ref, dnext_ref, dnext2_ref, yb_ref, w_ref, x1_ref, g2_ref, fg_ref, o_ref, buf, sem):
    i = pl.program_id(0)
    n = pl.num_programs(0)
    tm = x1_ref.shape[0]
    slot = i % COMBINE_SLOTS

    def gather(d_ref, into):
        for t in range(tm):
            for k in range(TOP_K):
                pltpu.make_async_copy(yb_ref.at[pl.ds(d_ref[TOP_K * t + k], 1)], buf.at[into, k, pl.ds(t, 1)],
                                      sem.at[into]).start()

    def reduce():
        wts = w_ref[...]
        moe = buf[slot, 0] * wts[:, 0:1]
        for k in range(1, TOP_K):
            moe += buf[slot, k] * wts[:, k:k + 1]
        x2 = x1_ref[...] + g2_ref[0] * moe
        ms = jnp.mean(x2 * x2, axis=-1, keepdims=True)
        o_ref[...] = x2 * lax.rsqrt(ms + NORM_EPS) * fg_ref[...]

    @pl.when(i == 0)
    def _():
        gather(dest_ref, 0)
        gather(dnext_ref, 1)

    for k in range(TOP_K):
        pltpu.make_async_copy(yb_ref.at[pl.ds(0, tm)], buf.at[slot, k], sem.at[slot]).wait()

    @pl.when(i + 2 < n)
    def _():
        gather(dnext2_ref, (i + 2) % COMBINE_SLOTS)
        reduce()

    @pl.when(i + 2 >= n)
    def _():
        reduce()


def _combine(yb, dest_flat, top_w, x1, g2, final_g):
    b, s, d = x1.shape
    tm = COMBINE_TM
    spt = s // tm
    n = b * spt
    assert n >= COMBINE_SLOTS
    out = pl.pallas_call(
        _combine_kernel,
        out_shape=jax.ShapeDtypeStruct((b * s, d), F32),
        grid=(n,),
        in_specs=[
            pl.BlockSpec((tm * TOP_K,), lambda i: (i,), memory_space=pltpu.SMEM),
            pl.BlockSpec((tm * TOP_K,), lambda i: (jnp.minimum(i + 1, n - 1),), memory_space=pltpu.SMEM),
            pl.BlockSpec((tm * TOP_K,), lambda i: (jnp.minimum(i + 2, n - 1),), memory_space=pltpu.SMEM),
            pl.BlockSpec(memory_space=pl.ANY),
            pl.BlockSpec((tm, LANES), lambda i: (i, 0)),
            pl.BlockSpec((tm, d), lambda i: (i, 0)),
            pl.BlockSpec((1, 1, d), lambda i: (i // spt, 0, 0)),
            pl.BlockSpec((1, d), lambda i: (0, 0)),
        ],
        out_specs=pl.BlockSpec((tm, d), lambda i: (i, 0)),
        scratch_shapes=[pltpu.VMEM((COMBINE_SLOTS, TOP_K, tm, d), F32), pltpu.SemaphoreType.DMA((COMBINE_SLOTS,))],
        compiler_params=_params(("arbitrary",), 40),
        name="combine",
    )(dest_flat, dest_flat, dest_flat, yb, top_w, x1.reshape(b * s, d), g2, final_g)
    return out.reshape(b, s, d)


def kernel(x, c, ctx, c_ctx, ada_w, ada_b, norm1_g, norm2_g, w_in, w_pa, w_pb, w_o, ret_decay_fwd, ret_decay_bwd,
           na_rpb, w_router, b_router, w_gate, b_gate, w_up, b_up, w_down, b_down, final_g):
    assert ada_w.shape[0] == 1, "single layer"
    b, s, d = x.shape
    l = ctx.shape[1]
    in_w = w_in.shape[2]
    rows = s // GRID_W
    assert s % IN_TM == 0 and l % RET_CHUNK == 0 and rows >= NA_WROWS and rows % NA_QROWS == 0

    c_rows = jnp.zeros((16, d), F32).at[:b].set(c).at[b].set(c_ctx)
    mod = _ada(c_rows, ada_w[0], ada_b[0][None, :])
    sh1, sc1, g1, sh2, sc2, g2 = [mod[:b, None, i * d:(i + 1) * d] for i in range(6)]
    shc1 = mod[b, 0 * d:1 * d][None, None, :]
    scc1 = mod[b, 1 * d:2 * d][None, None, :]

    w_in0 = w_in[0]
    n1 = norm1_g[0][None, :]
    proj = _inproj(x, n1, sh1, sc1, w_in0, tuple(range(in_w // IN_TN)), IN_TM)
    qk_w = RET_HEADS * RET_DK
    v_w = RET_HEADS * RET_DV
    na_w = NA_HEADS * NA_DH
    ctx_cols = tuple(range(qk_w // IN_TN, (2 * qk_w + v_w) // IN_TN)) + tuple(
        range((2 * qk_w + 2 * v_w + na_w) // IN_TN, (2 * qk_w + 2 * v_w + 3 * na_w) // IN_TN))
    projc = _inproj(ctx.reshape(1, b * l, d), n1, shc1, scc1, w_in0, ctx_cols, b * l).reshape(b, l, -1)

    lgf = jax.nn.log_sigmoid(ret_decay_fwd[0].astype(F32))
    lgb = jax.nn.log_sigmoid(ret_decay_bwd[0].astype(F32))
    cos_t, sin_t = _rope_tables(s)
    ret_in = _retention(proj, projc, lgf, lgb, cos_t, sin_t)
    na_in = _natt(proj, projc, _natt_pair_tiles(na_rpb[0]))

    t = b * s
    m = _merge(ret_in.reshape(t, -1), na_in.reshape(t, -1), proj.reshape(t, in_w),
               w_pa[0], w_pb[0])

    w_r = jnp.zeros((d, LANES), F32).at[:, :N_EXPERTS].set(w_router[0])
    b_r = jnp.full((1, LANES), NEG_INF, F32).at[0, :N_EXPERTS].set(b_router[0])
    x1, h2, logits = _oproj(m, x, g1, norm2_g[0][None, :], sh2, sc2, w_o[0].astype(BF16), w_r, b_r)

    top_e, top_w, rank, counts = _route(logits)
    counts = counts[0, :N_EXPERTS]
    items = _work_items(counts, t * TOP_K)
    start = items[5]
    is_e = top_e[:, :, None] == jnp.arange(N_EXPERTS, dtype=jnp.int32)
    dest = (jnp.sum(jnp.where(is_e, start.astype(jnp.int32), 0), axis=-1) + rank).reshape(-1)

    xb = _dispatch(h2, dest)
    yb = _experts(xb, items[:5], w_gate[0], b_gate[0], w_up[0], b_up[0], w_down[0], b_down[0])
    return _combine(yb, dest, top_w, x1, g2, final_g[None, :])
```

```python
import functools

import jax
import jax.numpy as jnp
import numpy as np
from jax import lax
from jax.experimental import pallas as pl
from jax.experimental.pallas import tpu as pltpu

F32 = jnp.float32
BF16 = jnp.bfloat16

GRID_W = 64
RET_HEADS = 8
RET_DK = 128
RET_DV = 256
RET_CHUNK = 128
ROPE_AXIS_DIM = RET_DK // 2
ROPE_BASE = 10000.0
NA_HEADS = 16
NA_DH = 128
NA_KR = 8
NA_KC = 16
N_EXPERTS = 32
TOP_K = 4
SWIGLU_ALPHA = 1.702
SWIGLU_LIMIT = 7.0
NORM_EPS = 1e-6
GN_EPS = 1e-5
NEG_INF = -1e30
LOG2_E = 1.4426950408889634

LANES = 128
MIB = 1024 * 1024

NA_QROWS = 4
NA_WROWS = NA_QROWS + NA_KR

IN_TM, IN_TN = 1024, 512
MERGE_TM, MERGE_TN = 1024, 512
OPROJ_TM = 256
ROUTE_TM = 512
DISPATCH_TM = 256
EXPERT_TM, EXPERT_SUB, EXPERT_TF = 1024, 256, 512
COMBINE_TM = 128
COMBINE_SLOTS = 3


def _params(semantics, vmem_mib):
    return pltpu.CompilerParams(dimension_semantics=semantics, vmem_limit_bytes=vmem_mib * MIB)


def _sigmoid(x):
    return 1.0 / (1.0 + jnp.exp(-x))


def _ada_kernel(c_ref, w_ref, b_ref, o_ref):
    c = c_ref[...]
    s = c * _sigmoid(c)
    o_ref[...] = jnp.dot(s.astype(BF16), w_ref[...].astype(BF16), preferred_element_type=F32) + b_ref[...]


def _ada(c_rows, w, b):
    r, d = c_rows.shape
    n = w.shape[1]
    tn = 1024
    return pl.pallas_call(
        _ada_kernel,
        out_shape=jax.ShapeDtypeStruct((r, n), F32),
        grid=(n // tn,),
        in_specs=[
            pl.BlockSpec((r, d), lambda j: (0, 0)),
            pl.BlockSpec((d, tn), lambda j: (0, j)),
            pl.BlockSpec((1, tn), lambda j: (0, j)),
        ],
        out_specs=pl.BlockSpec((r, tn), lambda j: (0, j)),
        compiler_params=_params(("arbitrary",), 40),
        name="ada",
    )(c_rows, w, b)


def _inproj_kernel(cols_ref, x_ref, g_ref, sh_ref, sc_ref, w_ref, o_ref, h_scr):
    @pl.when(pl.program_id(2) == 0)
    def _():
        x = x_ref[0]
        ms = jnp.mean(x * x, axis=-1, keepdims=True)
        y = x * lax.rsqrt(ms + NORM_EPS) * g_ref[...]
        h_scr[...] = (y * (1.0 + sc_ref[0]) + sh_ref[0]).astype(BF16)

    o_ref[0] = jnp.dot(h_scr[...], w_ref[...].astype(BF16), preferred_element_type=F32).astype(o_ref.dtype)


def _inproj(x, g, shift, scale, w, col_tiles, tm):
    b, n, d = x.shape
    tn = IN_TN
    nct = len(col_tiles)
    cols = jnp.asarray(np.asarray(col_tiles, np.int32))
    return pl.pallas_call(
        _inproj_kernel,
        out_shape=jax.ShapeDtypeStruct((b, n, nct * tn), BF16),
        grid_spec=pltpu.PrefetchScalarGridSpec(
            num_scalar_prefetch=1,
            grid=(b, n // tm, nct),
            in_specs=[
                pl.BlockSpec((1, tm, d), lambda bi, i, j, c: (bi, i, 0)),
                pl.BlockSpec((1, d), lambda bi, i, j, c: (0, 0)),
                pl.BlockSpec((1, 1, d), lambda bi, i, j, c: (bi, 0, 0)),
                pl.BlockSpec((1, 1, d), lambda bi, i, j, c: (bi, 0, 0)),
                pl.BlockSpec((d, tn), lambda bi, i, j, c: (0, c[j])),
            ],
            out_specs=pl.BlockSpec((1, tm, tn), lambda bi, i, j, c: (bi, i, j)),
            scratch_shapes=[pltpu.VMEM((tm, d), BF16)],
        ),
        compiler_params=_params(("arbitrary", "arbitrary", "arbitrary"), 48),
        name="inproj",
    )(cols, x, g, shift, scale, w)


def _swap_halves(x):
    lane = lax.broadcasted_iota(jnp.int32, x.shape, 1)
    return jnp.where(lane % 64 < 32, pltpu.roll(x, 96, 1), pltpu.roll(x, 32, 1))


def _ret_kernel(lgf_ref, lgb_ref, q_ref, k_ref, v_ref, g_ref, kc_ref, vc_ref, cos_ref, sin_ref,
                o_ref, ks_scr, rf_scr, rb_scr, ub_scr):
    h = pl.program_id(1)
    c = RET_CHUNK
    n = q_ref.shape[1]
    nc = n // c
    ncc = kc_ref.shape[1] // c
    lgf = lgf_ref[h]
    lgb = lgb_ref[h]
    k_scale = RET_DK ** -0.5

    pos_c = lax.broadcasted_iota(jnp.int32, (c, 1), 0).astype(F32)
    zeta_f = jnp.exp((c - 1.0 - pos_c) * lgf)
    zeta_b = jnp.exp(pos_c * lgb)
    xi_f = jnp.exp((pos_c + 1.0) * lgf)
    xi_b = jnp.exp((c - pos_c) * lgb)
    one = jnp.ones((1, 1), F32)
    gc_f = jnp.exp(one * (c * lgf))
    gc_b = jnp.exp(one * (c * lgb))
    ii = lax.broadcasted_iota(jnp.int32, (c, c), 0)
    jj = lax.broadcasted_iota(jnp.int32, (c, c), 1)
    diff = (ii - jj).astype(F32)
    dmask = (jnp.where(diff >= 0, jnp.exp(jnp.maximum(diff, 0.0) * lgf), 0.0)
             + jnp.where(diff <= 0, jnp.exp(jnp.maximum(-diff, 0.0) * lgb), 0.0))

    def ktv(k_bf, v_f32, zeta):
        return jnp.dot(k_bf.astype(F32).T.astype(BF16), (v_f32 * zeta).astype(BF16), preferred_element_type=F32)

    r_f = jnp.zeros((RET_DK, RET_DV), F32)
    for i in range(ncc):
        kc = (kc_ref[0, i * c:(i + 1) * c, :].astype(F32) * k_scale).astype(BF16)
        r_f = gc_f * r_f + ktv(kc, vc_ref[0, i * c:(i + 1) * c, :].astype(F32), zeta_f)
    r_b = jnp.zeros((RET_DK, RET_DV), F32)
    for i in reversed(range(ncc)):
        kc = (kc_ref[0, i * c:(i + 1) * c, :].astype(F32) * k_scale).astype(BF16)
        r_b = gc_b * r_b + ktv(kc, vc_ref[0, i * c:(i + 1) * c, :].astype(F32), zeta_b)

    def rope(x, rows):
        x = x.astype(F32)
        return x * cos_ref[rows, :] + _swap_halves(x) * sin_ref[rows, :]

    def chunk_updates(i, r):
        rows = pl.ds(pl.multiple_of(i * c, c), c)
        kb = (rope(k_ref[0, rows, :], rows) * k_scale).astype(BF16)
        ks_scr[rows, :] = kb
        kt = kb.astype(F32).T.astype(BF16)
        v = v_ref[0, rows, :].astype(F32)
        rf_scr[i] = r.astype(BF16)
        ub_scr[i] = jnp.dot(kt, (v * zeta_b).astype(BF16), preferred_element_type=F32)
        return gc_f * r + jnp.dot(kt, (v * zeta_f).astype(BF16), preferred_element_type=F32)

    lax.fori_loop(0, nc, chunk_updates, r_f, unroll=8)

    def bwd_scan(t, r):
        i = nc - 1 - t
        rb_scr[i] = r.astype(BF16)
        return gc_b * r + ub_scr[i]

    lax.fori_loop(0, nc, bwd_scan, r_b, unroll=2)

    def out_chunk(i, carry):
        rows = pl.ds(pl.multiple_of(i * c, c), c)
        q = rope(q_ref[0, rows, :], rows)
        kb = ks_scr[rows, :]
        vb = v_ref[0, rows, :]
        s = lax.dot_general(q.astype(BF16), kb, (((1,), (1,)), ((), ())), preferred_element_type=F32) * dmask
        o = jnp.dot(s.astype(BF16), vb, preferred_element_type=F32)
        o += jnp.dot((q * xi_f).astype(BF16), rf_scr[i], preferred_element_type=F32)
        o += jnp.dot((q * xi_b).astype(BF16), rb_scr[i], preferred_element_type=F32)
        mu = jnp.mean(o, axis=-1, keepdims=True)
        d = o - mu
        var = jnp.mean(d * d, axis=-1, keepdims=True)
        on = d * lax.rsqrt(var + GN_EPS)
        g = g_ref[0, rows, :].astype(F32)
        o_ref[0, rows, :] = (g * _sigmoid(g) * on).astype(o_ref.dtype)
        return carry

    lax.fori_loop(0, nc, out_chunk, 0, unroll=8)


def _retention(proj, projc, lgf, lgb, cos_t, sin_t):
    b, s, _ = proj.shape
    l = projc.shape[1]
    hq = RET_HEADS
    v_off = 2 * hq * RET_DK // RET_DV
    g_off = v_off + hq
    cv_off = hq * RET_DK // RET_DV
    smem = pl.BlockSpec(memory_space=pltpu.SMEM)
    return pl.pallas_call(
        _ret_kernel,
        out_shape=jax.ShapeDtypeStruct((b, s, hq * RET_DV), BF16),
        grid=(b, hq),
        in_specs=[
            smem, smem,
            pl.BlockSpec((1, s, RET_DK), lambda bi, h: (bi, 0, h)),
            pl.BlockSpec((1, s, RET_DK), lambda bi, h: (bi, 0, hq + h)),
            pl.BlockSpec((1, s, RET_DV), lambda bi, h: (bi, 0, v_off + h)),
            pl.BlockSpec((1, s, RET_DV), lambda bi, h: (bi, 0, g_off + h)),
            pl.BlockSpec((1, l, RET_DK), lambda bi, h: (bi, 0, h)),
            pl.BlockSpec((1, l, RET_DV), lambda bi, h: (bi, 0, cv_off + h)),
            pl.BlockSpec((s, RET_DK), lambda bi, h: (0, 0)),
            pl.BlockSpec((s, RET_DK), lambda bi, h: (0, 0)),
        ],
        out_specs=pl.BlockSpec((1, s, RET_DV), lambda bi, h: (bi, 0, h)),
        scratch_shapes=[
            pltpu.VMEM((s, RET_DK), BF16),
            pltpu.VMEM((s // RET_CHUNK, RET_DK, RET_DV), BF16),
            pltpu.VMEM((s // RET_CHUNK, RET_DK, RET_DV), BF16),
            pltpu.VMEM((s // RET_CHUNK, RET_DK, RET_DV), F32),
        ],
        compiler_params=_params(("arbitrary", "arbitrary"), 48),
        name="ret",
    )(lgf, lgb, proj, proj, proj, proj, projc, projc, cos_t, sin_t)


def _rope_tables(n):
    rows = n // GRID_W
    inv = ROPE_BASE ** (-jnp.arange(0, ROPE_AXIS_DIM, 2, dtype=F32) / ROPE_AXIS_DIM)
    ar = jnp.arange(rows, dtype=F32)[:, None] * inv
    ac = jnp.arange(GRID_W, dtype=F32)[:, None] * inv
    f = inv.shape[0]

    def per_token(row_part, col_part):
        row_part = jnp.broadcast_to(row_part[:, None, :], (rows, GRID_W, 2 * f))
        col_part = jnp.broadcast_to(col_part[None, :, :], (rows, GRID_W, 2 * f))
        return jnp.concatenate([row_part, col_part], axis=2).reshape(n, 4 * f)

    cos_t = per_token(jnp.concatenate([jnp.cos(ar), jnp.cos(ar)], axis=1),
                      jnp.concatenate([jnp.cos(ac), jnp.cos(ac)], axis=1))
    sin_t = per_token(jnp.concatenate([-jnp.sin(ar), jnp.sin(ar)], axis=1),
                      jnp.concatenate([-jnp.sin(ac), jnp.sin(ac)], axis=1))
    return cos_t, sin_t


def _natt_kernel(q_ref, k_ref, v_ref, kc_ref, vc_ref, pair_ref, o_ref, bias_scr, v1_scr, vc1_scr):
    @pl.when(pl.program_id(1) == 0)
    def _():
        _natt_fill_bias(pair_ref, bias_scr)

    n = q_ref.shape[1]
    rows = n // GRID_W
    n_blk = rows // NA_QROWS
    nq = NA_QROWS * GRID_W
    nk = NA_WROWS * GRID_W
    scale = NA_DH ** -0.5
    kc = kc_ref[0]
    nt = (((1,), (1,)), ((), ()))
    v1_scr[:, :NA_DH] = v_ref[0]
    v1_scr[:, NA_DH:] = jnp.ones((n, NA_DH), BF16)
    vc1_scr[:, :NA_DH] = vc_ref[0]
    vc1_scr[:, NA_DH:] = jnp.ones((vc_ref.shape[1], NA_DH), BF16)

    def block(blk, carry):
        ws = jnp.clip(blk * NA_QROWS - NA_KR // 2, 0, rows - NA_WROWS)
        variant = jnp.where(blk == 0, 0, jnp.where(blk == n_blk - 1, 2, 1))
        qrows = pl.ds(pl.multiple_of(blk * nq, nq), nq)
        krows = pl.ds(pl.multiple_of(ws * GRID_W, GRID_W), nk)
        q = q_ref[0, qrows, :]
        s_loc = lax.dot_general(q, k_ref[0, krows, :], nt, preferred_element_type=F32) + bias_scr[variant]
        s_ctx = lax.dot_general(q, kc, nt, preferred_element_type=F32)
        m = jnp.maximum(jnp.max(s_loc, axis=-1, keepdims=True), jnp.max(s_ctx, axis=-1, keepdims=True))
        p_loc = jnp.exp2((s_loc - m) * (scale * LOG2_E))
        p_ctx = jnp.exp2((s_ctx - m) * (scale * LOG2_E))
        o = jnp.dot(p_loc.astype(BF16), v1_scr[krows, :], preferred_element_type=F32)
        o += jnp.dot(p_ctx.astype(BF16), vc1_scr[...], preferred_element_type=F32)
        o_ref[0, qrows, :] = (o[:, :NA_DH] / o[:, NA_DH:]).astype(o_ref.dtype)
        return carry

    lax.fori_loop(0, n_blk, block, 0, unroll=4)


def _natt_row_offsets():
    rq = np.arange(NA_QROWS)[:, None]
    wr = np.arange(NA_WROWS)[None, :]
    dr0 = np.where(wr < NA_KR, wr - rq + NA_KR - 1, -1)
    dr1 = np.where((wr >= rq) & (wr < rq + NA_KR), wr - rq + NA_KR // 2 - 1, -1)
    dr2 = np.where(wr >= NA_QROWS, wr - rq - 1, -1)
    return np.stack([dr0, dr1, dr2])


def _natt_fill_bias(pair_ref, bias_scr):
    w = GRID_W
    dr = _natt_row_offsets()
    neg = jnp.full((w, 2 * w), NEG_INF, F32)
    lane_row = lax.broadcasted_iota(jnp.int32, (w, NA_WROWS * w), 1) // w
    for kind in range(dr.shape[0]):
        for rq in range(NA_QROWS):
            tiles = []
            for p in range(NA_WROWS // 2):
                a, b_ = int(dr[kind, rq, 2 * p]), int(dr[kind, rq, 2 * p + 1])
                if a < 0 and b_ < 0:
                    tiles.append(neg)
                else:
                    tiles.append(pair_ref[0, b_ if b_ >= 0 else a + 1])
            strip = jnp.concatenate(tiles, axis=1)
            valid = np.nonzero(dr[kind, rq] >= 0)[0]
            keep = (lane_row >= int(valid[0])) & (lane_row <= int(valid[-1]))
            bias_scr[kind, rq * w:(rq + 1) * w, :] = jnp.where(keep, strip, NEG_INF)


def _natt_pair_tiles(rpb):
    w = GRID_W
    qc = np.arange(w)[:, None]
    kcol = np.arange(w)[None, :]
    cs = np.clip(qc - NA_KC // 2, 0, w - NA_KC)
    col_ok = (kcol >= cs) & (kcol < cs + NA_KC)
    dc_idx = np.clip(kcol - qc, -(NA_KC - 1), NA_KC - 1) + NA_KC - 1
    pick = (dc_idx.reshape(1, -1) == np.arange(2 * NA_KC - 1)[:, None]).astype(np.float32)
    nh, ndr, ndc = rpb.shape
    t = jnp.dot(rpb.astype(F32).reshape(nh * ndr, ndc), jnp.asarray(pick), precision=lax.Precision.HIGHEST)
    t = jnp.where(col_ok[None, None], t.reshape(nh, ndr, w, w) * NA_DH ** 0.5, NEG_INF)
    neg = jnp.full((rpb.shape[0], 1, w, w), NEG_INF, F32)
    return jnp.concatenate([jnp.concatenate([neg, t], axis=1), jnp.concatenate([t, neg], axis=1)], axis=3)


def _natt(proj, projc, pairs):
    b, s, _ = proj.shape
    l = projc.shape[1]
    nh = NA_HEADS
    q_off = (2 * RET_HEADS * RET_DK + 2 * RET_HEADS * RET_DV) // NA_DH
    k_off = q_off + nh
    v_off = k_off + nh
    ck_off = (RET_HEADS * RET_DK + RET_HEADS * RET_DV) // NA_DH
    cv_off = ck_off + nh
    nq = NA_QROWS * GRID_W
    nk = NA_WROWS * GRID_W
    return pl.pallas_call(
        _natt_kernel,
        out_shape=jax.ShapeDtypeStruct((b, s, nh * NA_DH), BF16),
        grid=(nh, b),
        in_specs=[
            pl.BlockSpec((1, s, NA_DH), lambda h, bi: (bi, 0, q_off + h)),
            pl.BlockSpec((1, s, NA_DH), lambda h, bi: (bi, 0, k_off + h)),
            pl.BlockSpec((1, s, NA_DH), lambda h, bi: (bi, 0, v_off + h)),
            pl.BlockSpec((1, l, NA_DH), lambda h, bi: (bi, 0, ck_off + h)),
            pl.BlockSpec((1, l, NA_DH), lambda h, bi: (bi, 0, cv_off + h)),
            pl.BlockSpec((1, 2 * NA_KR, GRID_W, 2 * GRID_W), lambda h, bi: (h, 0, 0, 0)),
        ],
        out_specs=pl.BlockSpec((1, s, NA_DH), lambda h, bi: (bi, 0, h)),
        scratch_shapes=[pltpu.VMEM((3, nq, nk), F32), pltpu.VMEM((s, 2 * NA_DH), BF16),
                        pltpu.VMEM((l, 2 * NA_DH), BF16)],
        compiler_params=_params(("arbitrary", "arbitrary"), 48),
        name="natt",
    )(proj, proj, proj, projc, projc, pairs)


def _merge_kernel(a_ref, n_ref, ga_ref, gb_ref, wa_ref, wb_ref, o_ref):
    ra = jnp.dot(a_ref[...], wa_ref[...].astype(BF16), preferred_element_type=F32)
    rn = jnp.dot(n_ref[...], wb_ref[...].astype(BF16), preferred_element_type=F32)
    o_ref[...] = (_sigmoid(ga_ref[...].astype(F32)) * ra + _sigmoid(gb_ref[...].astype(F32)) * rn).astype(o_ref.dtype)


def _merge(ret_in, na_in, proj2d, w_pa, w_pb):
    t, d = ret_in.shape
    tm, tn = MERGE_TM, MERGE_TN
    ga_off = (proj2d.shape[1] - 2 * d) // tn
    gb_off = (proj2d.shape[1] - d) // tn
    return pl.pallas_call(
        _merge_kernel,
        out_shape=jax.ShapeDtypeStruct((t, d), BF16),
        grid=(t // tm, d // tn),
        in_specs=[
            pl.BlockSpec((tm, ret_in.shape[1]), lambda i, j: (i, 0)),
            pl.BlockSpec((tm, na_in.shape[1]), lambda i, j: (i, 0)),
            pl.BlockSpec((tm, tn), lambda i, j: (i, ga_off + j)),
            pl.BlockSpec((tm, tn), lambda i, j: (i, gb_off + j)),
            pl.BlockSpec((w_pa.shape[0], tn), lambda i, j: (0, j)),
            pl.BlockSpec((w_pb.shape[0], tn), lambda i, j: (0, j)),
        ],
        out_specs=pl.BlockSpec((tm, tn), lambda i, j: (i, j)),
        compiler_params=_params(("arbitrary", "arbitrary"), 48),
        name="merge",
    )(ret_in, na_in, proj2d, proj2d, w_pa, w_pb)


def _pack_bf16_pairs(lo, hi):
    lo_bits = lax.bitcast_convert_type(lo.astype(F32), jnp.uint32)
    hi_bits = lax.bitcast_convert_type(hi.astype(F32), jnp.uint32)
    return (lo_bits >> 16) | (hi_bits & jnp.uint32(0xFFFF0000))


def _unpack_bf16_pairs(words):
    lo = lax.bitcast_convert_type(words << 16, F32).astype(BF16)
    hi = lax.bitcast_convert_type(words & jnp.uint32(0xFFFF0000), F32).astype(BF16)
    return lo, hi


def _oproj_kernel(m_ref, x_ref, g1_ref, ng_ref, sh_ref, sc_ref, wo_ref, wrh_ref, wrl_ref, br_ref,
                  x1_ref, h2_ref, lg_ref):
    y = jnp.dot(m_ref[...], wo_ref[...], preferred_element_type=F32)
    x1 = x_ref[0] + g1_ref[0] * y
    x1_ref[0] = x1
    ms = jnp.mean(x1 * x1, axis=-1, keepdims=True)
    h2 = x1 * lax.rsqrt(ms + NORM_EPS) * ng_ref[...]
    h2 = h2 * (1.0 + sc_ref[0]) + sh_ref[0]
    h2_hi = h2.astype(BF16)
    half = h2.shape[1] // 2
    h2_ref[...] = _pack_bf16_pairs(h2_hi[:, :half], h2_hi[:, half:])
    h2_lo = (h2 - h2_hi.astype(F32)).astype(BF16)
    lg = jnp.dot(h2_hi, wrh_ref[...], preferred_element_type=F32)
    lg += jnp.dot(h2_lo, wrh_ref[...], preferred_element_type=F32)
    lg += jnp.dot(h2_hi, wrl_ref[...], preferred_element_type=F32)
    lg_ref[...] = lg + br_ref[...]


def _oproj(m, x, g1, norm_g, sh2, sc2, w_o, w_r, b_r):
    b, s, d = x.shape
    tm = OPROJ_TM
    spt = s // tm
    w_r_hi = w_r.astype(BF16)
    w_r_lo = (w_r - w_r_hi.astype(F32)).astype(BF16)
    return pl.pallas_call(
        _oproj_kernel,
        out_shape=(
            jax.ShapeDtypeStruct((b, s, d), F32),
            jax.ShapeDtypeStruct((b * s, d // 2), jnp.uint32),
            jax.ShapeDtypeStruct((b * s, LANES), F32),
        ),
        grid=(b, spt),
        in_specs=[
            pl.BlockSpec((tm, d), lambda bi, i: (bi * spt + i, 0)),
            pl.BlockSpec((1, tm, d), lambda bi, i: (bi, i, 0)),
            pl.BlockSpec((1, 1, d), lambda bi, i: (bi, 0, 0)),
            pl.BlockSpec((1, d), lambda bi, i: (0, 0)),
            pl.BlockSpec((1, 1, d), lambda bi, i: (bi, 0, 0)),
            pl.BlockSpec((1, 1, d), lambda bi, i: (bi, 0, 0)),
            pl.BlockSpec((d, d), lambda bi, i: (0, 0)),
            pl.BlockSpec((d, LANES), lambda bi, i: (0, 0)),
            pl.BlockSpec((d, LANES), lambda bi, i: (0, 0)),
            pl.BlockSpec((1, LANES), lambda bi, i: (0, 0)),
        ],
        out_specs=(
            pl.BlockSpec((1, tm, d), lambda bi, i: (bi, i, 0)),
            pl.BlockSpec((tm, d // 2), lambda bi, i: (bi * spt + i, 0)),
            pl.BlockSpec((tm, LANES), lambda bi, i: (bi * spt + i, 0)),
        ),
        compiler_params=_params(("arbitrary", "arbitrary"), 48),
        name="oproj",
    )(m, x, g1, norm_g, sh2, sc2, w_o, w_r_hi, w_r_lo, b_r)


def _route_kernel(lg_ref, e_ref, w_ref, r_ref, cnt_ref, run_scr):
    i = pl.program_id(0)
    tm = lg_ref.shape[0]

    @pl.when(i == 0)
    def _():
        run_scr[...] = jnp.zeros_like(run_scr)

    l = lg_ref[...]
    lane = lax.broadcasted_iota(jnp.int32, l.shape, 1)
    vals, idxs, hots = [], [], []
    for _ in range(TOP_K):
        m = jnp.max(l, axis=-1, keepdims=True)
        idx = jnp.min(jnp.where(l == m, lane, LANES), axis=-1, keepdims=True)
        hot = lane == idx
        l = jnp.where(hot, -jnp.inf, l)
        vals.append(m)
        idxs.append(idx)
        hots.append(hot)
    exps = [jnp.exp(v - vals[0]) for v in vals]
    tot = exps[0]
    for e in exps[1:]:
        tot = tot + e

    member = hots[0]
    for hot in hots[1:]:
        member = member | hot
    member = member.astype(F32)
    ri = lax.broadcasted_iota(jnp.int32, (tm, tm), 0)
    ci = lax.broadcasted_iota(jnp.int32, (tm, tm), 1)
    lower = (ci < ri).astype(BF16)
    before = jnp.dot(lower, member.astype(BF16), preferred_element_type=F32) + run_scr[...]

    e_out = jnp.zeros(l.shape, jnp.int32)
    w_out = jnp.zeros(l.shape, F32)
    r_out = jnp.zeros(l.shape, jnp.int32)
    for k in range(TOP_K):
        rank = jnp.sum(jnp.where(hots[k], before, 0.0), axis=-1, keepdims=True).astype(jnp.int32)
        e_out = jnp.where(lane == k, idxs[k], e_out)
        w_out = jnp.where(lane == k, exps[k] / tot, w_out)
        r_out = jnp.where(lane == k, rank, r_out)
    e_ref[...] = e_out[:, :TOP_K]
    w_ref[...] = w_out
    r_ref[...] = r_out[:, :TOP_K]
    run_scr[...] += jnp.sum(member, axis=0, keepdims=True)
    cnt_ref[...] = run_scr[...].astype(jnp.int32)


def _route(logits):
    t = logits.shape[0]
    tm = ROUTE_TM
    row = pl.BlockSpec((tm, LANES), lambda i: (i, 0))
    narrow = pl.BlockSpec((tm, TOP_K), lambda i: (i, 0))
    return pl.pallas_call(
        _route_kernel,
        out_shape=(
            jax.ShapeDtypeStruct((t, TOP_K), jnp.int32),
            jax.ShapeDtypeStruct((t, LANES), F32),
            jax.ShapeDtypeStruct((t, TOP_K), jnp.int32),
            jax.ShapeDtypeStruct((1, LANES), jnp.int32),
        ),
        grid=(t // tm,),
        in_specs=[row],
        out_specs=(narrow, row, narrow, pl.BlockSpec((1, LANES), lambda i: (0, 0))),
        scratch_shapes=[pltpu.VMEM((1, LANES), F32)],
        compiler_params=_params(("arbitrary",), 32),
        name="route",
    )(logits)


def _dispatch_kernel(dest_ref, h_ref, xb_ref, sem):
    tm = h_ref.shape[0]

    for t in range(tm):
        for k in range(TOP_K):
            pltpu.make_async_copy(h_ref.at[pl.ds(t, 1)], xb_ref.at[pl.ds(dest_ref[TOP_K * t + k], 1)],
                                  sem).start(priority=k % 2)
    for k in range(TOP_K):
        pltpu.make_async_copy(h_ref, xb_ref.at[pl.ds(0, tm)], sem).wait()


def _dispatch(h2, dest_flat):
    t, d = h2.shape
    tm = DISPATCH_TM
    return pl.pallas_call(
        _dispatch_kernel,
        out_shape=jax.ShapeDtypeStruct((t * TOP_K, d), h2.dtype),
        grid=(t // tm,),
        in_specs=[
            pl.BlockSpec((tm * TOP_K,), lambda i: (i,), memory_space=pltpu.SMEM),
            pl.BlockSpec((tm, d), lambda i: (i, 0)),
        ],
        out_specs=pl.BlockSpec(memory_space=pl.ANY),
        scratch_shapes=[pltpu.SemaphoreType.DMA],
        compiler_params=_params(("arbitrary",), 32),
        name="dispatch",
    )(dest_flat, h2)


def _expert_kernel(tile_ref, exp_ref, lo_ref, hi_ref, first_ref,
                   x_ref, wg_ref, bg_ref, wu_ref, bu_ref, wd_ref, bd_ref, o_ref, xs_scr):
    w = pl.program_id(0)
    j = pl.program_id(1)
    lo = lo_ref[w]
    hi = hi_ref[w]
    sub = EXPERT_SUB
    half = x_ref.shape[1]

    @pl.when((j == 0) & (first_ref[w] == 1))
    def _():
        x_lo, x_hi = _unpack_bf16_pairs(x_ref[...])
        xs_scr[:, :half] = x_lo
        xs_scr[:, half:] = x_hi
        o_ref[...] = jnp.zeros_like(o_ref)

    def ffn_tile(s):
        rows = slice(s * sub, (s + 1) * sub)
        xs = xs_scr[rows, :]
        gate = jnp.dot(xs, wg_ref[0].astype(BF16), preferred_element_type=F32) + bg_ref[0]
        up = jnp.dot(xs, wu_ref[0].astype(BF16), preferred_element_type=F32) + bu_ref[0]
        gate = jnp.minimum(gate, SWIGLU_LIMIT)
        up = jnp.clip(up, -SWIGLU_LIMIT, SWIGLU_LIMIT)
        act = gate * _sigmoid(SWIGLU_ALPHA * gate) * (up + 1.0)
        row = lax.broadcasted_iota(jnp.int32, (sub, 1), 0) + s * sub
        mine = (row >= lo) & (row < hi)
        act = jnp.where(mine, act, 0.0).astype(BF16)
        y = jnp.dot(act, wd_ref[0].astype(BF16), preferred_element_type=F32)
        o_ref[rows, :] += y + jnp.where(mine & (j == 0), bd_ref[0], 0.0)

    n_sub = x_ref.shape[0] // sub
    active = [(lo < (s + 1) * sub) & (hi > s * sub) for s in range(n_sub)]
    whole = functools.reduce(jnp.logical_and, active)

    @pl.when(whole)
    def _():
        for s in range(n_sub):
            ffn_tile(s)

    partial = jnp.logical_not(whole)
    for p in range(n_sub // 2):
        s0, s1 = 2 * p, 2 * p + 1

        @pl.when(partial & active[s0] & active[s1])
        def _(s0=s0, s1=s1):
            ffn_tile(s0)
            ffn_tile(s1)

        @pl.when(partial & active[s0] & jnp.logical_not(active[s1]))
        def _(s0=s0):
            ffn_tile(s0)

        @pl.when(partial & active[s1] & jnp.logical_not(active[s0]))
        def _(s1=s1):
            ffn_tile(s1)


def _experts(xb, items, w_gate, b_gate, w_up, b_up, w_down, b_down):
    r, half = xb.shape
    e, d, f = w_gate.shape
    tm, tf = EXPERT_TM, EXPERT_TF
    nj = f // tf
    tile, expert, lo, hi, first = items
    n_items = tile.shape[0]

    def jj(j, w, hi_ref, lo_ref):
        return jnp.where(hi_ref[w] > lo_ref[w], j, nj - 1)

    return pl.pallas_call(
        _expert_kernel,
        out_shape=jax.ShapeDtypeStruct((r, d), F32),
        grid_spec=pltpu.PrefetchScalarGridSpec(
            num_scalar_prefetch=5,
            grid=(n_items, nj),
            in_specs=[
                pl.BlockSpec((tm, half), lambda w, j, ti, ex, lo_, hi_, fi: (ti[w], 0)),
                pl.BlockSpec((1, d, tf), lambda w, j, ti, ex, lo_, hi_, fi: (ex[w], 0, jj(j, w, hi_, lo_))),
                pl.BlockSpec((1, 1, tf), lambda w, j, ti, ex, lo_, hi_, fi: (ex[w], 0, jj(j, w, hi_, lo_))),
                pl.BlockSpec((1, d, tf), lambda w, j, ti, ex, lo_, hi_, fi: (ex[w], 0, jj(j, w, hi_, lo_))),
                pl.BlockSpec((1, 1, tf), lambda w, j, ti, ex, lo_, hi_, fi: (ex[w], 0, jj(j, w, hi_, lo_))),
                pl.BlockSpec((1, tf, d), lambda w, j, ti, ex, lo_, hi_, fi: (ex[w], jj(j, w, hi_, lo_), 0)),
                pl.BlockSpec((1, 1, d), lambda w, j, ti, ex, lo_, hi_, fi: (ex[w], 0, 0)),
            ],
            out_specs=pl.BlockSpec((tm, d), lambda w, j, ti, ex, lo_, hi_, fi: (ti[w], 0)),
            scratch_shapes=[pltpu.VMEM((tm, d), BF16)],
        ),
        compiler_params=_params(("arbitrary", "arbitrary"), 60),
        name="experts",
    )(tile, expert, lo, hi, first, xb, w_gate, b_gate.reshape(e, 1, f), w_up, b_up.reshape(e, 1, f),
      w_down, b_down.reshape(e, 1, d))


def _work_items(counts, n_rows):
    tm = EXPERT_TM
    n_tiles = n_rows // tm
    n_items = n_tiles + N_EXPERTS - 1
    cum = jnp.cumsum(counts)
    start = cum - counts
    tile_lo = jnp.arange(n_tiles, dtype=jnp.int32) * tm
    e_lo = jnp.searchsorted(cum, tile_lo, side="right").astype(jnp.int32)
    e_hi = jnp.searchsorted(cum, tile_lo + tm - 1, side="right").astype(jnp.int32)
    per_tile = e_hi - e_lo + 1
    off = jnp.cumsum(per_tile) - per_tile
    total = jnp.sum(per_tile)
    w = jnp.arange(n_items, dtype=jnp.int32)
    valid = w < total
    tile = jnp.clip(jnp.searchsorted(off, w, side="right").astype(jnp.int32) - 1, 0, n_tiles - 1)
    expert = jnp.where(valid, e_lo[tile] + w - off[tile], e_hi[n_tiles - 1])
    tile = jnp.where(valid, tile, n_tiles - 1)
    lo = jnp.clip(start[expert] - tile * tm, 0, tm)
    hi = jnp.clip(cum[expert] - tile * tm, 0, tm)
    hi = jnp.where(valid, jnp.maximum(hi, lo), lo)
    first = (valid & (w == off[tile])).astype(jnp.int32)
    return tile, expert.astype(jnp.int32), lo.astype(jnp.int32), hi.astype(jnp.int32), first, start


def _combine_kernel(dest_ref, dnext_ref, dnext2_ref, yb_ref, w_ref, x1_ref, g2_ref, fg_ref, o_ref, buf, sem):
    i = pl.program_id(0)
    n = pl.num_programs(0)
    tm = x1_ref.shape[0]
    slot = i % COMBINE_SLOTS

    def gather(d_ref, into):
        for t in range(tm):
            for k in range(TOP_K):
                pltpu.make_async_copy(yb_ref.at[pl.ds(d_ref[TOP_K * t + k], 1)], buf.at[into, k, pl.ds(t, 1)],
                                      sem.at[into]).start(priority=k % 2)

    def reduce(src):
        wts = w_ref[...]
        moe = buf[src, 0] * wts[:, 0:1]
        for k in range(1, TOP_K):
            moe += buf[src, k] * wts[:, k:k + 1]
        x2 = x1_ref[...] + g2_ref[0] * moe
        ms = jnp.mean(x2 * x2, axis=-1, keepdims=True)
        o_ref[...] = x2 * lax.rsqrt(ms + NORM_EPS) * fg_ref[...]

    @pl.when(i == 0)
    def _():
        gather(dest_ref, 0)
        gather(dnext_ref, 1)

    for k in range(TOP_K):
        pltpu.make_async_copy(yb_ref.at[pl.ds(0, tm)], buf.at[slot, k], sem.at[slot]).wait()

    for v in range(COMBINE_SLOTS):
        @pl.when((slot == v) & (i + 2 < n))
        def _(v=v):
            gather(dnext2_ref, (v + 2) % COMBINE_SLOTS)
            reduce(v)

    @pl.when(i + 2 >= n)
    def _():
        reduce(slot)


def _combine(yb, dest_flat, top_w, x1, g2, final_g):
    b, s, d = x1.shape
    tm = COMBINE_TM
    spt = s // tm
    n = b * spt
    assert n >= COMBINE_SLOTS
    out = pl.pallas_call(
        _combine_kernel,
        out_shape=jax.ShapeDtypeStruct((b * s, d), F32),
        grid=(n,),
        in_specs=[
            pl.BlockSpec((tm * TOP_K,), lambda i: (i,), memory_space=pltpu.SMEM),
            pl.BlockSpec((tm * TOP_K,), lambda i: (jnp.minimum(i + 1, n - 1),), memory_space=pltpu.SMEM),
            pl.BlockSpec((tm * TOP_K,), lambda i: (jnp.minimum(i + 2, n - 1),), memory_space=pltpu.SMEM),
            pl.BlockSpec(memory_space=pl.ANY),
            pl.BlockSpec((tm, LANES), lambda i: (i, 0)),
            pl.BlockSpec((tm, d), lambda i: (i, 0)),
            pl.BlockSpec((1, 1, d), lambda i: (i // spt, 0, 0)),
            pl.BlockSpec((1, d), lambda i: (0, 0)),
        ],
        out_specs=pl.BlockSpec((tm, d), lambda i: (i, 0)),
        scratch_shapes=[pltpu.VMEM((COMBINE_SLOTS, TOP_K, tm, d), F32), pltpu.SemaphoreType.DMA((COMBINE_SLOTS,))],
        compiler_params=_params(("arbitrary",), 40),
        name="combine",
    )(dest_flat, dest_flat, dest_flat, yb, top_w, x1.reshape(b * s, d), g2, final_g)
    return out.reshape(b, s, d)


def kernel(x, c, ctx, c_ctx, ada_w, ada_b, norm1_g, norm2_g, w_in, w_pa, w_pb, w_o, ret_decay_fwd, ret_decay_bwd,
           na_rpb, w_router, b_router, w_gate, b_gate, w_up, b_up, w_down, b_down, final_g):
    assert ada_w.shape[0] == 1, "single layer"
    b, s, d = x.shape
    l = ctx.shape[1]
    in_w = w_in.shape[2]
    rows = s // GRID_W
    assert s % IN_TM == 0 and l % RET_CHUNK == 0 and rows >= NA_WROWS and rows % NA_QROWS == 0

    c_rows = jnp.zeros((16, d), F32).at[:b].set(c).at[b].set(c_ctx)
    mod = _ada(c_rows, ada_w[0], ada_b[0][None, :])
    sh1, sc1, g1, sh2, sc2, g2 = [mod[:b, None, i * d:(i + 1) * d] for i in range(6)]
    shc1 = mod[b, 0 * d:1 * d][None, None, :]
    scc1 = mod[b, 1 * d:2 * d][None, None, :]

    w_in0 = w_in[0]
    n1 = norm1_g[0][None, :]
    proj = _inproj(x, n1, sh1, sc1, w_in0, tuple(range(in_w // IN_TN)), IN_TM)
    qk_w = RET_HEADS * RET_DK
    v_w = RET_HEADS * RET_DV
    na_w = NA_HEADS * NA_DH
    ctx_cols = tuple(range(qk_w // IN_TN, (2 * qk_w + v_w) // IN_TN)) + tuple(
        range((2 * qk_w + 2 * v_w + na_w) // IN_TN, (2 * qk_w + 2 * v_w + 3 * na_w) // IN_TN))
    projc = _inproj(ctx.reshape(1, b * l, d), n1, shc1, scc1, w_in0, ctx_cols, b * l).reshape(b, l, -1)

    lgf = jax.nn.log_sigmoid(ret_decay_fwd[0].astype(F32))
    lgb = jax.nn.log_sigmoid(ret_decay_bwd[0].astype(F32))
    cos_t, sin_t = _rope_tables(s)
    ret_in = _retention(proj, projc, lgf, lgb, cos_t, sin_t)
    na_in = _natt(proj, projc, _natt_pair_tiles(na_rpb[0]))

    t = b * s
    m = _merge(ret_in.reshape(t, -1), na_in.reshape(t, -1), proj.reshape(t, in_w),
               w_pa[0], w_pb[0])

    w_r = jnp.zeros((d, LANES), F32).at[:, :N_EXPERTS].set(w_router[0])
    b_r = jnp.full((1, LANES), NEG_INF, F32).at[0, :N_EXPERTS].set(b_router[0])
    x1, h2, logits = _oproj(m, x, g1, norm2_g[0][None, :], sh2, sc2, w_o[0].astype(BF16), w_r, b_r)

    top_e, top_w, rank, counts = _route(logits)
    counts = counts[0, :N_EXPERTS]
    items = _work_items(counts, t * TOP_K)
    start = items[5]
    is_e = top_e[:, :, None] == jnp.arange(N_EXPERTS, dtype=jnp.int32)
    dest = (jnp.sum(jnp.where(is_e, start.astype(jnp.int32), 0), axis=-1) + rank).reshape(-1)

    xb = _dispatch(h2, dest)
    yb = _experts(xb, items[:5], w_gate[0], b_gate[0], w_up[0], b_up[0], w_down[0], b_down[0])
    return _combine(yb, dest, top_w, x1, g2, final_g[None, :])
```

```python
import functools

import jax
import jax.numpy as jnp
import numpy as np
from jax import lax
from jax.experimental import pallas as pl
from jax.experimental.pallas import tpu as pltpu

F32 = jnp.float32
BF16 = jnp.bfloat16

GRID_W = 64
RET_HEADS = 8
RET_DK = 128
RET_DV = 256
RET_CHUNK = 128
ROPE_AXIS_DIM = RET_DK // 2
ROPE_BASE = 10000.0
NA_HEADS = 16
NA_DH = 128
NA_KR = 8
NA_KC = 16
N_EXPERTS = 32
TOP_K = 4
SWIGLU_ALPHA = 1.702
SWIGLU_LIMIT = 7.0
NORM_EPS = 1e-6
GN_EPS = 1e-5
NEG_INF = -1e30
LOG2_E = 1.4426950408889634

LANES = 128
MIB = 1024 * 1024

NA_QROWS = 4
NA_WROWS = NA_QROWS + NA_KR

IN_TM, IN_TN = 1024, 1024
MERGE_TM, MERGE_TN = 1024, 512
OPROJ_TM = 512
ROUTE_TM = 512
DISPATCH_TM = 256
EXPERT_TM, EXPERT_SUB, EXPERT_TF = 1024, 256, 512
COMBINE_TM = 128
COMBINE_SLOTS = 3


def _params(semantics, vmem_mib):
    return pltpu.CompilerParams(dimension_semantics=semantics, vmem_limit_bytes=vmem_mib * MIB)


def _sigmoid(x):
    return 1.0 / (1.0 + jnp.exp(-x))


def _ada_kernel(c_ref, w_ref, b_ref, o_ref):
    c = c_ref[...]
    s = c * _sigmoid(c)
    o_ref[...] = jnp.dot(s.astype(BF16), w_ref[...].astype(BF16), preferred_element_type=F32) + b_ref[...]


def _ada(c_rows, w, b):
    r, d = c_rows.shape
    n = w.shape[1]
    tn = 1024
    return pl.pallas_call(
        _ada_kernel,
        out_shape=jax.ShapeDtypeStruct((r, n), F32),
        grid=(n // tn,),
        in_specs=[
            pl.BlockSpec((r, d), lambda j: (0, 0)),
            pl.BlockSpec((d, tn), lambda j: (0, j)),
            pl.BlockSpec((1, tn), lambda j: (0, j)),
        ],
        out_specs=pl.BlockSpec((r, tn), lambda j: (0, j)),
        compiler_params=_params(("arbitrary",), 40),
        name="ada",
    )(c_rows, w, b)


def _inproj_kernel(cols_ref, x_ref, g_ref, sh_ref, sc_ref, w_ref, o_ref, h_scr):
    @pl.when(pl.program_id(2) == 0)
    def _():
        x = x_ref[0]
        ms = jnp.mean(x * x, axis=-1, keepdims=True)
        y = x * lax.rsqrt(ms + NORM_EPS) * g_ref[...]
        h_scr[...] = (y * (1.0 + sc_ref[0]) + sh_ref[0]).astype(BF16)

    o_ref[0] = jnp.dot(h_scr[...], w_ref[...].astype(BF16), preferred_element_type=F32).astype(o_ref.dtype)


def _inproj(x, g, shift, scale, w, col_tiles, tm):
    b, n, d = x.shape
    tn = IN_TN
    nct = len(col_tiles)
    cols = jnp.asarray(np.asarray(col_tiles, np.int32))
    return pl.pallas_call(
        _inproj_kernel,
        out_shape=jax.ShapeDtypeStruct((b, n, nct * tn), BF16),
        grid_spec=pltpu.PrefetchScalarGridSpec(
            num_scalar_prefetch=1,
            grid=(b, n // tm, nct),
            in_specs=[
                pl.BlockSpec((1, tm, d), lambda bi, i, j, c: (bi, i, 0)),
                pl.BlockSpec((1, d), lambda bi, i, j, c: (0, 0)),
                pl.BlockSpec((1, 1, d), lambda bi, i, j, c: (bi, 0, 0)),
                pl.BlockSpec((1, 1, d), lambda bi, i, j, c: (bi, 0, 0)),
                pl.BlockSpec((d, tn), lambda bi, i, j, c: (0, c[j])),
            ],
            out_specs=pl.BlockSpec((1, tm, tn), lambda bi, i, j, c: (bi, i, j)),
            scratch_shapes=[pltpu.VMEM((tm, d), BF16)],
        ),
        compiler_params=_params(("arbitrary", "arbitrary", "arbitrary"), 56),
        name="inproj",
    )(cols, x, g, shift, scale, w)


def _swap_halves(x):
    lane = lax.broadcasted_iota(jnp.int32, x.shape, 1)
    return jnp.where(lane % 64 < 32, pltpu.roll(x, 96, 1), pltpu.roll(x, 32, 1))


def _ret_kernel(lgf_ref, lgb_ref, q_ref, k_ref, v_ref, g_ref, kc_ref, vc_ref, cos_ref, sin_ref,
                o_ref, ks_scr, rf_scr, rb_scr, ub_scr):
    h = pl.program_id(1)
    c = RET_CHUNK
    n = q_ref.shape[1]
    nc = n // c
    ncc = kc_ref.shape[1] // c
    lgf = lgf_ref[h]
    lgb = lgb_ref[h]
    k_scale = RET_DK ** -0.5

    pos_c = lax.broadcasted_iota(jnp.int32, (c, 1), 0).astype(F32)
    zeta_f = jnp.exp((c - 1.0 - pos_c) * lgf)
    zeta_b = jnp.exp(pos_c * lgb)
    xi_f = jnp.exp((pos_c + 1.0) * lgf)
    xi_b = jnp.exp((c - pos_c) * lgb)
    one = jnp.ones((1, 1), F32)
    gc_f = jnp.exp(one * (c * lgf))
    gc_b = jnp.exp(one * (c * lgb))
    ii = lax.broadcasted_iota(jnp.int32, (c, c), 0)
    jj = lax.broadcasted_iota(jnp.int32, (c, c), 1)
    diff = (ii - jj).astype(F32)
    dmask = (jnp.where(diff >= 0, jnp.exp(jnp.maximum(diff, 0.0) * lgf), 0.0)
             + jnp.where(diff <= 0, jnp.exp(jnp.maximum(-diff, 0.0) * lgb), 0.0))

    def ktv(k_bf, v_f32, zeta):
        return jnp.dot(k_bf.astype(F32).T.astype(BF16), (v_f32 * zeta).astype(BF16), preferred_element_type=F32)

    r_f = jnp.zeros((RET_DK, RET_DV), F32)
    for i in range(ncc):
        kc = (kc_ref[0, i * c:(i + 1) * c, :].astype(F32) * k_scale).astype(BF16)
        r_f = gc_f * r_f + ktv(kc, vc_ref[0, i * c:(i + 1) * c, :].astype(F32), zeta_f)
    r_b = jnp.zeros((RET_DK, RET_DV), F32)
    for i in reversed(range(ncc)):
        kc = (kc_ref[0, i * c:(i + 1) * c, :].astype(F32) * k_scale).astype(BF16)
        r_b = gc_b * r_b + ktv(kc, vc_ref[0, i * c:(i + 1) * c, :].astype(F32), zeta_b)

    def rope(x, rows):
        x = x.astype(F32)
        return x * cos_ref[rows, :] + _swap_halves(x) * sin_ref[rows, :]

    def chunk_updates(i, r):
        rows = pl.ds(pl.multiple_of(i * c, c), c)
        kb = (rope(k_ref[0, rows, :], rows) * k_scale).astype(BF16)
        ks_scr[rows, :] = kb
        kt = kb.astype(F32).T.astype(BF16)
        v = v_ref[0, rows, :].astype(F32)
        rf_scr[i] = r.astype(BF16)
        ub_scr[i] = jnp.dot(kt, (v * zeta_b).astype(BF16), preferred_element_type=F32)
        return gc_f * r + jnp.dot(kt, (v * zeta_f).astype(BF16), preferred_element_type=F32)

    lax.fori_loop(0, nc, chunk_updates, r_f, unroll=8)

    def bwd_scan(t, r):
        i = nc - 1 - t
        rb_scr[i] = r.astype(BF16)
        return gc_b * r + ub_scr[i]

    lax.fori_loop(0, nc, bwd_scan, r_b, unroll=2)

    def out_chunk(i, carry):
        rows = pl.ds(pl.multiple_of(i * c, c), c)
        q = rope(q_ref[0, rows, :], rows)
        kb = ks_scr[rows, :]
        vb = v_ref[0, rows, :]
        s = lax.dot_general(q.astype(BF16), kb, (((1,), (1,)), ((), ())), preferred_element_type=F32) * dmask
        o = jnp.dot(s.astype(BF16), vb, preferred_element_type=F32)
        o += jnp.dot((q * xi_f).astype(BF16), rf_scr[i], preferred_element_type=F32)
        o += jnp.dot((q * xi_b).astype(BF16), rb_scr[i], preferred_element_type=F32)
        mu = jnp.mean(o, axis=-1, keepdims=True)
        d = o - mu
        var = jnp.mean(d * d, axis=-1, keepdims=True)
        on = d * lax.rsqrt(var + GN_EPS)
        g = g_ref[0, rows, :].astype(F32)
        o_ref[0, rows, :] = (g * _sigmoid(g) * on).astype(o_ref.dtype)
        return carry

    lax.fori_loop(0, nc, out_chunk, 0, unroll=8)


def _retention(proj, projc, lgf, lgb, cos_t, sin_t):
    b, s, _ = proj.shape
    l = projc.shape[1]
    hq = RET_HEADS
    v_off = 2 * hq * RET_DK // RET_DV
    g_off = v_off + hq
    cv_off = hq * RET_DK // RET_DV
    smem = pl.BlockSpec(memory_space=pltpu.SMEM)
    return pl.pallas_call(
        _ret_kernel,
        out_shape=jax.ShapeDtypeStruct((b, s, hq * RET_DV), BF16),
        grid=(b, hq),
        in_specs=[
            smem, smem,
            pl.BlockSpec((1, s, RET_DK), lambda bi, h: (bi, 0, h)),
            pl.BlockSpec((1, s, RET_DK), lambda bi, h: (bi, 0, hq + h)),
            pl.BlockSpec((1, s, RET_DV), lambda bi, h: (bi, 0, v_off + h)),
            pl.BlockSpec((1, s, RET_DV), lambda bi, h: (bi, 0, g_off + h)),
            pl.BlockSpec((1, l, RET_DK), lambda bi, h: (bi, 0, h)),
            pl.BlockSpec((1, l, RET_DV), lambda bi, h: (bi, 0, cv_off + h)),
            pl.BlockSpec((s, RET_DK), lambda bi, h: (0, 0)),
            pl.BlockSpec((s, RET_DK), lambda bi, h: (0, 0)),
        ],
        out_specs=pl.BlockSpec((1, s, RET_DV), lambda bi, h: (bi, 0, h)),
        scratch_shapes=[
            pltpu.VMEM((s, RET_DK), BF16),
            pltpu.VMEM((s // RET_CHUNK, RET_DK, RET_DV), BF16),
            pltpu.VMEM((s // RET_CHUNK, RET_DK, RET_DV), BF16),
            pltpu.VMEM((s // RET_CHUNK, RET_DK, RET_DV), F32),
        ],
        compiler_params=_params(("arbitrary", "arbitrary"), 48),
        name="ret",
    )(lgf, lgb, proj, proj, proj, proj, projc, projc, cos_t, sin_t)


def _rope_tables(n):
    rows = n // GRID_W
    inv = ROPE_BASE ** (-jnp.arange(0, ROPE_AXIS_DIM, 2, dtype=F32) / ROPE_AXIS_DIM)
    ar = jnp.arange(rows, dtype=F32)[:, None] * inv
    ac = jnp.arange(GRID_W, dtype=F32)[:, None] * inv
    f = inv.shape[0]

    def per_token(row_part, col_part):
        row_part = jnp.broadcast_to(row_part[:, None, :], (rows, GRID_W, 2 * f))
        col_part = jnp.broadcast_to(col_part[None, :, :], (rows, GRID_W, 2 * f))
        return jnp.concatenate([row_part, col_part], axis=2).reshape(n, 4 * f)

    cos_t = per_token(jnp.concatenate([jnp.cos(ar), jnp.cos(ar)], axis=1),
                      jnp.concatenate([jnp.cos(ac), jnp.cos(ac)], axis=1))
    sin_t = per_token(jnp.concatenate([-jnp.sin(ar), jnp.sin(ar)], axis=1),
                      jnp.concatenate([-jnp.sin(ac), jnp.sin(ac)], axis=1))
    return cos_t, sin_t


def _natt_kernel(q_ref, k_ref, v_ref, kc_ref, vc_ref, pair_ref, o_ref, bias_scr, v1_scr, vc1_scr):
    @pl.when(pl.program_id(1) == 0)
    def _():
        _natt_fill_bias(pair_ref, bias_scr)

    n = q_ref.shape[1]
    rows = n // GRID_W
    n_blk = rows // NA_QROWS
    nq = NA_QROWS * GRID_W
    nk = NA_WROWS * GRID_W
    scale = NA_DH ** -0.5
    kc = kc_ref[0]
    nt = (((1,), (1,)), ((), ()))
    v1_scr[:, :NA_DH] = v_ref[0]
    v1_scr[:, NA_DH:] = jnp.ones((n, NA_DH), BF16)
    vc1_scr[:, :NA_DH] = vc_ref[0]
    vc1_scr[:, NA_DH:] = jnp.ones((vc_ref.shape[1], NA_DH), BF16)

    def block(blk, carry):
        ws = jnp.clip(blk * NA_QROWS - NA_KR // 2, 0, rows - NA_WROWS)
        variant = jnp.where(blk == 0, 0, jnp.where(blk == n_blk - 1, 2, 1))
        qrows = pl.ds(pl.multiple_of(blk * nq, nq), nq)
        krows = pl.ds(pl.multiple_of(ws * GRID_W, GRID_W), nk)
        q = q_ref[0, qrows, :]
        s_loc = lax.dot_general(q, k_ref[0, krows, :], nt, preferred_element_type=F32) + bias_scr[variant]
        s_ctx = lax.dot_general(q, kc, nt, preferred_element_type=F32)
        m = jnp.maximum(jnp.max(s_loc, axis=-1, keepdims=True), jnp.max(s_ctx, axis=-1, keepdims=True))
        p_loc = jnp.exp2((s_loc - m) * (scale * LOG2_E))
        p_ctx = jnp.exp2((s_ctx - m) * (scale * LOG2_E))
        o = jnp.dot(p_loc.astype(BF16), v1_scr[krows, :], preferred_element_type=F32)
        o += jnp.dot(p_ctx.astype(BF16), vc1_scr[...], preferred_element_type=F32)
        o_ref[0, qrows, :] = (o[:, :NA_DH] / o[:, NA_DH:]).astype(o_ref.dtype)
        return carry

    lax.fori_loop(0, n_blk, block, 0, unroll=4)


def _natt_row_offsets():
    rq = np.arange(NA_QROWS)[:, None]
    wr = np.arange(NA_WROWS)[None, :]
    dr0 = np.where(wr < NA_KR, wr - rq + NA_KR - 1, -1)
    dr1 = np.where((wr >= rq) & (wr < rq + NA_KR), wr - rq + NA_KR // 2 - 1, -1)
    dr2 = np.where(wr >= NA_QROWS, wr - rq - 1, -1)
    return np.stack([dr0, dr1, dr2])


def _natt_fill_bias(pair_ref, bias_scr):
    w = GRID_W
    dr = _natt_row_offsets()
    neg = jnp.full((w, 2 * w), NEG_INF, F32)
    lane_row = lax.broadcasted_iota(jnp.int32, (w, NA_WROWS * w), 1) // w
    for kind in range(dr.shape[0]):
        for rq in range(NA_QROWS):
            tiles = []
            for p in range(NA_WROWS // 2):
                a, b_ = int(dr[kind, rq, 2 * p]), int(dr[kind, rq, 2 * p + 1])
                if a < 0 and b_ < 0:
                    tiles.append(neg)
                else:
                    tiles.append(pair_ref[0, b_ if b_ >= 0 else a + 1])
            strip = jnp.concatenate(tiles, axis=1)
            valid = np.nonzero(dr[kind, rq] >= 0)[0]
            keep = (lane_row >= int(valid[0])) & (lane_row <= int(valid[-1]))
            bias_scr[kind, rq * w:(rq + 1) * w, :] = jnp.where(keep, strip, NEG_INF)


def _natt_pair_tiles(rpb):
    w = GRID_W
    qc = np.arange(w)[:, None]
    kcol = np.arange(w)[None, :]
    cs = np.clip(qc - NA_KC // 2, 0, w - NA_KC)
    col_ok = (kcol >= cs) & (kcol < cs + NA_KC)
    dc_idx = np.clip(kcol - qc, -(NA_KC - 1), NA_KC - 1) + NA_KC - 1
    pick = (dc_idx.reshape(1, -1) == np.arange(2 * NA_KC - 1)[:, None]).astype(np.float32)
    nh, ndr, ndc = rpb.shape
    t = jnp.dot(rpb.astype(F32).reshape(nh * ndr, ndc), jnp.asarray(pick), precision=lax.Precision.HIGHEST)
    t = jnp.where(col_ok[None, None], t.reshape(nh, ndr, w, w) * NA_DH ** 0.5, NEG_INF)
    neg = jnp.full((rpb.shape[0], 1, w, w), NEG_INF, F32)
    return jnp.concatenate([jnp.concatenate([neg, t], axis=1), jnp.concatenate([t, neg], axis=1)], axis=3)


def _natt(proj, projc, pairs):
    b, s, _ = proj.shape
    l = projc.shape[1]
    nh = NA_HEADS
    q_off = (2 * RET_HEADS * RET_DK + 2 * RET_HEADS * RET_DV) // NA_DH
    k_off = q_off + nh
    v_off = k_off + nh
    ck_off = (RET_HEADS * RET_DK + RET_HEADS * RET_DV) // NA_DH
    cv_off = ck_off + nh
    nq = NA_QROWS * GRID_W
    nk = NA_WROWS * GRID_W
    return pl.pallas_call(
        _natt_kernel,
        out_shape=jax.ShapeDtypeStruct((b, s, nh * NA_DH), BF16),
        grid=(nh, b),
        in_specs=[
            pl.BlockSpec((1, s, NA_DH), lambda h, bi: (bi, 0, q_off + h)),
            pl.BlockSpec((1, s, NA_DH), lambda h, bi: (bi, 0, k_off + h)),
            pl.BlockSpec((1, s, NA_DH), lambda h, bi: (bi, 0, v_off + h)),
            pl.BlockSpec((1, l, NA_DH), lambda h, bi: (bi, 0, ck_off + h)),
            pl.BlockSpec((1, l, NA_DH), lambda h, bi: (bi, 0, cv_off + h)),
            pl.BlockSpec((1, 2 * NA_KR, GRID_W, 2 * GRID_W), lambda h, bi: (h, 0, 0, 0)),
        ],
        out_specs=pl.BlockSpec((1, s, NA_DH), lambda h, bi: (bi, 0, h)),
        scratch_shapes=[pltpu.VMEM((3, nq, nk), F32), pltpu.VMEM((s, 2 * NA_DH), BF16),
                        pltpu.VMEM((l, 2 * NA_DH), BF16)],
        compiler_params=_params(("arbitrary", "arbitrary"), 48),
        name="natt",
    )(proj, proj, proj, projc, projc, pairs)


def _merge_kernel(a_ref, n_ref, ga_ref, gb_ref, wa_ref, wb_ref, o_ref):
    ra = jnp.dot(a_ref[...], wa_ref[...].astype(BF16), preferred_element_type=F32)
    rn = jnp.dot(n_ref[...], wb_ref[...].astype(BF16), preferred_element_type=F32)
    o_ref[...] = (_sigmoid(ga_ref[...].astype(F32)) * ra + _sigmoid(gb_ref[...].astype(F32)) * rn).astype(o_ref.dtype)


def _merge(ret_in, na_in, proj2d, w_pa, w_pb):
    t, d = ret_in.shape
    tm, tn = MERGE_TM, MERGE_TN
    ga_off = (proj2d.shape[1] - 2 * d) // tn
    gb_off = (proj2d.shape[1] - d) // tn
    return pl.pallas_call(
        _merge_kernel,
        out_shape=jax.ShapeDtypeStruct((t, d), BF16),
        grid=(t // tm, d // tn),
        in_specs=[
            pl.BlockSpec((tm, ret_in.shape[1]), lambda i, j: (i, 0)),
            pl.BlockSpec((tm, na_in.shape[1]), lambda i, j: (i, 0)),
            pl.BlockSpec((tm, tn), lambda i, j: (i, ga_off + j)),
            pl.BlockSpec((tm, tn), lambda i, j: (i, gb_off + j)),
            pl.BlockSpec((w_pa.shape[0], tn), lambda i, j: (0, j)),
            pl.BlockSpec((w_pb.shape[0], tn), lambda i, j: (0, j)),
        ],
        out_specs=pl.BlockSpec((tm, tn), lambda i, j: (i, j)),
        compiler_params=_params(("arbitrary", "arbitrary"), 48),
        name="merge",
    )(ret_in, na_in, proj2d, proj2d, w_pa, w_pb)


def _pack_bf16_pairs(lo, hi):
    lo_bits = lax.bitcast_convert_type(lo.astype(F32), jnp.uint32)
    hi_bits = lax.bitcast_convert_type(hi.astype(F32), jnp.uint32)
    return (lo_bits >> 16) | (hi_bits & jnp.uint32(0xFFFF0000))


def _unpack_bf16_pairs(words):
    lo = lax.bitcast_convert_type(words << 16, F32).astype(BF16)
    hi = lax.bitcast_convert_type(words & jnp.uint32(0xFFFF0000), F32).astype(BF16)
    return lo, hi


def _oproj_kernel(m_ref, x_ref, g1_ref, ng_ref, sh_ref, sc_ref, wo_ref, wrh_ref, wrl_ref, br_ref,
                  x1_ref, h2_ref, lg_ref):
    y = jnp.dot(m_ref[...], wo_ref[...], preferred_element_type=F32)
    x1 = x_ref[0] + g1_ref[0] * y
    x1_ref[0] = x1
    ms = jnp.mean(x1 * x1, axis=-1, keepdims=True)
    h2 = x1 * lax.rsqrt(ms + NORM_EPS) * ng_ref[...]
    h2 = h2 * (1.0 + sc_ref[0]) + sh_ref[0]
    h2_hi = h2.astype(BF16)
    half = h2.shape[1] // 2
    h2_ref[...] = _pack_bf16_pairs(h2_hi[:, :half], h2_hi[:, half:])
    h2_lo = (h2 - h2_hi.astype(F32)).astype(BF16)
    lg = jnp.dot(h2_hi, wrh_ref[...], preferred_element_type=F32)
    lg += jnp.dot(h2_lo, wrh_ref[...], preferred_element_type=F32)
    lg += jnp.dot(h2_hi, wrl_ref[...], preferred_element_type=F32)
    lg_ref[...] = lg + br_ref[...]


def _oproj(m, x, g1, norm_g, sh2, sc2, w_o, w_r, b_r):
    b, s, d = x.shape
    tm = OPROJ_TM
    spt = s // tm
    w_r_hi = w_r.astype(BF16)
    w_r_lo = (w_r - w_r_hi.astype(F32)).astype(BF16)
    return pl.pallas_call(
        _oproj_kernel,
        out_shape=(
            jax.ShapeDtypeStruct((b, s, d), F32),
            jax.ShapeDtypeStruct((b * s, d // 2), jnp.uint32),
            jax.ShapeDtypeStruct((b * s, LANES), F32),
        ),
        grid=(b, spt),
        in_specs=[
            pl.BlockSpec((tm, d), lambda bi, i: (bi * spt + i, 0)),
            pl.BlockSpec((1, tm, d), lambda bi, i: (bi, i, 0)),
            pl.BlockSpec((1, 1, d), lambda bi, i: (bi, 0, 0)),
            pl.BlockSpec((1, d), lambda bi, i: (0, 0)),
            pl.BlockSpec((1, 1, d), lambda bi, i: (bi, 0, 0)),
            pl.BlockSpec((1, 1, d), lambda bi, i: (bi, 0, 0)),
            pl.BlockSpec((d, d), lambda bi, i: (0, 0)),
            pl.BlockSpec((d, LANES), lambda bi, i: (0, 0)),
            pl.BlockSpec((d, LANES), lambda bi, i: (0, 0)),
            pl.BlockSpec((1, LANES), lambda bi, i: (0, 0)),
        ],
        out_specs=(
            pl.BlockSpec((1, tm, d), lambda bi, i: (bi, i, 0)),
            pl.BlockSpec((tm, d // 2), lambda bi, i: (bi * spt + i, 0)),
            pl.BlockSpec((tm, LANES), lambda bi, i: (bi * spt + i, 0)),
        ),
        compiler_params=_params(("arbitrary", "arbitrary"), 48),
        name="oproj",
    )(m, x, g1, norm_g, sh2, sc2, w_o, w_r_hi, w_r_lo, b_r)


def _route_kernel(lg_ref, e_ref, w_ref, r_ref, cnt_ref, run_scr):
    i = pl.program_id(0)
    tm = lg_ref.shape[0]

    @pl.when(i == 0)
    def _():
        run_scr[...] = jnp.zeros_like(run_scr)

    l = lg_ref[...]
    lane = lax.broadcasted_iota(jnp.int32, l.shape, 1)
    vals, idxs, hots = [], [], []
    for _ in range(TOP_K):
        m = jnp.max(l, axis=-1, keepdims=True)
        idx = jnp.min(jnp.where(l == m, lane, LANES), axis=-1, keepdims=True)
        hot = lane == idx
        l = jnp.where(hot, -jnp.inf, l)
        vals.append(m)
        idxs.append(idx)
        hots.append(hot)
    exps = [jnp.exp(v - vals[0]) for v in vals]
    tot = exps[0]
    for e in exps[1:]:
        tot = tot + e

    member = hots[0]
    for hot in hots[1:]:
        member = member | hot
    member = member.astype(F32)
    ri = lax.broadcasted_iota(jnp.int32, (tm, tm), 0)
    ci = lax.broadcasted_iota(jnp.int32, (tm, tm), 1)
    lower = (ci < ri).astype(BF16)
    before = jnp.dot(lower, member.astype(BF16), preferred_element_type=F32) + run_scr[...]

    e_out = jnp.zeros(l.shape, jnp.int32)
    w_out = jnp.zeros(l.shape, F32)
    r_out = jnp.zeros(l.shape, jnp.int32)
    for k in range(TOP_K):
        rank = jnp.sum(jnp.where(hots[k], before, 0.0), axis=-1, keepdims=True).astype(jnp.int32)
        e_out = jnp.where(lane == k, idxs[k], e_out)
        w_out = jnp.where(lane == k, exps[k] / tot, w_out)
        r_out = jnp.where(lane == k, rank, r_out)
    e_ref[...] = e_out[:, :TOP_K]
    w_ref[...] = w_out
    r_ref[...] = r_out[:, :TOP_K]
    run_scr[...] += jnp.sum(member, axis=0, keepdims=True)
    cnt_ref[...] = run_scr[...].astype(jnp.int32)


def _route(logits):
    t = logits.shape[0]
    tm = ROUTE_TM
    row = pl.BlockSpec((tm, LANES), lambda i: (i, 0))
    narrow = pl.BlockSpec((tm, TOP_K), lambda i: (i, 0))
    return pl.pallas_call(
        _route_kernel,
        out_shape=(
            jax.ShapeDtypeStruct((t, TOP_K), jnp.int32),
            jax.ShapeDtypeStruct((t, LANES), F32),
            jax.ShapeDtypeStruct((t, TOP_K), jnp.int32),
            jax.ShapeDtypeStruct((1, LANES), jnp.int32),
        ),
        grid=(t // tm,),
        in_specs=[row],
        out_specs=(narrow, row, narrow, pl.BlockSpec((1, LANES), lambda i: (0, 0))),
        scratch_shapes=[pltpu.VMEM((1, LANES), F32)],
        compiler_params=_params(("arbitrary",), 32),
        name="route",
    )(logits)


def _dispatch_kernel(dest_ref, h_ref, xb_ref, sem):
    tm = h_ref.shape[0]

    for t in range(tm):
        for k in range(TOP_K):
            pltpu.make_async_copy(h_ref.at[pl.ds(t, 1)], xb_ref.at[pl.ds(dest_ref[TOP_K * t + k], 1)],
                                  sem).start(priority=k % 2)
    for k in range(TOP_K):
        pltpu.make_async_copy(h_ref, xb_ref.at[pl.ds(0, tm)], sem).wait()


def _dispatch(h2, dest_flat):
    t, d = h2.shape
    tm = DISPATCH_TM
    return pl.pallas_call(
        _dispatch_kernel,
        out_shape=jax.ShapeDtypeStruct((t * TOP_K, d), h2.dtype),
        grid=(t // tm,),
        in_specs=[
            pl.BlockSpec((tm * TOP_K,), lambda i: (i,), memory_space=pltpu.SMEM),
            pl.BlockSpec((tm, d), lambda i: (i, 0)),
        ],
        out_specs=pl.BlockSpec(memory_space=pl.ANY),
        scratch_shapes=[pltpu.SemaphoreType.DMA],
        compiler_params=_params(("arbitrary",), 32),
        name="dispatch",
    )(dest_flat, h2)


def _expert_kernel(tile_ref, exp_ref, lo_ref, hi_ref, first_ref,
                   x_ref, wg_ref, bg_ref, wu_ref, bu_ref, wd_ref, bd_ref, o_ref, xs_scr):
    w = pl.program_id(0)
    j = pl.program_id(1)
    lo = lo_ref[w]
    hi = hi_ref[w]
    sub = EXPERT_SUB
    half = x_ref.shape[1]

    @pl.when((j == 0) & (first_ref[w] == 1))
    def _():
        x_lo, x_hi = _unpack_bf16_pairs(x_ref[...])
        xs_scr[:, :half] = x_lo
        xs_scr[:, half:] = x_hi
        o_ref[...] = jnp.zeros_like(o_ref)

    def ffn_tile(s):
        rows = slice(s * sub, (s + 1) * sub)
        xs = xs_scr[rows, :]
        gate = jnp.dot(xs, wg_ref[0].astype(BF16), preferred_element_type=F32) + bg_ref[0]
        up = jnp.dot(xs, wu_ref[0].astype(BF16), preferred_element_type=F32) + bu_ref[0]
        gate = jnp.minimum(gate, SWIGLU_LIMIT)
        up = jnp.clip(up, -SWIGLU_LIMIT, SWIGLU_LIMIT)
        act = gate * _sigmoid(SWIGLU_ALPHA * gate) * (up + 1.0)
        row = lax.broadcasted_iota(jnp.int32, (sub, 1), 0) + s * sub
        mine = (row >= lo) & (row < hi)
        act = jnp.where(mine, act, 0.0).astype(BF16)
        y = jnp.dot(act, wd_ref[0].astype(BF16), preferred_element_type=F32)
        o_ref[rows, :] += y + jnp.where(mine & (j == 0), bd_ref[0], 0.0)

    n_sub = x_ref.shape[0] // sub
    active = [(lo < (s + 1) * sub) & (hi > s * sub) for s in range(n_sub)]
    whole = functools.reduce(jnp.logical_and, active)

    @pl.when(whole)
    def _():
        for s in range(n_sub):
            ffn_tile(s)

    partial = jnp.logical_not(whole)
    for p in range(n_sub // 2):
        s0, s1 = 2 * p, 2 * p + 1

        @pl.when(partial & active[s0] & active[s1])
        def _(s0=s0, s1=s1):
            ffn_tile(s0)
            ffn_tile(s1)

        @pl.when(partial & active[s0] & jnp.logical_not(active[s1]))
        def _(s0=s0):
            ffn_tile(s0)

        @pl.when(partial & active[s1] & jnp.logical_not(active[s0]))
        def _(s1=s1):
            ffn_tile(s1)


def _experts(xb, items, w_gate, b_gate, w_up, b_up, w_down, b_down):
    r, half = xb.shape
    e, d, f = w_gate.shape
    tm, tf = EXPERT_TM, EXPERT_TF
    nj = f // tf
    tile, expert, lo, hi, first = items
    n_items = tile.shape[0]

    def jj(j, w, hi_ref, lo_ref):
        return jnp.where(hi_ref[w] > lo_ref[w], j, nj - 1)

    return pl.pallas_call(
        _expert_kernel,
        out_shape=jax.ShapeDtypeStruct((r, d), F32),
        grid_spec=pltpu.PrefetchScalarGridSpec(
            num_scalar_prefetch=5,
            grid=(n_items, nj),
            in_specs=[
                pl.BlockSpec((tm, half), lambda w, j, ti, ex, lo_, hi_, fi: (ti[w], 0)),
                pl.BlockSpec((1, d, tf), lambda w, j, ti, ex, lo_, hi_, fi: (ex[w], 0, jj(j, w, hi_, lo_))),
                pl.BlockSpec((1, 1, tf), lambda w, j, ti, ex, lo_, hi_, fi: (ex[w], 0, jj(j, w, hi_, lo_))),
                pl.BlockSpec((1, d, tf), lambda w, j, ti, ex, lo_, hi_, fi: (ex[w], 0, jj(j, w, hi_, lo_))),
                pl.BlockSpec((1, 1, tf), lambda w, j, ti, ex, lo_, hi_, fi: (ex[w], 0, jj(j, w, hi_, lo_))),
                pl.BlockSpec((1, tf, d), lambda w, j, ti, ex, lo_, hi_, fi: (ex[w], jj(j, w, hi_, lo_), 0)),
                pl.BlockSpec((1, 1, d), lambda w, j, ti, ex, lo_, hi_, fi: (ex[w], 0, 0)),
            ],
            out_specs=pl.BlockSpec((tm, d), lambda w, j, ti, ex, lo_, hi_, fi: (ti[w], 0)),
            scratch_shapes=[pltpu.VMEM((tm, d), BF16)],
        ),
        compiler_params=_params(("arbitrary", "arbitrary"), 60),
        name="experts",
    )(tile, expert, lo, hi, first, xb, w_gate, b_gate.reshape(e, 1, f), w_up, b_up.reshape(e, 1, f),
      w_down, b_down.reshape(e, 1, d))


def _work_items(counts, n_rows):
    tm = EXPERT_TM
    n_tiles = n_rows // tm
    n_items = n_tiles + N_EXPERTS - 1
    cum = jnp.cumsum(counts)
    start = cum - counts
    tile_lo = jnp.arange(n_tiles, dtype=jnp.int32) * tm
    e_lo = jnp.searchsorted(cum, tile_lo, side="right").astype(jnp.int32)
    e_hi = jnp.searchsorted(cum, tile_lo + tm - 1, side="right").astype(jnp.int32)
    per_tile = e_hi - e_lo + 1
    off = jnp.cumsum(per_tile) - per_tile
    total = jnp.sum(per_tile)
    w = jnp.arange(n_items, dtype=jnp.int32)
    valid = w < total
    tile = jnp.clip(jnp.searchsorted(off, w, side="right").astype(jnp.int32) - 1, 0, n_tiles - 1)
    expert = jnp.where(valid, e_lo[tile] + w - off[tile], e_hi[n_tiles - 1])
    tile = jnp.where(valid, tile, n_tiles - 1)
    lo = jnp.clip(start[expert] - tile * tm, 0, tm)
    hi = jnp.clip(cum[expert] - tile * tm, 0, tm)
    hi = jnp.where(valid, jnp.maximum(hi, lo), lo)
    first = (valid & (w == off[tile])).astype(jnp.int32)
    return tile, expert.astype(jnp.int32), lo.astype(jnp.int32), hi.astype(jnp.int32), first, start


def _combine_kernel(dest_ref, dnext_ref, dnext2_ref, yb_ref, w_ref, x1_ref, g2_ref, fg_ref, o_ref, buf, sem):
    i = pl.program_id(0)
    n = pl.num_programs(0)
    tm = x1_ref.shape[0]
    slot = i % COMBINE_SLOTS

    def gather(d_ref, into):
        for t in range(tm):
            for k in range(TOP_K):
                pltpu.make_async_copy(yb_ref.at[pl.ds(d_ref[TOP_K * t + k], 1)], buf.at[into, k, pl.ds(t, 1)],
                                      sem.at[into]).start(priority=k % 2)

    def reduce(src):
        wts = w_ref[...]
        moe = buf[src, 0] * wts[:, 0:1]
        for k in range(1, TOP_K):
            moe += buf[src, k] * wts[:, k:k + 1]
        x2 = x1_ref[...] + g2_ref[0] * moe
        ms = jnp.mean(x2 * x2, axis=-1, keepdims=True)
        o_ref[...] = x2 * lax.rsqrt(ms + NORM_EPS) * fg_ref[...]

    @pl.when(i == 0)
    def _():
        gather(dest_ref, 0)
        gather(dnext_ref, 1)

    for k in range(TOP_K):
        pltpu.make_async_copy(yb_ref.at[pl.ds(0, tm)], buf.at[slot, k], sem.at[slot]).wait()

    for v in range(COMBINE_SLOTS):
        @pl.when((slot == v) & (i + 2 < n))
        def _(v=v):
            gather(dnext2_ref, (v + 2) % COMBINE_SLOTS)
            reduce(v)

    @pl.when(i + 2 >= n)
    def _():
        reduce(slot)


def _combine(yb, dest_flat, top_w, x1, g2, final_g):
    b, s, d = x1.shape
    tm = COMBINE_TM
    spt = s // tm
    n = b * spt
    assert n >= COMBINE_SLOTS
    out = pl.pallas_call(
        _combine_kernel,
        out_shape=jax.ShapeDtypeStruct((b * s, d), F32),
        grid=(n,),
        in_specs=[
            pl.BlockSpec((tm * TOP_K,), lambda i: (i,), memory_space=pltpu.SMEM),
            pl.BlockSpec((tm * TOP_K,), lambda i: (jnp.minimum(i + 1, n - 1),), memory_space=pltpu.SMEM),
            pl.BlockSpec((tm * TOP_K,), lambda i: (jnp.minimum(i + 2, n - 1),), memory_space=pltpu.SMEM),
            pl.BlockSpec(memory_space=pl.ANY),
            pl.BlockSpec((tm, LANES), lambda i: (i, 0)),
            pl.BlockSpec((tm, d), lambda i: (i, 0)),
            pl.BlockSpec((1, 1, d), lambda i: (i // spt, 0, 0)),
            pl.BlockSpec((1, d), lambda i: (0, 0)),
        ],
        out_specs=pl.BlockSpec((tm, d), lambda i: (i, 0)),
        scratch_shapes=[pltpu.VMEM((COMBINE_SLOTS, TOP_K, tm, d), F32), pltpu.SemaphoreType.DMA((COMBINE_SLOTS,))],
        compiler_params=_params(("arbitrary",), 40),
        name="combine",
    )(dest_flat, dest_flat, dest_flat, yb, top_w, x1.reshape(b * s, d), g2, final_g)
    return out.reshape(b, s, d)


def kernel(x, c, ctx, c_ctx, ada_w, ada_b, norm1_g, norm2_g, w_in, w_pa, w_pb, w_o, ret_decay_fwd, ret_decay_bwd,
           na_rpb, w_router, b_router, w_gate, b_gate, w_up, b_up, w_down, b_down, final_g):
    assert ada_w.shape[0] == 1, "single layer"
    b, s, d = x.shape
    l = ctx.shape[1]
    in_w = w_in.shape[2]
    rows = s // GRID_W
    assert s % IN_TM == 0 and l % RET_CHUNK == 0 and rows >= NA_WROWS and rows % NA_QROWS == 0

    c_rows = jnp.zeros((16, d), F32).at[:b].set(c).at[b].set(c_ctx)
    mod = _ada(c_rows, ada_w[0], ada_b[0][None, :])
    sh1, sc1, g1, sh2, sc2, g2 = [mod[:b, None, i * d:(i + 1) * d] for i in range(6)]
    shc1 = mod[b, 0 * d:1 * d][None, None, :]
    scc1 = mod[b, 1 * d:2 * d][None, None, :]

    w_in0 = w_in[0]
    n1 = norm1_g[0][None, :]
    proj = _inproj(x, n1, sh1, sc1, w_in0, tuple(range(in_w // IN_TN)), IN_TM)
    qk_w = RET_HEADS * RET_DK
    v_w = RET_HEADS * RET_DV
    na_w = NA_HEADS * NA_DH
    ctx_cols = tuple(range(qk_w // IN_TN, (2 * qk_w + v_w) // IN_TN)) + tuple(
        range((2 * qk_w + 2 * v_w + na_w) // IN_TN, (2 * qk_w + 2 * v_w + 3 * na_w) // IN_TN))
    projc = _inproj(ctx.reshape(1, b * l, d), n1, shc1, scc1, w_in0, ctx_cols, b * l).reshape(b, l, -1)

    lgf = jax.nn.log_sigmoid(ret_decay_fwd[0].astype(F32))
    lgb = jax.nn.log_sigmoid(ret_decay_bwd[0].astype(F32))
    cos_t, sin_t = _rope_tables(s)
    ret_in = _retention(proj, projc, lgf, lgb, cos_t, sin_t)
    na_in = _natt(proj, projc, _natt_pair_tiles(na_rpb[0]))

    t = b * s
    m = _merge(ret_in.reshape(t, -1), na_in.reshape(t, -1), proj.reshape(t, in_w),
               w_pa[0], w_pb[0])

    w_r = jnp.zeros((d, LANES), F32).at[:, :N_EXPERTS].set(w_router[0])
    b_r = jnp.full((1, LANES), NEG_INF, F32).at[0, :N_EXPERTS].set(b_router[0])
    x1, h2, logits = _oproj(m, x, g1, norm2_g[0][None, :], sh2, sc2, w_o[0].astype(BF16), w_r, b_r)

    top_e, top_w, rank, counts = _route(logits)
    counts = counts[0, :N_EXPERTS]
    items = _work_items(counts, t * TOP_K)
    start = items[5]
    is_e = top_e[:, :, None] == jnp.arange(N_EXPERTS, dtype=jnp.int32)
    dest = (jnp.sum(jnp.where(is_e, start.astype(jnp.int32), 0), axis=-1) + rank).reshape(-1)

    xb = _dispatch(h2, dest)
    yb = _experts(xb, items[:5], w_gate[0], b_gate[0], w_up[0], b_up[0], w_down[0], b_down[0])
    return _combine(yb, dest, top_w, x1, g2, final_g[None, :])
```

```python
import functools

import jax
import jax.numpy as jnp
import numpy as np
from jax import lax
from jax.experimental import pallas as pl
from jax.experimental.pallas import tpu as pltpu

F32 = jnp.float32
BF16 = jnp.bfloat16

GRID_W = 64
RET_HEADS = 8
RET_DK = 128
RET_DV = 256
RET_CHUNK = 128
ROPE_AXIS_DIM = RET_DK // 2
ROPE_BASE = 10000.0
NA_HEADS = 16
NA_DH = 128
NA_KR = 8
NA_KC = 16
N_EXPERTS = 32
TOP_K = 4
SWIGLU_ALPHA = 1.702
SWIGLU_LIMIT = 7.0
NORM_EPS = 1e-6
GN_EPS = 1e-5
NEG_INF = -1e30
LOG2_E = 1.4426950408889634

LANES = 128
MIB = 1024 * 1024

NA_QROWS = 4
NA_WROWS = NA_QROWS + NA_KR

IN_TM, IN_TN = 1024, 1024
MERGE_TM, MERGE_TN = 1024, 512
OPROJ_TM = 512
ROUTE_TM = 512
DISPATCH_TM = 256
EXPERT_TM, EXPERT_SUB, EXPERT_TF = 1024, 256, 512
COMBINE_TM = 128
COMBINE_SLOTS = 3


def _params(semantics, vmem_mib):
    return pltpu.CompilerParams(dimension_semantics=semantics, vmem_limit_bytes=vmem_mib * MIB)


def _sigmoid(x):
    return 1.0 / (1.0 + jnp.exp(-x))


def _ada_kernel(c_ref, w_ref, b_ref, o_ref):
    c = c_ref[...]
    s = c * _sigmoid(c)
    o_ref[...] = jnp.dot(s.astype(BF16), w_ref[...].astype(BF16), preferred_element_type=F32) + b_ref[...]


def _ada(c_rows, w, b):
    r, d = c_rows.shape
    n = w.shape[1]
    tn = 1024
    return pl.pallas_call(
        _ada_kernel,
        out_shape=jax.ShapeDtypeStruct((r, n), F32),
        grid=(n // tn,),
        in_specs=[
            pl.BlockSpec((r, d), lambda j: (0, 0)),
            pl.BlockSpec((d, tn), lambda j: (0, j)),
            pl.BlockSpec((1, tn), lambda j: (0, j)),
        ],
        out_specs=pl.BlockSpec((r, tn), lambda j: (0, j)),
        compiler_params=_params(("arbitrary",), 40),
        name="ada",
    )(c_rows, w, b)


def _inproj_kernel(cols_ref, x_ref, g_ref, sh_ref, sc_ref, w_ref, o_ref, h_scr):
    @pl.when(pl.program_id(2) == 0)
    def _():
        x = x_ref[0]
        ms = jnp.mean(x * x, axis=-1, keepdims=True)
        y = x * lax.rsqrt(ms + NORM_EPS) * g_ref[...]
        h_scr[...] = (y * (1.0 + sc_ref[0]) + sh_ref[0]).astype(BF16)

    o_ref[0] = jnp.dot(h_scr[...], w_ref[...].astype(BF16), preferred_element_type=F32).astype(o_ref.dtype)


def _inproj(x, g, shift, scale, w, col_tiles, tm):
    b, n, d = x.shape
    tn = IN_TN
    nct = len(col_tiles)
    cols = jnp.asarray(np.asarray(col_tiles, np.int32))
    return pl.pallas_call(
        _inproj_kernel,
        out_shape=jax.ShapeDtypeStruct((b, n, nct * tn), BF16),
        grid_spec=pltpu.PrefetchScalarGridSpec(
            num_scalar_prefetch=1,
            grid=(b, n // tm, nct),
            in_specs=[
                pl.BlockSpec((1, tm, d), lambda bi, i, j, c: (bi, i, 0)),
                pl.BlockSpec((1, d), lambda bi, i, j, c: (0, 0)),
                pl.BlockSpec((1, 1, d), lambda bi, i, j, c: (bi, 0, 0)),
                pl.BlockSpec((1, 1, d), lambda bi, i, j, c: (bi, 0, 0)),
                pl.BlockSpec((d, tn), lambda bi, i, j, c: (0, c[j])),
            ],
            out_specs=pl.BlockSpec((1, tm, tn), lambda bi, i, j, c: (bi, i, j)),
            scratch_shapes=[pltpu.VMEM((tm, d), BF16)],
        ),
        compiler_params=_params(("arbitrary", "arbitrary", "arbitrary"), 56),
        name="inproj",
    )(cols, x, g, shift, scale, w)


def _swap_halves(x):
    lane = lax.broadcasted_iota(jnp.int32, x.shape, 1)
    return jnp.where(lane % 64 < 32, pltpu.roll(x, 96, 1), pltpu.roll(x, 32, 1))


def _ret_kernel(lgf_ref, lgb_ref, q_ref, k_ref, v_ref, g_ref, kc_ref, vc_ref, cos_ref, sin_ref,
                o_ref, ks_scr, rf_scr, rb_scr, ub_scr):
    h = pl.program_id(1)
    c = RET_CHUNK
    n = q_ref.shape[1]
    nc = n // c
    ncc = kc_ref.shape[1] // c
    lgf = lgf_ref[h]
    lgb = lgb_ref[h]
    k_scale = RET_DK ** -0.5

    pos_c = lax.broadcasted_iota(jnp.int32, (c, 1), 0).astype(F32)
    zeta_f = jnp.exp((c - 1.0 - pos_c) * lgf)
    zeta_b = jnp.exp(pos_c * lgb)
    xi_f = jnp.exp((pos_c + 1.0) * lgf)
    xi_b = jnp.exp((c - pos_c) * lgb)
    one = jnp.ones((1, 1), F32)
    gc_f = jnp.exp(one * (c * lgf))
    gc_b = jnp.exp(one * (c * lgb))
    ii = lax.broadcasted_iota(jnp.int32, (c, c), 0)
    jj = lax.broadcasted_iota(jnp.int32, (c, c), 1)
    diff = (ii - jj).astype(F32)
    dmask = (jnp.where(diff >= 0, jnp.exp(jnp.maximum(diff, 0.0) * lgf), 0.0)
             + jnp.where(diff <= 0, jnp.exp(jnp.maximum(-diff, 0.0) * lgb), 0.0))

    def ktv(k_bf, v_f32, zeta):
        return jnp.dot(k_bf.astype(F32).T.astype(BF16), (v_f32 * zeta).astype(BF16), preferred_element_type=F32)

    r_f = jnp.zeros((RET_DK, RET_DV), F32)
    for i in range(ncc):
        kc = (kc_ref[0, i * c:(i + 1) * c, :].astype(F32) * k_scale).astype(BF16)
        r_f = gc_f * r_f + ktv(kc, vc_ref[0, i * c:(i + 1) * c, :].astype(F32), zeta_f)
    r_b = jnp.zeros((RET_DK, RET_DV), F32)
    for i in reversed(range(ncc)):
        kc = (kc_ref[0, i * c:(i + 1) * c, :].astype(F32) * k_scale).astype(BF16)
        r_b = gc_b * r_b + ktv(kc, vc_ref[0, i * c:(i + 1) * c, :].astype(F32), zeta_b)

    def rope(x, rows):
        x = x.astype(F32)
        return x * cos_ref[rows, :] + _swap_halves(x) * sin_ref[rows, :]

    def chunk_updates(i, r):
        rows = pl.ds(pl.multiple_of(i * c, c), c)
        kb = (rope(k_ref[0, rows, :], rows) * k_scale).astype(BF16)
        ks_scr[rows, :] = kb
        kt = kb.astype(F32).T.astype(BF16)
        v = v_ref[0, rows, :].astype(F32)
        rf_scr[i] = r.astype(BF16)
        ub_scr[i] = jnp.dot(kt, (v * zeta_b).astype(BF16), preferred_element_type=F32)
        return gc_f * r + jnp.dot(kt, (v * zeta_f).astype(BF16), preferred_element_type=F32)

    lax.fori_loop(0, nc, chunk_updates, r_f, unroll=8)

    def bwd_scan(t, r):
        i = nc - 1 - t
        rb_scr[i] = r.astype(BF16)
        return gc_b * r + ub_scr[i]

    lax.fori_loop(0, nc, bwd_scan, r_b, unroll=2)

    def out_chunk(i, carry):
        rows = pl.ds(pl.multiple_of(i * c, c), c)
        q = rope(q_ref[0, rows, :], rows)
        kb = ks_scr[rows, :]
        vb = v_ref[0, rows, :]
        s = lax.dot_general(q.astype(BF16), kb, (((1,), (1,)), ((), ())), preferred_element_type=F32) * dmask
        o = jnp.dot(s.astype(BF16), vb, preferred_element_type=F32)
        o += jnp.dot((q * xi_f).astype(BF16), rf_scr[i], preferred_element_type=F32)
        o += jnp.dot((q * xi_b).astype(BF16), rb_scr[i], preferred_element_type=F32)
        mu = jnp.mean(o, axis=-1, keepdims=True)
        d = o - mu
        var = jnp.mean(d * d, axis=-1, keepdims=True)
        on = d * lax.rsqrt(var + GN_EPS)
        g = g_ref[0, rows, :].astype(F32)
        o_ref[0, rows, :] = (g * _sigmoid(g) * on).astype(o_ref.dtype)
        return carry

    lax.fori_loop(0, nc, out_chunk, 0, unroll=8)


def _retention(proj, projc, lgf, lgb, cos_t, sin_t):
    b, s, _ = proj.shape
    l = projc.shape[1]
    hq = RET_HEADS
    v_off = 2 * hq * RET_DK // RET_DV
    g_off = v_off + hq
    cv_off = hq * RET_DK // RET_DV
    smem = pl.BlockSpec(memory_space=pltpu.SMEM)
    return pl.pallas_call(
        _ret_kernel,
        out_shape=jax.ShapeDtypeStruct((b, s, hq * RET_DV), BF16),
        grid=(b, hq),
        in_specs=[
            smem, smem,
            pl.BlockSpec((1, s, RET_DK), lambda bi, h: (bi, 0, h)),
            pl.BlockSpec((1, s, RET_DK), lambda bi, h: (bi, 0, hq + h)),
            pl.BlockSpec((1, s, RET_DV), lambda bi, h: (bi, 0, v_off + h)),
            pl.BlockSpec((1, s, RET_DV), lambda bi, h: (bi, 0, g_off + h)),
            pl.BlockSpec((1, l, RET_DK), lambda bi, h: (bi, 0, h)),
            pl.BlockSpec((1, l, RET_DV), lambda bi, h: (bi, 0, cv_off + h)),
            pl.BlockSpec((s, RET_DK), lambda bi, h: (0, 0)),
            pl.BlockSpec((s, RET_DK), lambda bi, h: (0, 0)),
        ],
        out_specs=pl.BlockSpec((1, s, RET_DV), lambda bi, h: (bi, 0, h)),
        scratch_shapes=[
            pltpu.VMEM((s, RET_DK), BF16),
            pltpu.VMEM((s // RET_CHUNK, RET_DK, RET_DV), BF16),
            pltpu.VMEM((s // RET_CHUNK, RET_DK, RET_DV), BF16),
            pltpu.VMEM((s // RET_CHUNK, RET_DK, RET_DV), F32),
        ],
        compiler_params=_params(("arbitrary", "arbitrary"), 48),
        name="ret",
    )(lgf, lgb, proj, proj, proj, proj, projc, projc, cos_t, sin_t)


def _rope_tables(n):
    rows = n // GRID_W
    inv = ROPE_BASE ** (-jnp.arange(0, ROPE_AXIS_DIM, 2, dtype=F32) / ROPE_AXIS_DIM)
    ar = jnp.arange(rows, dtype=F32)[:, None] * inv
    ac = jnp.arange(GRID_W, dtype=F32)[:, None] * inv
    f = inv.shape[0]

    def per_token(row_part, col_part):
        row_part = jnp.broadcast_to(row_part[:, None, :], (rows, GRID_W, 2 * f))
        col_part = jnp.broadcast_to(col_part[None, :, :], (rows, GRID_W, 2 * f))
        return jnp.concatenate([row_part, col_part], axis=2).reshape(n, 4 * f)

    cos_t = per_token(jnp.concatenate([jnp.cos(ar), jnp.cos(ar)], axis=1),
                      jnp.concatenate([jnp.cos(ac), jnp.cos(ac)], axis=1))
    sin_t = per_token(jnp.concatenate([-jnp.sin(ar), jnp.sin(ar)], axis=1),
                      jnp.concatenate([-jnp.sin(ac), jnp.sin(ac)], axis=1))
    return cos_t, sin_t


def _natt_kernel(q_ref, k_ref, v_ref, kc_ref, vc_ref, pair_ref, o_ref, bias_scr, v1_scr, vc1_scr):
    @pl.when(pl.program_id(1) == 0)
    def _():
        _natt_fill_bias(pair_ref, bias_scr)

    n = q_ref.shape[1]
    rows = n // GRID_W
    n_blk = rows // NA_QROWS
    nq = NA_QROWS * GRID_W
    nk = NA_WROWS * GRID_W
    scale = NA_DH ** -0.5
    kc = kc_ref[0]
    nt = (((1,), (1,)), ((), ()))
    v1_scr[:, :NA_DH] = v_ref[0]
    v1_scr[:, NA_DH:] = jnp.ones((n, NA_DH), BF16)
    vc1_scr[:, :NA_DH] = vc_ref[0]
    vc1_scr[:, NA_DH:] = jnp.ones((vc_ref.shape[1], NA_DH), BF16)

    def block(blk, carry):
        ws = jnp.clip(blk * NA_QROWS - NA_KR // 2, 0, rows - NA_WROWS)
        variant = jnp.where(blk == 0, 0, jnp.where(blk == n_blk - 1, 2, 1))
        qrows = pl.ds(pl.multiple_of(blk * nq, nq), nq)
        krows = pl.ds(pl.multiple_of(ws * GRID_W, GRID_W), nk)
        q = q_ref[0, qrows, :]
        s_loc = lax.dot_general(q, k_ref[0, krows, :], nt, preferred_element_type=F32) + bias_scr[variant]
        s_ctx = lax.dot_general(q, kc, nt, preferred_element_type=F32)
        m = jnp.maximum(jnp.max(s_loc, axis=-1, keepdims=True), jnp.max(s_ctx, axis=-1, keepdims=True))
        p_loc = jnp.exp2((s_loc - m) * (scale * LOG2_E))
        p_ctx = jnp.exp2((s_ctx - m) * (scale * LOG2_E))
        o = jnp.dot(p_loc.astype(BF16), v1_scr[krows, :], preferred_element_type=F32)
        o += jnp.dot(p_ctx.astype(BF16), vc1_scr[...], preferred_element_type=F32)
        o_ref[0, qrows, :] = (o[:, :NA_DH] / o[:, NA_DH:]).astype(o_ref.dtype)
        return carry

    lax.fori_loop(0, n_blk, block, 0, unroll=4)


def _natt_row_offsets():
    rq = np.arange(NA_QROWS)[:, None]
    wr = np.arange(NA_WROWS)[None, :]
    dr0 = np.where(wr < NA_KR, wr - rq + NA_KR - 1, -1)
    dr1 = np.where((wr >= rq) & (wr < rq + NA_KR), wr - rq + NA_KR // 2 - 1, -1)
    dr2 = np.where(wr >= NA_QROWS, wr - rq - 1, -1)
    return np.stack([dr0, dr1, dr2])


def _natt_fill_bias(pair_ref, bias_scr):
    w = GRID_W
    dr = _natt_row_offsets()
    neg = jnp.full((w, 2 * w), NEG_INF, F32)
    lane_row = lax.broadcasted_iota(jnp.int32, (w, NA_WROWS * w), 1) // w
    for kind in range(dr.shape[0]):
        for rq in range(NA_QROWS):
            tiles = []
            for p in range(NA_WROWS // 2):
                a, b_ = int(dr[kind, rq, 2 * p]), int(dr[kind, rq, 2 * p + 1])
                if a < 0 and b_ < 0:
                    tiles.append(neg)
                else:
                    tiles.append(pair_ref[0, b_ if b_ >= 0 else a + 1])
            strip = jnp.concatenate(tiles, axis=1)
            valid = np.nonzero(dr[kind, rq] >= 0)[0]
            keep = (lane_row >= int(valid[0])) & (lane_row <= int(valid[-1]))
            bias_scr[kind, rq * w:(rq + 1) * w, :] = jnp.where(keep, strip, NEG_INF)


def _natt_pair_tiles(rpb):
    w = GRID_W
    qc = np.arange(w)[:, None]
    kcol = np.arange(w)[None, :]
    cs = np.clip(qc - NA_KC // 2, 0, w - NA_KC)
    col_ok = (kcol >= cs) & (kcol < cs + NA_KC)
    dc_idx = np.clip(kcol - qc, -(NA_KC - 1), NA_KC - 1) + NA_KC - 1
    pick = (dc_idx.reshape(1, -1) == np.arange(2 * NA_KC - 1)[:, None]).astype(np.float32)
    nh, ndr, ndc = rpb.shape
    t = jnp.dot(rpb.astype(F32).reshape(nh * ndr, ndc), jnp.asarray(pick), precision=lax.Precision.HIGHEST)
    t = jnp.where(col_ok[None, None], t.reshape(nh, ndr, w, w) * NA_DH ** 0.5, NEG_INF)
    neg = jnp.full((rpb.shape[0], 1, w, w), NEG_INF, F32)
    return jnp.concatenate([jnp.concatenate([neg, t], axis=1), jnp.concatenate([t, neg], axis=1)], axis=3)


def _natt(proj, projc, pairs):
    b, s, _ = proj.shape
    l = projc.shape[1]
    nh = NA_HEADS
    q_off = (2 * RET_HEADS * RET_DK + 2 * RET_HEADS * RET_DV) // NA_DH
    k_off = q_off + nh
    v_off = k_off + nh
    ck_off = (RET_HEADS * RET_DK + RET_HEADS * RET_DV) // NA_DH
    cv_off = ck_off + nh
    nq = NA_QROWS * GRID_W
    nk = NA_WROWS * GRID_W
    return pl.pallas_call(
        _natt_kernel,
        out_shape=jax.ShapeDtypeStruct((b, s, nh * NA_DH), BF16),
        grid=(nh, b),
        in_specs=[
            pl.BlockSpec((1, s, NA_DH), lambda h, bi: (bi, 0, q_off + h)),
            pl.BlockSpec((1, s, NA_DH), lambda h, bi: (bi, 0, k_off + h)),
            pl.BlockSpec((1, s, NA_DH), lambda h, bi: (bi, 0, v_off + h)),
            pl.BlockSpec((1, l, NA_DH), lambda h, bi: (bi, 0, ck_off + h)),
            pl.BlockSpec((1, l, NA_DH), lambda h, bi: (bi, 0, cv_off + h)),
            pl.BlockSpec((1, 2 * NA_KR, GRID_W, 2 * GRID_W), lambda h, bi: (h, 0, 0, 0)),
        ],
        out_specs=pl.BlockSpec((1, s, NA_DH), lambda h, bi: (bi, 0, h)),
        scratch_shapes=[pltpu.VMEM((3, nq, nk), F32), pltpu.VMEM((s, 2 * NA_DH), BF16),
                        pltpu.VMEM((l, 2 * NA_DH), BF16)],
        compiler_params=_params(("arbitrary", "arbitrary"), 48),
        name="natt",
    )(proj, proj, proj, projc, projc, pairs)


def _merge_kernel(a_ref, n_ref, ga_ref, gb_ref, wa_ref, wb_ref, o_ref):
    ra = jnp.dot(a_ref[...], wa_ref[...].astype(BF16), preferred_element_type=F32)
    rn = jnp.dot(n_ref[...], wb_ref[...].astype(BF16), preferred_element_type=F32)
    o_ref[...] = (_sigmoid(ga_ref[...].astype(F32)) * ra + _sigmoid(gb_ref[...].astype(F32)) * rn).astype(o_ref.dtype)


def _merge(ret_in, na_in, proj2d, w_pa, w_pb):
    t, d = ret_in.shape
    tm, tn = MERGE_TM, MERGE_TN
    ga_off = (proj2d.shape[1] - 2 * d) // tn
    gb_off = (proj2d.shape[1] - d) // tn
    return pl.pallas_call(
        _merge_kernel,
        out_shape=jax.ShapeDtypeStruct((t, d), BF16),
        grid=(t // tm, d // tn),
        in_specs=[
            pl.BlockSpec((tm, ret_in.shape[1]), lambda i, j: (i, 0)),
            pl.BlockSpec((tm, na_in.shape[1]), lambda i, j: (i, 0)),
            pl.BlockSpec((tm, tn), lambda i, j: (i, ga_off + j)),
            pl.BlockSpec((tm, tn), lambda i, j: (i, gb_off + j)),
            pl.BlockSpec((w_pa.shape[0], tn), lambda i, j: (0, j)),
            pl.BlockSpec((w_pb.shape[0], tn), lambda i, j: (0, j)),
        ],
        out_specs=pl.BlockSpec((tm, tn), lambda i, j: (i, j)),
        compiler_params=_params(("arbitrary", "arbitrary"), 48),
        name="merge",
    )(ret_in, na_in, proj2d, proj2d, w_pa, w_pb)


def _pack_bf16_pairs(lo, hi):
    lo_bits = lax.bitcast_convert_type(lo.astype(F32), jnp.uint32)
    hi_bits = lax.bitcast_convert_type(hi.astype(F32), jnp.uint32)
    return (lo_bits >> 16) | (hi_bits & jnp.uint32(0xFFFF0000))


def _unpack_bf16_pairs(words):
    lo = lax.bitcast_convert_type(words << 16, F32).astype(BF16)
    hi = lax.bitcast_convert_type(words & jnp.uint32(0xFFFF0000), F32).astype(BF16)
    return lo, hi


def _oproj_kernel(m_ref, x_ref, g1_ref, ng_ref, sh_ref, sc_ref, wo_ref, wrh_ref, wrl_ref, br_ref,
                  x1_ref, h2_ref, lg_ref):
    y = jnp.dot(m_ref[...], wo_ref[...], preferred_element_type=F32)
    x1 = x_ref[0] + g1_ref[0] * y
    x1_ref[0] = x1
    ms = jnp.mean(x1 * x1, axis=-1, keepdims=True)
    h2 = x1 * lax.rsqrt(ms + NORM_EPS) * ng_ref[...]
    h2 = h2 * (1.0 + sc_ref[0]) + sh_ref[0]
    h2_hi = h2.astype(BF16)
    half = h2.shape[1] // 2
    h2_ref[...] = _pack_bf16_pairs(h2_hi[:, :half], h2_hi[:, half:])
    h2_lo = (h2 - h2_hi.astype(F32)).astype(BF16)
    lg = jnp.dot(h2_hi, wrh_ref[...], preferred_element_type=F32)
    lg += jnp.dot(h2_lo, wrh_ref[...], preferred_element_type=F32)
    lg += jnp.dot(h2_hi, wrl_ref[...], preferred_element_type=F32)
    lg_ref[...] = lg + br_ref[...]


def _oproj(m, x, g1, norm_g, sh2, sc2, w_o, w_r, b_r):
    b, s, d = x.shape
    tm = OPROJ_TM
    spt = s // tm
    w_r_hi = w_r.astype(BF16)
    w_r_lo = (w_r - w_r_hi.astype(F32)).astype(BF16)
    return pl.pallas_call(
        _oproj_kernel,
        out_shape=(
            jax.ShapeDtypeStruct((b, s, d), F32),
            jax.ShapeDtypeStruct((b * s, d // 2), jnp.uint32),
            jax.ShapeDtypeStruct((b * s, LANES), F32),
        ),
        grid=(b, spt),
        in_specs=[
            pl.BlockSpec((tm, d), lambda bi, i: (bi * spt + i, 0)),
            pl.BlockSpec((1, tm, d), lambda bi, i: (bi, i, 0)),
            pl.BlockSpec((1, 1, d), lambda bi, i: (bi, 0, 0)),
            pl.BlockSpec((1, d), lambda bi, i: (0, 0)),
            pl.BlockSpec((1, 1, d), lambda bi, i: (bi, 0, 0)),
            pl.BlockSpec((1, 1, d), lambda bi, i: (bi, 0, 0)),
            pl.BlockSpec((d, d), lambda bi, i: (0, 0)),
            pl.BlockSpec((d, LANES), lambda bi, i: (0, 0)),
            pl.BlockSpec((d, LANES), lambda bi, i: (0, 0)),
            pl.BlockSpec((1, LANES), lambda bi, i: (0, 0)),
        ],
        out_specs=(
            pl.BlockSpec((1, tm, d), lambda bi, i: (bi, i, 0)),
            pl.BlockSpec((tm, d // 2), lambda bi, i: (bi * spt + i, 0)),
            pl.BlockSpec((tm, LANES), lambda bi, i: (bi * spt + i, 0)),
        ),
        compiler_params=_params(("arbitrary", "arbitrary"), 48),
        name="oproj",
    )(m, x, g1, norm_g, sh2, sc2, w_o, w_r_hi, w_r_lo, b_r)


def _route_kernel(lg_ref, e_ref, w_ref, r_ref, cnt_ref, run_scr):
    i = pl.program_id(0)
    tm = lg_ref.shape[0]

    @pl.when(i == 0)
    def _():
        run_scr[...] = jnp.zeros_like(run_scr)

    l = lg_ref[...]
    lane = lax.broadcasted_iota(jnp.int32, l.shape, 1)
    vals, idxs, hots = [], [], []
    for _ in range(TOP_K):
        m = jnp.max(l, axis=-1, keepdims=True)
        idx = jnp.min(jnp.where(l == m, lane, LANES), axis=-1, keepdims=True)
        hot = lane == idx
        l = jnp.where(hot, -jnp.inf, l)
        vals.append(m)
        idxs.append(idx)
        hots.append(hot)
    exps = [jnp.exp(v - vals[0]) for v in vals]
    tot = exps[0]
    for e in exps[1:]:
        tot = tot + e

    member = hots[0]
    for hot in hots[1:]:
        member = member | hot
    member = member.astype(F32)
    ri = lax.broadcasted_iota(jnp.int32, (tm, tm), 0)
    ci = lax.broadcasted_iota(jnp.int32, (tm, tm), 1)
    lower = (ci < ri).astype(BF16)
    before = jnp.dot(lower, member.astype(BF16), preferred_element_type=F32) + run_scr[...]

    e_out = jnp.zeros(l.shape, jnp.int32)
    w_out = jnp.zeros(l.shape, F32)
    r_out = jnp.zeros(l.shape, jnp.int32)
    for k in range(TOP_K):
        rank = jnp.sum(jnp.where(hots[k], before, 0.0), axis=-1, keepdims=True).astype(jnp.int32)
        e_out = jnp.where(lane == k, idxs[k], e_out)
        w_out = jnp.where(lane == k, exps[k] / tot, w_out)
        r_out = jnp.where(lane == k, rank, r_out)
    e_ref[...] = e_out[:, :TOP_K]
    w_ref[...] = w_out
    r_ref[...] = r_out[:, :TOP_K]
    run_scr[...] += jnp.sum(member, axis=0, keepdims=True)
    cnt_ref[...] = run_scr[...].astype(jnp.int32)


def _route(logits):
    t = logits.shape[0]
    tm = ROUTE_TM
    row = pl.BlockSpec((tm, LANES), lambda i: (i, 0))
    narrow = pl.BlockSpec((tm, TOP_K), lambda i: (i, 0))
    return pl.pallas_call(
        _route_kernel,
        out_shape=(
            jax.ShapeDtypeStruct((t, TOP_K), jnp.int32),
            jax.ShapeDtypeStruct((t, LANES), F32),
            jax.ShapeDtypeStruct((t, TOP_K), jnp.int32),
            jax.ShapeDtypeStruct((1, LANES), jnp.int32),
        ),
        grid=(t // tm,),
        in_specs=[row],
        out_specs=(narrow, row, narrow, pl.BlockSpec((1, LANES), lambda i: (0, 0))),
        scratch_shapes=[pltpu.VMEM((1, LANES), F32)],
        compiler_params=_params(("arbitrary",), 32),
        name="route",
    )(logits)


def _dispatch_kernel(dest_ref, h_ref, xb_ref, sem):
    tm = h_ref.shape[0]

    for t in range(tm):
        for k in range(TOP_K):
            pltpu.make_async_copy(h_ref.at[pl.ds(t, 1)], xb_ref.at[pl.ds(dest_ref[TOP_K * t + k], 1)],
                                  sem).start(priority=k % 2)
    for k in range(TOP_K):
        pltpu.make_async_copy(h_ref, xb_ref.at[pl.ds(0, tm)], sem).wait()


def _dispatch(h2, dest_flat):
    t, d = h2.shape
    tm = DISPATCH_TM
    return pl.pallas_call(
        _dispatch_kernel,
        out_shape=jax.ShapeDtypeStruct((t * TOP_K, d), h2.dtype),
        grid=(t // tm,),
        in_specs=[
            pl.BlockSpec((tm * TOP_K,), lambda i: (i,), memory_space=pltpu.SMEM),
            pl.BlockSpec((tm, d), lambda i: (i, 0)),
        ],
        out_specs=pl.BlockSpec(memory_space=pl.ANY),
        scratch_shapes=[pltpu.SemaphoreType.DMA],
        compiler_params=_params(("arbitrary",), 32),
        name="dispatch",
    )(dest_flat, h2)


def _expert_kernel(tile_ref, exp_ref, lo_ref, hi_ref, first_ref,
                   x_ref, wg_ref, bg_ref, wu_ref, bu_ref, wd_ref, bd_ref, o_ref, xs_scr):
    w = pl.program_id(0)
    j = pl.program_id(1)
    lo = lo_ref[w]
    hi = hi_ref[w]
    sub = EXPERT_SUB
    half = x_ref.shape[1]

    @pl.when((j == 0) & (first_ref[w] == 1))
    def _():
        x_lo, x_hi = _unpack_bf16_pairs(x_ref[...])
        xs_scr[:, :half] = x_lo
        xs_scr[:, half:] = x_hi
        o_ref[...] = jnp.zeros_like(o_ref)

    def ffn_tile(s, n=1):
        rows = slice(s * sub, (s + n) * sub)
        xs = xs_scr[rows, :]
        gate = jnp.dot(xs, wg_ref[0].astype(BF16), preferred_element_type=F32) + bg_ref[0]
        up = jnp.dot(xs, wu_ref[0].astype(BF16), preferred_element_type=F32) + bu_ref[0]
        gate = jnp.minimum(gate, SWIGLU_LIMIT)
        up = jnp.clip(up, -SWIGLU_LIMIT, SWIGLU_LIMIT)
        act = gate * _sigmoid(SWIGLU_ALPHA * gate) * (up + 1.0)
        row = lax.broadcasted_iota(jnp.int32, (n * sub, 1), 0) + s * sub
        mine = (row >= lo) & (row < hi)
        act = jnp.where(mine, act, 0.0).astype(BF16)
        y = jnp.dot(act, wd_ref[0].astype(BF16), preferred_element_type=F32)
        o_ref[rows, :] += y + jnp.where(mine & (j == 0), bd_ref[0], 0.0)

    n_sub = x_ref.shape[0] // sub
    active = [(lo < (s + 1) * sub) & (hi > s * sub) for s in range(n_sub)]
    whole = functools.reduce(jnp.logical_and, active)

    @pl.when(whole)
    def _():
        ffn_tile(0, n_sub)

    partial = jnp.logical_not(whole)
    for p in range(n_sub // 2):
        s0, s1 = 2 * p, 2 * p + 1

        @pl.when(partial & active[s0] & active[s1])
        def _(s0=s0):
            ffn_tile(s0, 2)

        @pl.when(partial & active[s0] & jnp.logical_not(active[s1]))
        def _(s0=s0):
            ffn_tile(s0)

        @pl.when(partial & active[s1] & jnp.logical_not(active[s0]))
        def _(s1=s1):
            ffn_tile(s1)


def _experts(xb, items, w_gate, b_gate, w_up, b_up, w_down, b_down):
    r, half = xb.shape
    e, d, f = w_gate.shape
    tm, tf = EXPERT_TM, EXPERT_TF
    nj = f // tf
    tile, expert, lo, hi, first = items
    n_items = tile.shape[0]

    def jj(j, w, hi_ref, lo_ref):
        return jnp.where(hi_ref[w] > lo_ref[w], j, nj - 1)

    return pl.pallas_call(
        _expert_kernel,
        out_shape=jax.ShapeDtypeStruct((r, d), F32),
        grid_spec=pltpu.PrefetchScalarGridSpec(
            num_scalar_prefetch=5,
            grid=(n_items, nj),
            in_specs=[
                pl.BlockSpec((tm, half), lambda w, j, ti, ex, lo_, hi_, fi: (ti[w], 0)),
                pl.BlockSpec((1, d, tf), lambda w, j, ti, ex, lo_, hi_, fi: (ex[w], 0, jj(j, w, hi_, lo_))),
                pl.BlockSpec((1, 1, tf), lambda w, j, ti, ex, lo_, hi_, fi: (ex[w], 0, jj(j, w, hi_, lo_))),
                pl.BlockSpec((1, d, tf), lambda w, j, ti, ex, lo_, hi_, fi: (ex[w], 0, jj(j, w, hi_, lo_))),
                pl.BlockSpec((1, 1, tf), lambda w, j, ti, ex, lo_, hi_, fi: (ex[w], 0, jj(j, w, hi_, lo_))),
                pl.BlockSpec((1, tf, d), lambda w, j, ti, ex, lo_, hi_, fi: (ex[w], jj(j, w, hi_, lo_), 0)),
                pl.BlockSpec((1, 1, d), lambda w, j, ti, ex, lo_, hi_, fi: (ex[w], 0, 0)),
            ],
            out_specs=pl.BlockSpec((tm, d), lambda w, j, ti, ex, lo_, hi_, fi: (ti[w], 0)),
            scratch_shapes=[pltpu.VMEM((tm, d), BF16)],
        ),
        compiler_params=_params(("arbitrary", "arbitrary"), 60),
        name="experts",
    )(tile, expert, lo, hi, first, xb, w_gate, b_gate.reshape(e, 1, f), w_up, b_up.reshape(e, 1, f),
      w_down, b_down.reshape(e, 1, d))


def _work_items(counts, n_rows):
    tm = EXPERT_TM
    n_tiles = n_rows // tm
    n_items = n_tiles + N_EXPERTS - 1
    cum = jnp.cumsum(counts)
    start = cum - counts
    tile_lo = jnp.arange(n_tiles, dtype=jnp.int32) * tm
    e_lo = jnp.searchsorted(cum, tile_lo, side="right").astype(jnp.int32)
    e_hi = jnp.searchsorted(cum, tile_lo + tm - 1, side="right").astype(jnp.int32)
    per_tile = e_hi - e_lo + 1
    off = jnp.cumsum(per_tile) - per_tile
    total = jnp.sum(per_tile)
    w = jnp.arange(n_items, dtype=jnp.int32)
    valid = w < total
    tile = jnp.clip(jnp.searchsorted(off, w, side="right").astype(jnp.int32) - 1, 0, n_tiles - 1)
    expert = jnp.where(valid, e_lo[tile] + w - off[tile], e_hi[n_tiles - 1])
    tile = jnp.where(valid, tile, n_tiles - 1)
    lo = jnp.clip(start[expert] - tile * tm, 0, tm)
    hi = jnp.clip(cum[expert] - tile * tm, 0, tm)
    hi = jnp.where(valid, jnp.maximum(hi, lo), lo)
    first = (valid & (w == off[tile])).astype(jnp.int32)
    return tile, expert.astype(jnp.int32), lo.astype(jnp.int32), hi.astype(jnp.int32), first, start


def _combine_kernel(dest_ref, dnext_ref, dnext2_ref, yb_ref, w_ref, x1_ref, g2_ref, fg_ref, o_ref, buf, sem):
    i = pl.program_id(0)
    n = pl.num_programs(0)
    tm = x1_ref.shape[0]
    slot = i % COMBINE_SLOTS

    def gather(d_ref, into):
        for t in range(tm):
            for k in range(TOP_K):
                pltpu.make_async_copy(yb_ref.at[pl.ds(d_ref[TOP_K * t + k], 1)], buf.at[into, k, pl.ds(t, 1)],
                                      sem.at[into]).start(priority=k % 2)

    def reduce(src):
        wts = w_ref[...]
        moe = buf[src, 0] * wts[:, 0:1]
        for k in range(1, TOP_K):
            moe += buf[src, k] * wts[:, k:k + 1]
        x2 = x1_ref[...] + g2_ref[0] * moe
        ms = jnp.mean(x2 * x2, axis=-1, keepdims=True)
        o_ref[...] = x2 * lax.rsqrt(ms + NORM_EPS) * fg_ref[...]

    @pl.when(i == 0)
    def _():
        gather(dest_ref, 0)
        gather(dnext_ref, 1)

    for k in range(TOP_K):
        pltpu.make_async_copy(yb_ref.at[pl.ds(0, tm)], buf.at[slot, k], sem.at[slot]).wait()

    for v in range(COMBINE_SLOTS):
        @pl.when((slot == v) & (i + 2 < n))
        def _(v=v):
            gather(dnext2_ref, (v + 2) % COMBINE_SLOTS)
            reduce(v)

    @pl.when(i + 2 >= n)
    def _():
        reduce(slot)


def _combine(yb, dest_flat, top_w, x1, g2, final_g):
    b, s, d = x1.shape
    tm = COMBINE_TM
    spt = s // tm
    n = b * spt
    assert n >= COMBINE_SLOTS
    out = pl.pallas_call(
        _combine_kernel,
        out_shape=jax.ShapeDtypeStruct((b * s, d), F32),
        grid=(n,),
        in_specs=[
            pl.BlockSpec((tm * TOP_K,), lambda i: (i,), memory_space=pltpu.SMEM),
            pl.BlockSpec((tm * TOP_K,), lambda i: (jnp.minimum(i + 1, n - 1),), memory_space=pltpu.SMEM),
            pl.BlockSpec((tm * TOP_K,), lambda i: (jnp.minimum(i + 2, n - 1),), memory_space=pltpu.SMEM),
            pl.BlockSpec(memory_space=pl.ANY),
            pl.BlockSpec((tm, LANES), lambda i: (i, 0)),
            pl.BlockSpec((tm, d), lambda i: (i, 0)),
            pl.BlockSpec((1, 1, d), lambda i: (i // spt, 0, 0)),
            pl.BlockSpec((1, d), lambda i: (0, 0)),
        ],
        out_specs=pl.BlockSpec((tm, d), lambda i: (i, 0)),
        scratch_shapes=[pltpu.VMEM((COMBINE_SLOTS, TOP_K, tm, d), F32), pltpu.SemaphoreType.DMA((COMBINE_SLOTS,))],
        compiler_params=_params(("arbitrary",), 40),
        name="combine",
    )(dest_flat, dest_flat, dest_flat, yb, top_w, x1.reshape(b * s, d), g2, final_g)
    return out.reshape(b, s, d)


def kernel(x, c, ctx, c_ctx, ada_w, ada_b, norm1_g, norm2_g, w_in, w_pa, w_pb, w_o, ret_decay_fwd, ret_decay_bwd,
           na_rpb, w_router, b_router, w_gate, b_gate, w_up, b_up, w_down, b_down, final_g):
    assert ada_w.shape[0] == 1, "single layer"
    b, s, d = x.shape
    l = ctx.shape[1]
    in_w = w_in.shape[2]
    rows = s // GRID_W
    assert s % IN_TM == 0 and l % RET_CHUNK == 0 and rows >= NA_WROWS and rows % NA_QROWS == 0

    c_rows = jnp.zeros((16, d), F32).at[:b].set(c).at[b].set(c_ctx)
    mod = _ada(c_rows, ada_w[0], ada_b[0][None, :])
    sh1, sc1, g1, sh2, sc2, g2 = [mod[:b, None, i * d:(i + 1) * d] for i in range(6)]
    shc1 = mod[b, 0 * d:1 * d][None, None, :]
    scc1 = mod[b, 1 * d:2 * d][None, None, :]

    w_in0 = w_in[0]
    n1 = norm1_g[0][None, :]
    proj = _inproj(x, n1, sh1, sc1, w_in0, tuple(range(in_w // IN_TN)), IN_TM)
    qk_w = RET_HEADS * RET_DK
    v_w = RET_HEADS * RET_DV
    na_w = NA_HEADS * NA_DH
    ctx_cols = tuple(range(qk_w // IN_TN, (2 * qk_w + v_w) // IN_TN)) + tuple(
        range((2 * qk_w + 2 * v_w + na_w) // IN_TN, (2 * qk_w + 2 * v_w + 3 * na_w) // IN_TN))
    projc = _inproj(ctx.reshape(1, b * l, d), n1, shc1, scc1, w_in0, ctx_cols, b * l).reshape(b, l, -1)

    lgf = jax.nn.log_sigmoid(ret_decay_fwd[0].astype(F32))
    lgb = jax.nn.log_sigmoid(ret_decay_bwd[0].astype(F32))
    cos_t, sin_t = _rope_tables(s)
    ret_in = _retention(proj, projc, lgf, lgb, cos_t, sin_t)
    na_in = _natt(proj, projc, _natt_pair_tiles(na_rpb[0]))

    t = b * s
    m = _merge(ret_in.reshape(t, -1), na_in.reshape(t, -1), proj.reshape(t, in_w),
               w_pa[0], w_pb[0])

    w_r = jnp.zeros((d, LANES), F32).at[:, :N_EXPERTS].set(w_router[0])
    b_r = jnp.full((1, LANES), NEG_INF, F32).at[0, :N_EXPERTS].set(b_router[0])
    x1, h2, logits = _oproj(m, x, g1, norm2_g[0][None, :], sh2, sc2, w_o[0].astype(BF16), w_r, b_r)

    top_e, top_w, rank, counts = _route(logits)
    counts = counts[0, :N_EXPERTS]
    items = _work_items(counts, t * TOP_K)
    start = items[5]
    is_e = top_e[:, :, None] == jnp.arange(N_EXPERTS, dtype=jnp.int32)
    dest = (jnp.sum(jnp.where(is_e, start.astype(jnp.int32), 0), axis=-1) + rank).reshape(-1)

    xb = _dispatch(h2, dest)
    yb = _experts(xb, items[:5], w_gate[0], b_gate[0], w_up[0], b_up[0], w_down[0], b_down[0])
    return _combine(yb, dest, top_w, x1, g2, final_g[None, :])
```

```python
import functools

import jax
import jax.numpy as jnp
import numpy as np
from jax import lax
from jax.experimental import pallas as pl
from jax.experimental.pallas import tpu as pltpu

F32 = jnp.float32
BF16 = jnp.bfloat16

GRID_W = 64
RET_HEADS = 8
RET_DK = 128
RET_DV = 256
RET_CHUNK = 256
ROPE_AXIS_DIM = RET_DK // 2
ROPE_BASE = 10000.0
NA_HEADS = 16
NA_DH = 128
NA_KR = 8
NA_KC = 16
N_EXPERTS = 32
TOP_K = 4
SWIGLU_ALPHA = 1.702
SWIGLU_LIMIT = 7.0
NORM_EPS = 1e-6
GN_EPS = 1e-5
NEG_INF = -1e30
LOG2_E = 1.4426950408889634

LANES = 128
MIB = 1024 * 1024

NA_QROWS = 4
NA_WROWS = NA_QROWS + NA_KR

IN_TM, IN_TN = 1024, 1024
MERGE_TM, MERGE_TN = 1024, 512
OPROJ_TM = 512
ROUTE_TM = 512
DISPATCH_TM = 256
EXPERT_TM, EXPERT_SUB, EXPERT_TF = 1024, 256, 512
COMBINE_TM = 128
COMBINE_SLOTS = 3


def _params(semantics, vmem_mib):
    return pltpu.CompilerParams(dimension_semantics=semantics, vmem_limit_bytes=vmem_mib * MIB)


def _sigmoid(x):
    return 1.0 / (1.0 + jnp.exp(-x))


def _ada_kernel(c_ref, w_ref, b_ref, o_ref):
    c = c_ref[...]
    s = c * _sigmoid(c)
    o_ref[...] = jnp.dot(s.astype(BF16), w_ref[...].astype(BF16), preferred_element_type=F32) + b_ref[...]


def _ada(c_rows, w, b):
    r, d = c_rows.shape
    n = w.shape[1]
    tn = 1024
    return pl.pallas_call(
        _ada_kernel,
        out_shape=jax.ShapeDtypeStruct((r, n), F32),
        grid=(n // tn,),
        in_specs=[
            pl.BlockSpec((r, d), lambda j: (0, 0)),
            pl.BlockSpec((d, tn), lambda j: (0, j)),
            pl.BlockSpec((1, tn), lambda j: (0, j)),
        ],
        out_specs=pl.BlockSpec((r, tn), lambda j: (0, j)),
        compiler_params=_params(("arbitrary",), 40),
        name="ada",
    )(c_rows, w, b)


def _inproj_kernel(cols_ref, x_ref, g_ref, sh_ref, sc_ref, w_ref, o_ref, h_scr):
    @pl.when(pl.program_id(2) == 0)
    def _():
        x = x_ref[0]
        ms = jnp.mean(x * x, axis=-1, keepdims=True)
        y = x * lax.rsqrt(ms + NORM_EPS) * g_ref[...]
        h_scr[...] = (y * (1.0 + sc_ref[0]) + sh_ref[0]).astype(BF16)

    o_ref[0] = jnp.dot(h_scr[...], w_ref[...].astype(BF16), preferred_element_type=F32).astype(o_ref.dtype)


def _inproj(x, g, shift, scale, w, col_tiles, tm):
    b, n, d = x.shape
    tn = IN_TN
    nct = len(col_tiles)
    cols = jnp.asarray(np.asarray(col_tiles, np.int32))
    return pl.pallas_call(
        _inproj_kernel,
        out_shape=jax.ShapeDtypeStruct((b, n, nct * tn), BF16),
        grid_spec=pltpu.PrefetchScalarGridSpec(
            num_scalar_prefetch=1,
            grid=(b, n // tm, nct),
            in_specs=[
                pl.BlockSpec((1, tm, d), lambda bi, i, j, c: (bi, i, 0)),
                pl.BlockSpec((1, d), lambda bi, i, j, c: (0, 0)),
                pl.BlockSpec((1, 1, d), lambda bi, i, j, c: (bi, 0, 0)),
                pl.BlockSpec((1, 1, d), lambda bi, i, j, c: (bi, 0, 0)),
                pl.BlockSpec((d, tn), lambda bi, i, j, c: (0, c[j])),
            ],
            out_specs=pl.BlockSpec((1, tm, tn), lambda bi, i, j, c: (bi, i, j)),
            scratch_shapes=[pltpu.VMEM((tm, d), BF16)],
        ),
        compiler_params=_params(("arbitrary", "arbitrary", "arbitrary"), 56),
        name="inproj",
    )(cols, x, g, shift, scale, w)


def _swap_halves(x):
    lane = lax.broadcasted_iota(jnp.int32, x.shape, 1)
    return jnp.where(lane % 64 < 32, pltpu.roll(x, 96, 1), pltpu.roll(x, 32, 1))


def _ret_kernel(lgf_ref, lgb_ref, q_ref, k_ref, v_ref, g_ref, kc_ref, vc_ref, cos_ref, sin_ref,
                o_ref, ks_scr, rf_scr, rb_scr, ub_scr):
    h = pl.program_id(1)
    c = RET_CHUNK
    n = q_ref.shape[1]
    nc = n // c
    ncc = kc_ref.shape[1] // c
    lgf = lgf_ref[h]
    lgb = lgb_ref[h]
    k_scale = RET_DK ** -0.5

    pos_c = lax.broadcasted_iota(jnp.int32, (c, 1), 0).astype(F32)
    zeta_f = jnp.exp((c - 1.0 - pos_c) * lgf)
    zeta_b = jnp.exp(pos_c * lgb)
    xi_f = jnp.exp((pos_c + 1.0) * lgf)
    xi_b = jnp.exp((c - pos_c) * lgb)
    one = jnp.ones((1, 1), F32)
    gc_f = jnp.exp(one * (c * lgf))
    gc_b = jnp.exp(one * (c * lgb))
    ii = lax.broadcasted_iota(jnp.int32, (c, c), 0)
    jj = lax.broadcasted_iota(jnp.int32, (c, c), 1)
    diff = (ii - jj).astype(F32)
    dmask = (jnp.where(diff >= 0, jnp.exp(jnp.maximum(diff, 0.0) * lgf), 0.0)
             + jnp.where(diff <= 0, jnp.exp(jnp.maximum(-diff, 0.0) * lgb), 0.0))

    def ktv(k_bf, v_f32, zeta):
        return jnp.dot(k_bf.astype(F32).T.astype(BF16), (v_f32 * zeta).astype(BF16), preferred_element_type=F32)

    r_f = jnp.zeros((RET_DK, RET_DV), F32)
    for i in range(ncc):
        kc = (kc_ref[0, i * c:(i + 1) * c, :].astype(F32) * k_scale).astype(BF16)
        r_f = gc_f * r_f + ktv(kc, vc_ref[0, i * c:(i + 1) * c, :].astype(F32), zeta_f)
    r_b = jnp.zeros((RET_DK, RET_DV), F32)
    for i in reversed(range(ncc)):
        kc = (kc_ref[0, i * c:(i + 1) * c, :].astype(F32) * k_scale).astype(BF16)
        r_b = gc_b * r_b + ktv(kc, vc_ref[0, i * c:(i + 1) * c, :].astype(F32), zeta_b)

    def rope(x, rows):
        x = x.astype(F32)
        return x * cos_ref[rows, :] + _swap_halves(x) * sin_ref[rows, :]

    def chunk_updates(i, r):
        rows = pl.ds(pl.multiple_of(i * c, c), c)
        kb = (rope(k_ref[0, rows, :], rows) * k_scale).astype(BF16)
        ks_scr[rows, :] = kb
        kt = kb.astype(F32).T.astype(BF16)
        v = v_ref[0, rows, :].astype(F32)
        rf_scr[i] = r.astype(BF16)
        ub_scr[i] = jnp.dot(kt, (v * zeta_b).astype(BF16), preferred_element_type=F32)
        return gc_f * r + jnp.dot(kt, (v * zeta_f).astype(BF16), preferred_element_type=F32)

    lax.fori_loop(0, nc, chunk_updates, r_f, unroll=8)

    def bwd_scan(t, r):
        i = nc - 1 - t
        rb_scr[i] = r.astype(BF16)
        return gc_b * r + ub_scr[i]

    lax.fori_loop(0, nc, bwd_scan, r_b, unroll=2)

    def out_chunk(i, carry):
        rows = pl.ds(pl.multiple_of(i * c, c), c)
        q = rope(q_ref[0, rows, :], rows)
        kb = ks_scr[rows, :]
        vb = v_ref[0, rows, :]
        s = lax.dot_general(q.astype(BF16), kb, (((1,), (1,)), ((), ())), preferred_element_type=F32) * dmask
        o = jnp.dot(s.astype(BF16), vb, preferred_element_type=F32)
        o += jnp.dot((q * xi_f).astype(BF16), rf_scr[i], preferred_element_type=F32)
        o += jnp.dot((q * xi_b).astype(BF16), rb_scr[i], preferred_element_type=F32)
        mu = jnp.mean(o, axis=-1, keepdims=True)
        d = o - mu
        var = jnp.mean(d * d, axis=-1, keepdims=True)
        on = d * lax.rsqrt(var + GN_EPS)
        g = g_ref[0, rows, :].astype(F32)
        o_ref[0, rows, :] = (g * _sigmoid(g) * on).astype(o_ref.dtype)
        return carry

    lax.fori_loop(0, nc, out_chunk, 0, unroll=8)


def _retention(proj, projc, lgf, lgb, cos_t, sin_t):
    b, s, _ = proj.shape
    l = projc.shape[1]
    hq = RET_HEADS
    v_off = 2 * hq * RET_DK // RET_DV
    g_off = v_off + hq
    cv_off = hq * RET_DK // RET_DV
    smem = pl.BlockSpec(memory_space=pltpu.SMEM)
    return pl.pallas_call(
        _ret_kernel,
        out_shape=jax.ShapeDtypeStruct((b, s, hq * RET_DV), BF16),
        grid=(b, hq),
        in_specs=[
            smem, smem,
            pl.BlockSpec((1, s, RET_DK), lambda bi, h: (bi, 0, h)),
            pl.BlockSpec((1, s, RET_DK), lambda bi, h: (bi, 0, hq + h)),
            pl.BlockSpec((1, s, RET_DV), lambda bi, h: (bi, 0, v_off + h)),
            pl.BlockSpec((1, s, RET_DV), lambda bi, h: (bi, 0, g_off + h)),
            pl.BlockSpec((1, l, RET_DK), lambda bi, h: (bi, 0, h)),
            pl.BlockSpec((1, l, RET_DV), lambda bi, h: (bi, 0, cv_off + h)),
            pl.BlockSpec((s, RET_DK), lambda bi, h: (0, 0)),
            pl.BlockSpec((s, RET_DK), lambda bi, h: (0, 0)),
        ],
        out_specs=pl.BlockSpec((1, s, RET_DV), lambda bi, h: (bi, 0, h)),
        scratch_shapes=[
            pltpu.VMEM((s, RET_DK), BF16),
            pltpu.VMEM((s // RET_CHUNK, RET_DK, RET_DV), BF16),
            pltpu.VMEM((s // RET_CHUNK, RET_DK, RET_DV), BF16),
            pltpu.VMEM((s // RET_CHUNK, RET_DK, RET_DV), F32),
        ],
        compiler_params=_params(("arbitrary", "arbitrary"), 48),
        name="ret",
    )(lgf, lgb, proj, proj, proj, proj, projc, projc, cos_t, sin_t)


def _rope_tables(n):
    rows = n // GRID_W
    inv = ROPE_BASE ** (-jnp.arange(0, ROPE_AXIS_DIM, 2, dtype=F32) / ROPE_AXIS_DIM)
    ar = jnp.arange(rows, dtype=F32)[:, None] * inv
    ac = jnp.arange(GRID_W, dtype=F32)[:, None] * inv
    f = inv.shape[0]

    def per_token(row_part, col_part):
        row_part = jnp.broadcast_to(row_part[:, None, :], (rows, GRID_W, 2 * f))
        col_part = jnp.broadcast_to(col_part[None, :, :], (rows, GRID_W, 2 * f))
        return jnp.concatenate([row_part, col_part], axis=2).reshape(n, 4 * f)

    cos_t = per_token(jnp.concatenate([jnp.cos(ar), jnp.cos(ar)], axis=1),
                      jnp.concatenate([jnp.cos(ac), jnp.cos(ac)], axis=1))
    sin_t = per_token(jnp.concatenate([-jnp.sin(ar), jnp.sin(ar)], axis=1),
                      jnp.concatenate([-jnp.sin(ac), jnp.sin(ac)], axis=1))
    return cos_t, sin_t


def _natt_kernel(q_ref, k_ref, v_ref, kc_ref, vc_ref, pair_ref, o_ref, bias_scr, v1_scr, vc1_scr):
    @pl.when(pl.program_id(1) == 0)
    def _():
        _natt_fill_bias(pair_ref, bias_scr)

    n = q_ref.shape[1]
    rows = n // GRID_W
    n_blk = rows // NA_QROWS
    nq = NA_QROWS * GRID_W
    nk = NA_WROWS * GRID_W
    scale = NA_DH ** -0.5
    kc = kc_ref[0]
    nt = (((1,), (1,)), ((), ()))
    v1_scr[:, :NA_DH] = v_ref[0]
    v1_scr[:, NA_DH:] = jnp.ones((n, NA_DH), BF16)
    vc1_scr[:, :NA_DH] = vc_ref[0]
    vc1_scr[:, NA_DH:] = jnp.ones((vc_ref.shape[1], NA_DH), BF16)

    def block(blk, carry):
        ws = jnp.clip(blk * NA_QROWS - NA_KR // 2, 0, rows - NA_WROWS)
        variant = jnp.where(blk == 0, 0, jnp.where(blk == n_blk - 1, 2, 1))
        qrows = pl.ds(pl.multiple_of(blk * nq, nq), nq)
        krows = pl.ds(pl.multiple_of(ws * GRID_W, GRID_W), nk)
        q = q_ref[0, qrows, :]
        s_loc = lax.dot_general(q, k_ref[0, krows, :], nt, preferred_element_type=F32) + bias_scr[variant]
        s_ctx = lax.dot_general(q, kc, nt, preferred_element_type=F32)
        m = jnp.maximum(jnp.max(s_loc, axis=-1, keepdims=True), jnp.max(s_ctx, axis=-1, keepdims=True))
        p_loc = jnp.exp2((s_loc - m) * (scale * LOG2_E))
        p_ctx = jnp.exp2((s_ctx - m) * (scale * LOG2_E))
        o = jnp.dot(p_loc.astype(BF16), v1_scr[krows, :], preferred_element_type=F32)
        o += jnp.dot(p_ctx.astype(BF16), vc1_scr[...], preferred_element_type=F32)
        o_ref[0, qrows, :] = (o[:, :NA_DH] / o[:, NA_DH:]).astype(o_ref.dtype)
        return carry

    lax.fori_loop(0, n_blk, block, 0, unroll=4)


def _natt_row_offsets():
    rq = np.arange(NA_QROWS)[:, None]
    wr = np.arange(NA_WROWS)[None, :]
    dr0 = np.where(wr < NA_KR, wr - rq + NA_KR - 1, -1)
    dr1 = np.where((wr >= rq) & (wr < rq + NA_KR), wr - rq + NA_KR // 2 - 1, -1)
    dr2 = np.where(wr >= NA_QROWS, wr - rq - 1, -1)
    return np.stack([dr0, dr1, dr2])


def _natt_fill_bias(pair_ref, bias_scr):
    w = GRID_W
    dr = _natt_row_offsets()
    neg = jnp.full((w, 2 * w), NEG_INF, F32)
    lane_row = lax.broadcasted_iota(jnp.int32, (w, NA_WROWS * w), 1) // w
    for kind in range(dr.shape[0]):
        for rq in range(NA_QROWS):
            tiles = []
            for p in range(NA_WROWS // 2):
                a, b_ = int(dr[kind, rq, 2 * p]), int(dr[kind, rq, 2 * p + 1])
                if a < 0 and b_ < 0:
                    tiles.append(neg)
                else:
                    tiles.append(pair_ref[0, b_ if b_ >= 0 else a + 1])
            strip = jnp.concatenate(tiles, axis=1)
            valid = np.nonzero(dr[kind, rq] >= 0)[0]
            keep = (lane_row >= int(valid[0])) & (lane_row <= int(valid[-1]))
            bias_scr[kind, rq * w:(rq + 1) * w, :] = jnp.where(keep, strip, NEG_INF)


def _natt_pair_tiles(rpb):
    w = GRID_W
    qc = np.arange(w)[:, None]
    kcol = np.arange(w)[None, :]
    cs = np.clip(qc - NA_KC // 2, 0, w - NA_KC)
    col_ok = (kcol >= cs) & (kcol < cs + NA_KC)
    dc_idx = np.clip(kcol - qc, -(NA_KC - 1), NA_KC - 1) + NA_KC - 1
    pick = (dc_idx.reshape(1, -1) == np.arange(2 * NA_KC - 1)[:, None]).astype(np.float32)
    nh, ndr, ndc = rpb.shape
    t = jnp.dot(rpb.astype(F32).reshape(nh * ndr, ndc), jnp.asarray(pick), precision=lax.Precision.HIGHEST)
    t = jnp.where(col_ok[None, None], t.reshape(nh, ndr, w, w) * NA_DH ** 0.5, NEG_INF)
    neg = jnp.full((rpb.shape[0], 1, w, w), NEG_INF, F32)
    return jnp.concatenate([jnp.concatenate([neg, t], axis=1), jnp.concatenate([t, neg], axis=1)], axis=3)


def _natt(proj, projc, pairs):
    b, s, _ = proj.shape
    l = projc.shape[1]
    nh = NA_HEADS
    q_off = (2 * RET_HEADS * RET_DK + 2 * RET_HEADS * RET_DV) // NA_DH
    k_off = q_off + nh
    v_off = k_off + nh
    ck_off = (RET_HEADS * RET_DK + RET_HEADS * RET_DV) // NA_DH
    cv_off = ck_off + nh
    nq = NA_QROWS * GRID_W
    nk = NA_WROWS * GRID_W
    return pl.pallas_call(
        _natt_kernel,
        out_shape=jax.ShapeDtypeStruct((b, s, nh * NA_DH), BF16),
        grid=(nh, b),
        in_specs=[
            pl.BlockSpec((1, s, NA_DH), lambda h, bi: (bi, 0, q_off + h)),
            pl.BlockSpec((1, s, NA_DH), lambda h, bi: (bi, 0, k_off + h)),
            pl.BlockSpec((1, s, NA_DH), lambda h, bi: (bi, 0, v_off + h)),
            pl.BlockSpec((1, l, NA_DH), lambda h, bi: (bi, 0, ck_off + h)),
            pl.BlockSpec((1, l, NA_DH), lambda h, bi: (bi, 0, cv_off + h)),
            pl.BlockSpec((1, 2 * NA_KR, GRID_W, 2 * GRID_W), lambda h, bi: (h, 0, 0, 0)),
        ],
        out_specs=pl.BlockSpec((1, s, NA_DH), lambda h, bi: (bi, 0, h)),
        scratch_shapes=[pltpu.VMEM((3, nq, nk), F32), pltpu.VMEM((s, 2 * NA_DH), BF16),
                        pltpu.VMEM((l, 2 * NA_DH), BF16)],
        compiler_params=_params(("arbitrary", "arbitrary"), 48),
        name="natt",
    )(proj, proj, proj, projc, projc, pairs)


def _merge_kernel(a_ref, n_ref, ga_ref, gb_ref, wa_ref, wb_ref, o_ref):
    ra = jnp.dot(a_ref[...], wa_ref[...].astype(BF16), preferred_element_type=F32)
    rn = jnp.dot(n_ref[...], wb_ref[...].astype(BF16), preferred_element_type=F32)
    o_ref[...] = (_sigmoid(ga_ref[...].astype(F32)) * ra + _sigmoid(gb_ref[...].astype(F32)) * rn).astype(o_ref.dtype)


def _merge(ret_in, na_in, proj2d, w_pa, w_pb):
    t, d = ret_in.shape
    tm, tn = MERGE_TM, MERGE_TN
    ga_off = (proj2d.shape[1] - 2 * d) // tn
    gb_off = (proj2d.shape[1] - d) // tn
    return pl.pallas_call(
        _merge_kernel,
        out_shape=jax.ShapeDtypeStruct((t, d), BF16),
        grid=(t // tm, d // tn),
        in_specs=[
            pl.BlockSpec((tm, ret_in.shape[1]), lambda i, j: (i, 0)),
            pl.BlockSpec((tm, na_in.shape[1]), lambda i, j: (i, 0)),
            pl.BlockSpec((tm, tn), lambda i, j: (i, ga_off + j)),
            pl.BlockSpec((tm, tn), lambda i, j: (i, gb_off + j)),
            pl.BlockSpec((w_pa.shape[0], tn), lambda i, j: (0, j)),
            pl.BlockSpec((w_pb.shape[0], tn), lambda i, j: (0, j)),
        ],
        out_specs=pl.BlockSpec((tm, tn), lambda i, j: (i, j)),
        compiler_params=_params(("arbitrary", "arbitrary"), 48),
        name="merge",
    )(ret_in, na_in, proj2d, proj2d, w_pa, w_pb)


def _oproj_kernel(m_ref, x_ref, g1_ref, ng_ref, sh_ref, sc_ref, wo_ref, wrh_ref, wrl_ref, br_ref,
                  x1_ref, h2_ref, lg_ref):
    y = jnp.dot(m_ref[...], wo_ref[...], preferred_element_type=F32)
    x1 = x_ref[0] + g1_ref[0] * y
    x1_ref[0] = x1
    ms = jnp.mean(x1 * x1, axis=-1, keepdims=True)
    h2 = x1 * lax.rsqrt(ms + NORM_EPS) * ng_ref[...]
    h2 = h2 * (1.0 + sc_ref[0]) + sh_ref[0]
    h2_ref[...] = h2
    h2_hi = h2.astype(BF16)
    h2_lo = (h2 - h2_hi.astype(F32)).astype(BF16)
    lg = jnp.dot(h2_hi, wrh_ref[...], preferred_element_type=F32)
    lg += jnp.dot(h2_lo, wrh_ref[...], preferred_element_type=F32)
    lg += jnp.dot(h2_hi, wrl_ref[...], preferred_element_type=F32)
    lg_ref[...] = lg + br_ref[...]


def _oproj(m, x, g1, norm_g, sh2, sc2, w_o, w_r, b_r):
    b, s, d = x.shape
    tm = OPROJ_TM
    spt = s // tm
    w_r_hi = w_r.astype(BF16)
    w_r_lo = (w_r - w_r_hi.astype(F32)).astype(BF16)
    return pl.pallas_call(
        _oproj_kernel,
        out_shape=(
            jax.ShapeDtypeStruct((b, s, d), F32),
            jax.ShapeDtypeStruct((b * s, d), F32),
            jax.ShapeDtypeStruct((b * s, LANES), F32),
        ),
        grid=(b, spt),
        in_specs=[
            pl.BlockSpec((tm, d), lambda bi, i: (bi * spt + i, 0)),
            pl.BlockSpec((1, tm, d), lambda bi, i: (bi, i, 0)),
            pl.BlockSpec((1, 1, d), lambda bi, i: (bi, 0, 0)),
            pl.BlockSpec((1, d), lambda bi, i: (0, 0)),
            pl.BlockSpec((1, 1, d), lambda bi, i: (bi, 0, 0)),
            pl.BlockSpec((1, 1, d), lambda bi, i: (bi, 0, 0)),
            pl.BlockSpec((d, d), lambda bi, i: (0, 0)),
            pl.BlockSpec((d, LANES), lambda bi, i: (0, 0)),
            pl.BlockSpec((d, LANES), lambda bi, i: (0, 0)),
            pl.BlockSpec((1, LANES), lambda bi, i: (0, 0)),
        ],
        out_specs=(
            pl.BlockSpec((1, tm, d), lambda bi, i: (bi, i, 0)),
            pl.BlockSpec((tm, d), lambda bi, i: (bi * spt + i, 0)),
            pl.BlockSpec((tm, LANES), lambda bi, i: (bi * spt + i, 0)),
        ),
        compiler_params=_params(("arbitrary", "arbitrary"), 48),
        name="oproj",
    )(m, x, g1, norm_g, sh2, sc2, w_o, w_r_hi, w_r_lo, b_r)


def _route_kernel(lg_ref, e_ref, w_ref, r_ref, cnt_ref, run_scr):
    i = pl.program_id(0)
    tm = lg_ref.shape[0]

    @pl.when(i == 0)
    def _():
        run_scr[...] = jnp.zeros_like(run_scr)

    l = lg_ref[...]
    lane = lax.broadcasted_iota(jnp.int32, l.shape, 1)
    vals, idxs, hots = [], [], []
    for _ in range(TOP_K):
        m = jnp.max(l, axis=-1, keepdims=True)
        idx = jnp.min(jnp.where(l == m, lane, LANES), axis=-1, keepdims=True)
        hot = lane == idx
        l = jnp.where(hot, -jnp.inf, l)
        vals.append(m)
        idxs.append(idx)
        hots.append(hot)
    exps = [jnp.exp(v - vals[0]) for v in vals]
    tot = exps[0]
    for e in exps[1:]:
        tot = tot + e

    member = hots[0]
    for hot in hots[1:]:
        member = member | hot
    member = member.astype(F32)
    ri = lax.broadcasted_iota(jnp.int32, (tm, tm), 0)
    ci = lax.broadcasted_iota(jnp.int32, (tm, tm), 1)
    lower = (ci < ri).astype(BF16)
    before = jnp.dot(lower, member.astype(BF16), preferred_element_type=F32) + run_scr[...]

    e_out = jnp.zeros(l.shape, jnp.int32)
    w_out = jnp.zeros(l.shape, F32)
    r_out = jnp.zeros(l.shape, jnp.int32)
    for k in range(TOP_K):
        rank = jnp.sum(jnp.where(hots[k], before, 0.0), axis=-1, keepdims=True).astype(jnp.int32)
        e_out = jnp.where(lane == k, idxs[k], e_out)
        w_out = jnp.where(lane == k, exps[k] / tot, w_out)
        r_out = jnp.where(lane == k, rank, r_out)
    e_ref[...] = e_out[:, :TOP_K]
    w_ref[...] = w_out
    r_ref[...] = r_out[:, :TOP_K]
    run_scr[...] += jnp.sum(member, axis=0, keepdims=True)
    cnt_ref[...] = run_scr[...].astype(jnp.int32)


def _route(logits):
    t = logits.shape[0]
    tm = ROUTE_TM
    row = pl.BlockSpec((tm, LANES), lambda i: (i, 0))
    narrow = pl.BlockSpec((tm, TOP_K), lambda i: (i, 0))
    return pl.pallas_call(
        _route_kernel,
        out_shape=(
            jax.ShapeDtypeStruct((t, TOP_K), jnp.int32),
            jax.ShapeDtypeStruct((t, LANES), F32),
            jax.ShapeDtypeStruct((t, TOP_K), jnp.int32),
            jax.ShapeDtypeStruct((1, LANES), jnp.int32),
        ),
        grid=(t // tm,),
        in_specs=[row],
        out_specs=(narrow, row, narrow, pl.BlockSpec((1, LANES), lambda i: (0, 0))),
        scratch_shapes=[pltpu.VMEM((1, LANES), F32)],
        compiler_params=_params(("arbitrary",), 32),
        name="route",
    )(logits)


def _dispatch_kernel(dest_ref, h_ref, xb_ref, sem):
    tm = h_ref.shape[0]

    for t in range(tm):
        for k in range(TOP_K):
            pltpu.make_async_copy(h_ref.at[pl.ds(t, 1)], xb_ref.at[pl.ds(dest_ref[TOP_K * t + k], 1)],
                                  sem).start(priority=k % 2)
    for k in range(TOP_K):
        pltpu.make_async_copy(h_ref, xb_ref.at[pl.ds(0, tm)], sem).wait()


def _dispatch(h2, dest_flat):
    t, d = h2.shape
    tm = DISPATCH_TM
    return pl.pallas_call(
        _dispatch_kernel,
        out_shape=jax.ShapeDtypeStruct((t * TOP_K, d), h2.dtype),
        grid=(t // tm,),
        in_specs=[
            pl.BlockSpec((tm * TOP_K,), lambda i: (i,), memory_space=pltpu.SMEM),
            pl.BlockSpec((tm, d), lambda i: (i, 0)),
        ],
        out_specs=pl.BlockSpec(memory_space=pl.ANY),
        scratch_shapes=[pltpu.SemaphoreType.DMA],
        compiler_params=_params(("arbitrary",), 32),
        name="dispatch",
    )(dest_flat, h2)


def _expert_kernel(tile_ref, exp_ref, lo_ref, hi_ref, first_ref,
                   x_hbm, wg_ref, bg_ref, wu_ref, bu_ref, wd_ref, bd_ref, o_ref, xs_scr, x_stage, x_sem):
    w = pl.program_id(0)
    j = pl.program_id(1)
    lo = lo_ref[w]
    hi = hi_ref[w]
    sub = EXPERT_SUB
    tm = o_ref.shape[0]
    n_tiles = x_hbm.shape[0] // tm

    def tile_copy(t):
        return pltpu.make_async_copy(x_hbm.at[pl.ds(pl.multiple_of(t * tm, tm), tm)], x_stage, x_sem)

    @pl.when((j == 0) & (first_ref[w] == 1))
    def _():
        t = tile_ref[w]

        @pl.when(w == 0)
        def _():
            tile_copy(t).start()

        tile_copy(t).wait()
        xs_scr[...] = x_stage[...].astype(BF16)

        @pl.when(t + 1 < n_tiles)
        def _():
            tile_copy(t + 1).start()

        o_ref[...] = jnp.zeros_like(o_ref)

    def ffn_tile(s, n=1):
        rows = slice(s * sub, (s + n) * sub)
        xs = xs_scr[rows, :]
        gate = jnp.dot(xs, wg_ref[0].astype(BF16), preferred_element_type=F32) + bg_ref[0]
        up = jnp.dot(xs, wu_ref[0].astype(BF16), preferred_element_type=F32) + bu_ref[0]
        gate = jnp.minimum(gate, SWIGLU_LIMIT)
        up = jnp.clip(up, -SWIGLU_LIMIT, SWIGLU_LIMIT)
        act = gate * _sigmoid(SWIGLU_ALPHA * gate) * (up + 1.0)
        row = lax.broadcasted_iota(jnp.int32, (n * sub, 1), 0) + s * sub
        mine = (row >= lo) & (row < hi)
        act = jnp.where(mine, act, 0.0).astype(BF16)
        y = jnp.dot(act, wd_ref[0].astype(BF16), preferred_element_type=F32)
        o_ref[rows, :] += y + jnp.where(mine & (j == 0), bd_ref[0], 0.0)

    n_sub = tm // sub
    active = [(lo < (s + 1) * sub) & (hi > s * sub) for s in range(n_sub)]
    whole = functools.reduce(jnp.logical_and, active)

    @pl.when(whole)
    def _():
        ffn_tile(0, n_sub)

    partial = jnp.logical_not(whole)
    for p in range(n_sub // 2):
        s0, s1 = 2 * p, 2 * p + 1

        @pl.when(partial & active[s0] & active[s1])
        def _(s0=s0):
            ffn_tile(s0, 2)

        @pl.when(partial & active[s0] & jnp.logical_not(active[s1]))
        def _(s0=s0):
            ffn_tile(s0)

        @pl.when(partial & active[s1] & jnp.logical_not(active[s0]))
        def _(s1=s1):
            ffn_tile(s1)


def _experts(xb, items, w_gate, b_gate, w_up, b_up, w_down, b_down):
    r = xb.shape[0]
    e, d, f = w_gate.shape
    tm, tf = EXPERT_TM, EXPERT_TF
    nj = f // tf
    tile, expert, lo, hi, first = items
    n_items = tile.shape[0]

    def jj(j, w, hi_ref, lo_ref):
        return jnp.where(hi_ref[w] > lo_ref[w], j, nj - 1)

    return pl.pallas_call(
        _expert_kernel,
        out_shape=jax.ShapeDtypeStruct((r, d), F32),
        grid_spec=pltpu.PrefetchScalarGridSpec(
            num_scalar_prefetch=5,
            grid=(n_items, nj),
            in_specs=[
                pl.BlockSpec(memory_space=pl.ANY),
                pl.BlockSpec((1, d, tf), lambda w, j, ti, ex, lo_, hi_, fi: (ex[w], 0, jj(j, w, hi_, lo_))),
                pl.BlockSpec((1, 1, tf), lambda w, j, ti, ex, lo_, hi_, fi: (ex[w], 0, jj(j, w, hi_, lo_))),
                pl.BlockSpec((1, d, tf), lambda w, j, ti, ex, lo_, hi_, fi: (ex[w], 0, jj(j, w, hi_, lo_))),
                pl.BlockSpec((1, 1, tf), lambda w, j, ti, ex, lo_, hi_, fi: (ex[w], 0, jj(j, w, hi_, lo_))),
                pl.BlockSpec((1, tf, d), lambda w, j, ti, ex, lo_, hi_, fi: (ex[w], jj(j, w, hi_, lo_), 0)),
                pl.BlockSpec((1, 1, d), lambda w, j, ti, ex, lo_, hi_, fi: (ex[w], 0, 0)),
            ],
            out_specs=pl.BlockSpec((tm, d), lambda w, j, ti, ex, lo_, hi_, fi: (ti[w], 0)),
            scratch_shapes=[pltpu.VMEM((tm, d), BF16), pltpu.VMEM((tm, d), F32), pltpu.SemaphoreType.DMA],
        ),
        compiler_params=_params(("arbitrary", "arbitrary"), 60),
        name="experts",
    )(tile, expert, lo, hi, first, xb, w_gate, b_gate.reshape(e, 1, f), w_up, b_up.reshape(e, 1, f),
      w_down, b_down.reshape(e, 1, d))


def _work_items(counts, n_rows):
    tm = EXPERT_TM
    n_tiles = n_rows // tm
    n_items = n_tiles + N_EXPERTS - 1
    def count_le(table, x):
        return jnp.sum((table[None, :] <= x[:, None]).astype(jnp.int32), axis=1)

    def lookup(table, idx):
        hit = idx[:, None] == jnp.arange(table.shape[0], dtype=jnp.int32)[None, :]
        return jnp.sum(jnp.where(hit, table[None, :], 0), axis=1)

    counts = counts.astype(jnp.int32)
    cum = jnp.cumsum(counts)
    start = cum - counts
    tile_lo = jnp.arange(n_tiles, dtype=jnp.int32) * tm
    e_lo = count_le(cum, tile_lo)
    e_hi = count_le(cum, tile_lo + tm - 1)
    per_tile = e_hi - e_lo + 1
    off = jnp.cumsum(per_tile) - per_tile
    total = jnp.sum(per_tile)
    w = jnp.arange(n_items, dtype=jnp.int32)
    valid = w < total
    tile = jnp.clip(count_le(off, w) - 1, 0, n_tiles - 1)
    off_t = lookup(off, tile)
    expert = jnp.where(valid, lookup(e_lo, tile) + w - off_t, e_hi[n_tiles - 1])
    tile = jnp.where(valid, tile, n_tiles - 1)
    lo = jnp.clip(lookup(start, expert) - tile * tm, 0, tm)
    hi = jnp.clip(lookup(cum, expert) - tile * tm, 0, tm)
    hi = jnp.where(valid, jnp.maximum(hi, lo), lo)
    first = (valid & (w == off_t)).astype(jnp.int32)
    return tile, expert.astype(jnp.int32), lo.astype(jnp.int32), hi.astype(jnp.int32), first, start


def _combine_kernel(dest_ref, dnext_ref, dnext2_ref, yb_ref, w_ref, x1_ref, g2_ref, fg_ref, o_ref, buf, sem):
    i = pl.program_id(0)
    n = pl.num_programs(0)
    tm = x1_ref.shape[0]
    slot = i % COMBINE_SLOTS

    def gather(d_ref, into):
        for t in range(tm):
            for k in range(TOP_K):
                pltpu.make_async_copy(yb_ref.at[pl.ds(d_ref[TOP_K * t + k], 1)], buf.at[into, k, pl.ds(t, 1)],
                                      sem.at[into]).start(priority=k % 2)

    def reduce(src):
        wts = w_ref[...]
        moe = buf[src, 0] * wts[:, 0:1]
        for k in range(1, TOP_K):
            moe += buf[src, k] * wts[:, k:k + 1]
        x2 = x1_ref[...] + g2_ref[0] * moe
        ms = jnp.mean(x2 * x2, axis=-1, keepdims=True)
        o_ref[...] = x2 * lax.rsqrt(ms + NORM_EPS) * fg_ref[...]

    @pl.when(i == 0)
    def _():
        gather(dest_ref, 0)
        gather(dnext_ref, 1)

    for k in range(TOP_K):
        pltpu.make_async_copy(yb_ref.at[pl.ds(0, tm)], buf.at[slot, k], sem.at[slot]).wait()

    for v in range(COMBINE_SLOTS):
        @pl.when((slot == v) & (i + 2 < n))
        def _(v=v):
            gather(dnext2_ref, (v + 2) % COMBINE_SLOTS)
            reduce(v)

    @pl.when(i + 2 >= n)
    def _():
        reduce(slot)


def _combine(yb, dest_flat, top_w, x1, g2, final_g):
    b, s, d = x1.shape
    tm = COMBINE_TM
    spt = s // tm
    n = b * spt
    assert n >= COMBINE_SLOTS
    out = pl.pallas_call(
        _combine_kernel,
        out_shape=jax.ShapeDtypeStruct((b * s, d), F32),
        grid=(n,),
        in_specs=[
            pl.BlockSpec((tm * TOP_K,), lambda i: (i,), memory_space=pltpu.SMEM),
            pl.BlockSpec((tm * TOP_K,), lambda i: (jnp.minimum(i + 1, n - 1),), memory_space=pltpu.SMEM),
            pl.BlockSpec((tm * TOP_K,), lambda i: (jnp.minimum(i + 2, n - 1),), memory_space=pltpu.SMEM),
            pl.BlockSpec(memory_space=pl.ANY),
            pl.BlockSpec((tm, LANES), lambda i: (i, 0)),
            pl.BlockSpec((tm, d), lambda i: (i, 0)),
            pl.BlockSpec((1, 1, d), lambda i: (i // spt, 0, 0)),
            pl.BlockSpec((1, d), lambda i: (0, 0)),
        ],
        out_specs=pl.BlockSpec((tm, d), lambda i: (i, 0)),
        scratch_shapes=[pltpu.VMEM((COMBINE_SLOTS, TOP_K, tm, d), F32), pltpu.SemaphoreType.DMA((COMBINE_SLOTS,))],
        compiler_params=_params(("arbitrary",), 40),
        name="combine",
    )(dest_flat, dest_flat, dest_flat, yb, top_w, x1.reshape(b * s, d), g2, final_g)
    return out.reshape(b, s, d)


def kernel(x, c, ctx, c_ctx, ada_w, ada_b, norm1_g, norm2_g, w_in, w_pa, w_pb, w_o, ret_decay_fwd, ret_decay_bwd,
           na_rpb, w_router, b_router, w_gate, b_gate, w_up, b_up, w_down, b_down, final_g):
    assert ada_w.shape[0] == 1, "single layer"
    b, s, d = x.shape
    l = ctx.shape[1]
    in_w = w_in.shape[2]
    rows = s // GRID_W
    assert s % IN_TM == 0 and l % RET_CHUNK == 0 and rows >= NA_WROWS and rows % NA_QROWS == 0

    c_rows = jnp.zeros((16, d), F32).at[:b].set(c).at[b].set(c_ctx)
    mod = _ada(c_rows, ada_w[0], ada_b[0][None, :])
    sh1, sc1, g1, sh2, sc2, g2 = [mod[:b, None, i * d:(i + 1) * d] for i in range(6)]
    shc1 = mod[b, 0 * d:1 * d][None, None, :]
    scc1 = mod[b, 1 * d:2 * d][None, None, :]

    w_in0 = w_in[0]
    n1 = norm1_g[0][None, :]
    proj = _inproj(x, n1, sh1, sc1, w_in0, tuple(range(in_w // IN_TN)), IN_TM)
    qk_w = RET_HEADS * RET_DK
    v_w = RET_HEADS * RET_DV
    na_w = NA_HEADS * NA_DH
    ctx_cols = tuple(range(qk_w // IN_TN, (2 * qk_w + v_w) // IN_TN)) + tuple(
        range((2 * qk_w + 2 * v_w + na_w) // IN_TN, (2 * qk_w + 2 * v_w + 3 * na_w) // IN_TN))
    projc = _inproj(ctx.reshape(1, b * l, d), n1, shc1, scc1, w_in0, ctx_cols, b * l).reshape(b, l, -1)

    lgf = jax.nn.log_sigmoid(ret_decay_fwd[0].astype(F32))
    lgb = jax.nn.log_sigmoid(ret_decay_bwd[0].astype(F32))
    cos_t, sin_t = _rope_tables(s)
    ret_in = _retention(proj, projc, lgf, lgb, cos_t, sin_t)
    na_in = _natt(proj, projc, _natt_pair_tiles(na_rpb[0]))

    t = b * s
    m = _merge(ret_in.reshape(t, -1), na_in.reshape(t, -1), proj.reshape(t, in_w),
               w_pa[0], w_pb[0])

    w_r = jnp.zeros((d, LANES), F32).at[:, :N_EXPERTS].set(w_router[0])
    b_r = jnp.full((1, LANES), NEG_INF, F32).at[0, :N_EXPERTS].set(b_router[0])
    x1, h2, logits = _oproj(m, x, g1, norm2_g[0][None, :], sh2, sc2, w_o[0].astype(BF16), w_r, b_r)

    top_e, top_w, rank, counts = _route(logits)
    counts = counts[0, :N_EXPERTS]
    items = _work_items(counts, t * TOP_K)
    start = items[5]
    is_e = top_e[:, :, None] == jnp.arange(N_EXPERTS, dtype=jnp.int32)
    dest = (jnp.sum(jnp.where(is_e, start.astype(jnp.int32), 0), axis=-1) + rank).reshape(-1)

    xb = _dispatch(h2, dest)
    yb = _experts(xb, items[:5], w_gate[0], b_gate[0], w_up[0], b_up[0], w_down[0], b_down[0])
    return _combine(yb, dest, top_w, x1, g2, final_g[None, :])
```

```python
import functools

import jax
import jax.numpy as jnp
import numpy as np
from jax import lax
from jax.experimental import pallas as pl
from jax.experimental.pallas import tpu as pltpu

F32 = jnp.float32
BF16 = jnp.bfloat16

GRID_W = 64
RET_HEADS = 8
RET_DK = 128
RET_DV = 256
RET_CHUNK = 256
ROPE_AXIS_DIM = RET_DK // 2
ROPE_BASE = 10000.0
NA_HEADS = 16
NA_DH = 128
NA_KR = 8
NA_KC = 16
N_EXPERTS = 32
TOP_K = 4
SWIGLU_ALPHA = 1.702
SWIGLU_LIMIT = 7.0
NORM_EPS = 1e-6
GN_EPS = 1e-5
NEG_INF = -1e30
LOG2_E = 1.4426950408889634

LANES = 128
MIB = 1024 * 1024

NA_QROWS = 4
NA_WROWS = NA_QROWS + NA_KR

IN_TM, IN_TN = 1024, 1024
MERGE_TM, MERGE_TN = 1024, 512
OPROJ_TM = 512
ROUTE_TM = 512
DISPATCH_TM = 512
EXPERT_TM, EXPERT_SUB, EXPERT_TF = 1024, 256, 512
COMBINE_TM = 256
COMBINE_SLOTS = 3


def _params(semantics, vmem_mib):
    return pltpu.CompilerParams(dimension_semantics=semantics, vmem_limit_bytes=vmem_mib * MIB)


def _sigmoid(x):
    return 1.0 / (1.0 + jnp.exp(-x))


def _ada_kernel(c_ref, w_ref, b_ref, o_ref):
    c = c_ref[...]
    s = c * _sigmoid(c)
    o_ref[...] = jnp.dot(s.astype(BF16), w_ref[...].astype(BF16), preferred_element_type=F32) + b_ref[...]


def _ada(c_rows, w, b):
    r, d = c_rows.shape
    n = w.shape[1]
    tn = 1024
    return pl.pallas_call(
        _ada_kernel,
        out_shape=jax.ShapeDtypeStruct((r, n), F32),
        grid=(n // tn,),
        in_specs=[
            pl.BlockSpec((r, d), lambda j: (0, 0)),
            pl.BlockSpec((d, tn), lambda j: (0, j)),
            pl.BlockSpec((1, tn), lambda j: (0, j)),
        ],
        out_specs=pl.BlockSpec((r, tn), lambda j: (0, j)),
        compiler_params=_params(("arbitrary",), 40),
        name="ada",
    )(c_rows, w, b)


def _inproj_kernel(cols_ref, x_ref, g_ref, sh_ref, sc_ref, w_ref, o_ref, h_scr):
    @pl.when(pl.program_id(2) == 0)
    def _():
        x = x_ref[0]
        ms = jnp.mean(x * x, axis=-1, keepdims=True)
        y = x * lax.rsqrt(ms + NORM_EPS) * g_ref[...]
        h_scr[...] = (y * (1.0 + sc_ref[0]) + sh_ref[0]).astype(BF16)

    o_ref[0] = jnp.dot(h_scr[...], w_ref[...].astype(BF16), preferred_element_type=F32).astype(o_ref.dtype)


def _inproj(x, g, shift, scale, w, col_tiles, tm):
    b, n, d = x.shape
    tn = IN_TN
    nct = len(col_tiles)
    cols = jnp.asarray(np.asarray(col_tiles, np.int32))
    return pl.pallas_call(
        _inproj_kernel,
        out_shape=jax.ShapeDtypeStruct((b, n, nct * tn), BF16),
        grid_spec=pltpu.PrefetchScalarGridSpec(
            num_scalar_prefetch=1,
            grid=(b, n // tm, nct),
            in_specs=[
                pl.BlockSpec((1, tm, d), lambda bi, i, j, c: (bi, i, 0)),
                pl.BlockSpec((1, d), lambda bi, i, j, c: (0, 0)),
                pl.BlockSpec((1, 1, d), lambda bi, i, j, c: (bi, 0, 0)),
                pl.BlockSpec((1, 1, d), lambda bi, i, j, c: (bi, 0, 0)),
                pl.BlockSpec((d, tn), lambda bi, i, j, c: (0, c[j])),
            ],
            out_specs=pl.BlockSpec((1, tm, tn), lambda bi, i, j, c: (bi, i, j)),
            scratch_shapes=[pltpu.VMEM((tm, d), BF16)],
        ),
        compiler_params=_params(("arbitrary", "arbitrary", "arbitrary"), 56),
        name="inproj",
    )(cols, x, g, shift, scale, w)


def _swap_halves(x):
    half = ROPE_AXIS_DIM // 2
    lane = lax.broadcasted_iota(jnp.int32, x.shape, 1)
    return jnp.where(lane % ROPE_AXIS_DIM < half, pltpu.roll(x, RET_DK - half, 1), pltpu.roll(x, half, 1))


def _ret_kernel(lgf_ref, lgb_ref, q_ref, k_ref, v_ref, g_ref, kc_ref, vc_ref, cos_ref, sin_ref,
                o_ref, ks_scr, rf_scr, rb_scr, ub_scr):
    h = pl.program_id(1)
    c = RET_CHUNK
    n = q_ref.shape[1]
    nc = n // c
    ncc = kc_ref.shape[1] // c
    lgf = lgf_ref[h]
    lgb = lgb_ref[h]
    k_scale = RET_DK ** -0.5

    pos_c = lax.broadcasted_iota(jnp.int32, (c, 1), 0).astype(F32)
    zeta_f = jnp.exp((c - 1.0 - pos_c) * lgf)
    zeta_b = jnp.exp(pos_c * lgb)
    xi_f = jnp.exp((pos_c + 1.0) * lgf)
    xi_b = jnp.exp((c - pos_c) * lgb)
    one = jnp.ones((1, 1), F32)
    gc_f = jnp.exp(one * (c * lgf))
    gc_b = jnp.exp(one * (c * lgb))
    ii = lax.broadcasted_iota(jnp.int32, (c, c), 0)
    jj = lax.broadcasted_iota(jnp.int32, (c, c), 1)
    diff = (ii - jj).astype(F32)
    dmask = (jnp.where(diff >= 0, jnp.exp(jnp.maximum(diff, 0.0) * lgf), 0.0)
             + jnp.where(diff <= 0, jnp.exp(jnp.maximum(-diff, 0.0) * lgb), 0.0))

    def ktv(k_bf, v_f32, zeta):
        return jnp.dot(k_bf.astype(F32).T.astype(BF16), (v_f32 * zeta).astype(BF16), preferred_element_type=F32)

    r_f = jnp.zeros((RET_DK, RET_DV), F32)
    for i in range(ncc):
        kc = (kc_ref[0, i * c:(i + 1) * c, :].astype(F32) * k_scale).astype(BF16)
        r_f = gc_f * r_f + ktv(kc, vc_ref[0, i * c:(i + 1) * c, :].astype(F32), zeta_f)
    r_b = jnp.zeros((RET_DK, RET_DV), F32)
    for i in reversed(range(ncc)):
        kc = (kc_ref[0, i * c:(i + 1) * c, :].astype(F32) * k_scale).astype(BF16)
        r_b = gc_b * r_b + ktv(kc, vc_ref[0, i * c:(i + 1) * c, :].astype(F32), zeta_b)

    def rope(x, rows):
        x = x.astype(F32)
        return x * cos_ref[rows, :] + _swap_halves(x) * sin_ref[rows, :]

    def chunk_updates(i, r):
        rows = pl.ds(pl.multiple_of(i * c, c), c)
        kb = (rope(k_ref[0, rows, :], rows) * k_scale).astype(BF16)
        ks_scr[rows, :] = kb
        kt = kb.astype(F32).T.astype(BF16)
        v = v_ref[0, rows, :].astype(F32)
        rf_scr[i] = r.astype(BF16)
        ub_scr[i] = jnp.dot(kt, (v * zeta_b).astype(BF16), preferred_element_type=F32)
        return gc_f * r + jnp.dot(kt, (v * zeta_f).astype(BF16), preferred_element_type=F32)

    lax.fori_loop(0, nc, chunk_updates, r_f, unroll=8)

    def bwd_scan(t, r):
        i = nc - 1 - t
        rb_scr[i] = r.astype(BF16)
        return gc_b * r + ub_scr[i]

    lax.fori_loop(0, nc, bwd_scan, r_b, unroll=2)

    def out_chunk(i, carry):
        rows = pl.ds(pl.multiple_of(i * c, c), c)
        q = rope(q_ref[0, rows, :], rows)
        kb = ks_scr[rows, :]
        vb = v_ref[0, rows, :]
        s = lax.dot_general(q.astype(BF16), kb, (((1,), (1,)), ((), ())), preferred_element_type=F32) * dmask
        o = jnp.dot(s.astype(BF16), vb, preferred_element_type=F32)
        o += jnp.dot((q * xi_f).astype(BF16), rf_scr[i], preferred_element_type=F32)
        o += jnp.dot((q * xi_b).astype(BF16), rb_scr[i], preferred_element_type=F32)
        mu = jnp.mean(o, axis=-1, keepdims=True)
        d = o - mu
        var = jnp.mean(d * d, axis=-1, keepdims=True)
        on = d * lax.rsqrt(var + GN_EPS)
        g = g_ref[0, rows, :].astype(F32)
        o_ref[0, rows, :] = (g * _sigmoid(g) * on).astype(o_ref.dtype)
        return carry

    lax.fori_loop(0, nc, out_chunk, 0, unroll=8)


def _retention(proj, projc, lgf, lgb, cos_t, sin_t):
    b, s, _ = proj.shape
    l = projc.shape[1]
    hq = RET_HEADS
    v_off = 2 * hq * RET_DK // RET_DV
    g_off = v_off + hq
    cv_off = hq * RET_DK // RET_DV
    smem = pl.BlockSpec(memory_space=pltpu.SMEM)
    return pl.pallas_call(
        _ret_kernel,
        out_shape=jax.ShapeDtypeStruct((b, s, hq * RET_DV), BF16),
        grid=(b, hq),
        in_specs=[
            smem, smem,
            pl.BlockSpec((1, s, RET_DK), lambda bi, h: (bi, 0, h)),
            pl.BlockSpec((1, s, RET_DK), lambda bi, h: (bi, 0, hq + h)),
            pl.BlockSpec((1, s, RET_DV), lambda bi, h: (bi, 0, v_off + h)),
            pl.BlockSpec((1, s, RET_DV), lambda bi, h: (bi, 0, g_off + h)),
            pl.BlockSpec((1, l, RET_DK), lambda bi, h: (bi, 0, h)),
            pl.BlockSpec((1, l, RET_DV), lambda bi, h: (bi, 0, cv_off + h)),
            pl.BlockSpec((s, RET_DK), lambda bi, h: (0, 0)),
            pl.BlockSpec((s, RET_DK), lambda bi, h: (0, 0)),
        ],
        out_specs=pl.BlockSpec((1, s, RET_DV), lambda bi, h: (bi, 0, h)),
        scratch_shapes=[
            pltpu.VMEM((s, RET_DK), BF16),
            pltpu.VMEM((s // RET_CHUNK, RET_DK, RET_DV), BF16),
            pltpu.VMEM((s // RET_CHUNK, RET_DK, RET_DV), BF16),
            pltpu.VMEM((s // RET_CHUNK, RET_DK, RET_DV), F32),
        ],
        compiler_params=_params(("arbitrary", "arbitrary"), 48),
        name="ret",
    )(lgf, lgb, proj, proj, proj, proj, projc, projc, cos_t, sin_t)


def _rope_tables(n):
    rows = n // GRID_W
    inv = ROPE_BASE ** (-jnp.arange(0, ROPE_AXIS_DIM, 2, dtype=F32) / ROPE_AXIS_DIM)
    ar = jnp.arange(rows, dtype=F32)[:, None] * inv
    ac = jnp.arange(GRID_W, dtype=F32)[:, None] * inv
    f = inv.shape[0]

    def per_token(row_part, col_part):
        row_part = jnp.broadcast_to(row_part[:, None, :], (rows, GRID_W, 2 * f))
        col_part = jnp.broadcast_to(col_part[None, :, :], (rows, GRID_W, 2 * f))
        return jnp.concatenate([row_part, col_part], axis=2).reshape(n, 4 * f)

    cos_t = per_token(jnp.concatenate([jnp.cos(ar), jnp.cos(ar)], axis=1),
                      jnp.concatenate([jnp.cos(ac), jnp.cos(ac)], axis=1))
    sin_t = per_token(jnp.concatenate([-jnp.sin(ar), jnp.sin(ar)], axis=1),
                      jnp.concatenate([-jnp.sin(ac), jnp.sin(ac)], axis=1))
    return cos_t, sin_t


def _natt_kernel(q_ref, k_ref, v_ref, kc_ref, vc_ref, pair_ref, o_ref, bias_scr, v1_scr, vc1_scr):
    @pl.when(pl.program_id(1) == 0)
    def _():
        _natt_fill_bias(pair_ref, bias_scr)

    n = q_ref.shape[1]
    rows = n // GRID_W
    n_blk = rows // NA_QROWS
    nq = NA_QROWS * GRID_W
    nk = NA_WROWS * GRID_W
    scale = NA_DH ** -0.5
    kc = kc_ref[0]
    nt = (((1,), (1,)), ((), ()))
    v1_scr[:, :NA_DH] = v_ref[0]
    v1_scr[:, NA_DH:] = jnp.ones((n, NA_DH), BF16)
    vc1_scr[:, :NA_DH] = vc_ref[0]
    vc1_scr[:, NA_DH:] = jnp.ones((vc_ref.shape[1], NA_DH), BF16)

    def block(blk, carry):
        ws = jnp.clip(blk * NA_QROWS - NA_KR // 2, 0, rows - NA_WROWS)
        variant = jnp.where(blk == 0, 0, jnp.where(blk == n_blk - 1, 2, 1))
        qrows = pl.ds(pl.multiple_of(blk * nq, nq), nq)
        krows = pl.ds(pl.multiple_of(ws * GRID_W, GRID_W), nk)
        q = q_ref[0, qrows, :]
        s_loc = lax.dot_general(q, k_ref[0, krows, :], nt, preferred_element_type=F32) + bias_scr[variant]
        s_ctx = lax.dot_general(q, kc, nt, preferred_element_type=F32)
        m = jnp.maximum(jnp.max(s_loc, axis=-1, keepdims=True), jnp.max(s_ctx, axis=-1, keepdims=True))
        p_loc = jnp.exp2((s_loc - m) * (scale * LOG2_E))
        p_ctx = jnp.exp2((s_ctx - m) * (scale * LOG2_E))
        o = jnp.dot(p_loc.astype(BF16), v1_scr[krows, :], preferred_element_type=F32)
        o += jnp.dot(p_ctx.astype(BF16), vc1_scr[...], preferred_element_type=F32)
        o_ref[0, qrows, :] = (o[:, :NA_DH] / o[:, NA_DH:]).astype(o_ref.dtype)
        return carry

    lax.fori_loop(0, n_blk, block, 0, unroll=4)


def _natt_row_offsets():
    rq = np.arange(NA_QROWS)[:, None]
    wr = np.arange(NA_WROWS)[None, :]
    dr0 = np.where(wr < NA_KR, wr - rq + NA_KR - 1, -1)
    dr1 = np.where((wr >= rq) & (wr < rq + NA_KR), wr - rq + NA_KR // 2 - 1, -1)
    dr2 = np.where(wr >= NA_QROWS, wr - rq - 1, -1)
    return np.stack([dr0, dr1, dr2])


def _natt_fill_bias(pair_ref, bias_scr):
    w = GRID_W
    dr = _natt_row_offsets()
    neg = jnp.full((w, 2 * w), NEG_INF, F32)
    lane_row = lax.broadcasted_iota(jnp.int32, (w, NA_WROWS * w), 1) // w
    for kind in range(dr.shape[0]):
        for rq in range(NA_QROWS):
            tiles = []
            for p in range(NA_WROWS // 2):
                a, b_ = int(dr[kind, rq, 2 * p]), int(dr[kind, rq, 2 * p + 1])
                if a < 0 and b_ < 0:
                    tiles.append(neg)
                else:
                    tiles.append(pair_ref[0, b_ if b_ >= 0 else a + 1])
            strip = jnp.concatenate(tiles, axis=1)
            valid = np.nonzero(dr[kind, rq] >= 0)[0]
            keep = (lane_row >= int(valid[0])) & (lane_row <= int(valid[-1]))
            bias_scr[kind, rq * w:(rq + 1) * w, :] = jnp.where(keep, strip, NEG_INF)


def _natt_pair_tiles(rpb):
    w = GRID_W
    qc = np.arange(w)[:, None]
    kcol = np.arange(w)[None, :]
    cs = np.clip(qc - NA_KC // 2, 0, w - NA_KC)
    col_ok = (kcol >= cs) & (kcol < cs + NA_KC)
    dc_idx = np.clip(kcol - qc, -(NA_KC - 1), NA_KC - 1) + NA_KC - 1
    pick = (dc_idx.reshape(1, -1) == np.arange(2 * NA_KC - 1)[:, None]).astype(np.float32)
    nh, ndr, ndc = rpb.shape
    t = jnp.dot(rpb.astype(F32).reshape(nh * ndr, ndc), jnp.asarray(pick), precision=lax.Precision.HIGHEST)
    t = jnp.where(col_ok[None, None], t.reshape(nh, ndr, w, w) * NA_DH ** 0.5, NEG_INF)
    neg = jnp.full((rpb.shape[0], 1, w, w), NEG_INF, F32)
    return jnp.concatenate([jnp.concatenate([neg, t], axis=1), jnp.concatenate([t, neg], axis=1)], axis=3)


def _natt(proj, projc, pairs):
    b, s, _ = proj.shape
    l = projc.shape[1]
    nh = NA_HEADS
    q_off = (2 * RET_HEADS * RET_DK + 2 * RET_HEADS * RET_DV) // NA_DH
    k_off = q_off + nh
    v_off = k_off + nh
    ck_off = (RET_HEADS * RET_DK + RET_HEADS * RET_DV) // NA_DH
    cv_off = ck_off + nh
    nq = NA_QROWS * GRID_W
    nk = NA_WROWS * GRID_W
    return pl.pallas_call(
        _natt_kernel,
        out_shape=jax.ShapeDtypeStruct((b, s, nh * NA_DH), BF16),
        grid=(nh, b),
        in_specs=[
            pl.BlockSpec((1, s, NA_DH), lambda h, bi: (bi, 0, q_off + h)),
            pl.BlockSpec((1, s, NA_DH), lambda h, bi: (bi, 0, k_off + h)),
            pl.BlockSpec((1, s, NA_DH), lambda h, bi: (bi, 0, v_off + h)),
            pl.BlockSpec((1, l, NA_DH), lambda h, bi: (bi, 0, ck_off + h)),
            pl.BlockSpec((1, l, NA_DH), lambda h, bi: (bi, 0, cv_off + h)),
            pl.BlockSpec((1, 2 * NA_KR, GRID_W, 2 * GRID_W), lambda h, bi: (h, 0, 0, 0)),
        ],
        out_specs=pl.BlockSpec((1, s, NA_DH), lambda h, bi: (bi, 0, h)),
        scratch_shapes=[pltpu.VMEM((3, nq, nk), F32), pltpu.VMEM((s, 2 * NA_DH), BF16),
                        pltpu.VMEM((l, 2 * NA_DH), BF16)],
        compiler_params=_params(("arbitrary", "arbitrary"), 48),
        name="natt",
    )(proj, proj, proj, projc, projc, pairs)


def _merge_kernel(a_ref, n_ref, ga_ref, gb_ref, wa_ref, wb_ref, o_ref):
    ra = jnp.dot(a_ref[...], wa_ref[...].astype(BF16), preferred_element_type=F32)
    rn = jnp.dot(n_ref[...], wb_ref[...].astype(BF16), preferred_element_type=F32)
    o_ref[...] = (_sigmoid(ga_ref[...].astype(F32)) * ra + _sigmoid(gb_ref[...].astype(F32)) * rn).astype(o_ref.dtype)


def _merge(ret_in, na_in, proj2d, w_pa, w_pb):
    t, d = ret_in.shape
    tm, tn = MERGE_TM, MERGE_TN
    ga_off = (proj2d.shape[1] - 2 * d) // tn
    gb_off = (proj2d.shape[1] - d) // tn
    return pl.pallas_call(
        _merge_kernel,
        out_shape=jax.ShapeDtypeStruct((t, d), BF16),
        grid=(t // tm, d // tn),
        in_specs=[
            pl.BlockSpec((tm, ret_in.shape[1]), lambda i, j: (i, 0)),
            pl.BlockSpec((tm, na_in.shape[1]), lambda i, j: (i, 0)),
            pl.BlockSpec((tm, tn), lambda i, j: (i, ga_off + j)),
            pl.BlockSpec((tm, tn), lambda i, j: (i, gb_off + j)),
            pl.BlockSpec((w_pa.shape[0], tn), lambda i, j: (0, j)),
            pl.BlockSpec((w_pb.shape[0], tn), lambda i, j: (0, j)),
        ],
        out_specs=pl.BlockSpec((tm, tn), lambda i, j: (i, j)),
        compiler_params=_params(("arbitrary", "arbitrary"), 48),
        name="merge",
    )(ret_in, na_in, proj2d, proj2d, w_pa, w_pb)


def _oproj_kernel(m_ref, x_ref, g1_ref, ng_ref, sh_ref, sc_ref, wo_ref, wrh_ref, wrl_ref, br_ref,
                  x1_ref, h2_ref, lg_ref):
    y = jnp.dot(m_ref[...], wo_ref[...], preferred_element_type=F32)
    x1 = x_ref[0] + g1_ref[0] * y
    x1_ref[0] = x1
    ms = jnp.mean(x1 * x1, axis=-1, keepdims=True)
    h2 = x1 * lax.rsqrt(ms + NORM_EPS) * ng_ref[...]
    h2 = h2 * (1.0 + sc_ref[0]) + sh_ref[0]
    h2_ref[...] = h2
    h2_hi = h2.astype(BF16)
    h2_lo = (h2 - h2_hi.astype(F32)).astype(BF16)
    lg = jnp.dot(h2_hi, wrh_ref[...], preferred_element_type=F32)
    lg += jnp.dot(h2_lo, wrh_ref[...], preferred_element_type=F32)
    lg += jnp.dot(h2_hi, wrl_ref[...], preferred_element_type=F32)
    lg_ref[...] = lg + br_ref[...]


def _oproj(m, x, g1, norm_g, sh2, sc2, w_o, w_r, b_r):
    b, s, d = x.shape
    tm = OPROJ_TM
    spt = s // tm
    w_r_hi = w_r.astype(BF16)
    w_r_lo = (w_r - w_r_hi.astype(F32)).astype(BF16)
    return pl.pallas_call(
        _oproj_kernel,
        out_shape=(
            jax.ShapeDtypeStruct((b, s, d), F32),
            jax.ShapeDtypeStruct((b * s, d), F32),
            jax.ShapeDtypeStruct((b * s, LANES), F32),
        ),
        grid=(b, spt),
        in_specs=[
            pl.BlockSpec((tm, d), lambda bi, i: (bi * spt + i, 0)),
            pl.BlockSpec((1, tm, d), lambda bi, i: (bi, i, 0)),
            pl.BlockSpec((1, 1, d), lambda bi, i: (bi, 0, 0)),
            pl.BlockSpec((1, d), lambda bi, i: (0, 0)),
            pl.BlockSpec((1, 1, d), lambda bi, i: (bi, 0, 0)),
            pl.BlockSpec((1, 1, d), lambda bi, i: (bi, 0, 0)),
            pl.BlockSpec((d, d), lambda bi, i: (0, 0)),
            pl.BlockSpec((d, LANES), lambda bi, i: (0, 0)),
            pl.BlockSpec((d, LANES), lambda bi, i: (0, 0)),
            pl.BlockSpec((1, LANES), lambda bi, i: (0, 0)),
        ],
        out_specs=(
            pl.BlockSpec((1, tm, d), lambda bi, i: (bi, i, 0)),
            pl.BlockSpec((tm, d), lambda bi, i: (bi * spt + i, 0)),
            pl.BlockSpec((tm, LANES), lambda bi, i: (bi * spt + i, 0)),
        ),
        compiler_params=_params(("arbitrary", "arbitrary"), 48),
        name="oproj",
    )(m, x, g1, norm_g, sh2, sc2, w_o, w_r_hi, w_r_lo, b_r)


def _route_kernel(lg_ref, e_ref, w_ref, r_ref, cnt_ref, run_scr):
    i = pl.program_id(0)
    tm = lg_ref.shape[0]

    @pl.when(i == 0)
    def _():
        run_scr[...] = jnp.zeros_like(run_scr)

    l = lg_ref[...]
    lane = lax.broadcasted_iota(jnp.int32, l.shape, 1)
    vals, idxs, hots = [], [], []
    for _ in range(TOP_K):
        m = jnp.max(l, axis=-1, keepdims=True)
        idx = jnp.min(jnp.where(l == m, lane, LANES), axis=-1, keepdims=True)
        hot = lane == idx
        l = jnp.where(hot, -jnp.inf, l)
        vals.append(m)
        idxs.append(idx)
        hots.append(hot)
    exps = [jnp.exp(v - vals[0]) for v in vals]
    tot = exps[0]
    for e in exps[1:]:
        tot = tot + e

    member = hots[0]
    for hot in hots[1:]:
        member = member | hot
    member = member.astype(F32)
    ri = lax.broadcasted_iota(jnp.int32, (tm, tm), 0)
    ci = lax.broadcasted_iota(jnp.int32, (tm, tm), 1)
    lower = (ci < ri).astype(BF16)
    before = jnp.dot(lower, member.astype(BF16), preferred_element_type=F32) + run_scr[...]

    e_out = jnp.zeros(l.shape, jnp.int32)
    w_out = jnp.zeros(l.shape, F32)
    r_out = jnp.zeros(l.shape, jnp.int32)
    for k in range(TOP_K):
        rank = jnp.sum(jnp.where(hots[k], before, 0.0), axis=-1, keepdims=True).astype(jnp.int32)
        e_out = jnp.where(lane == k, idxs[k], e_out)
        w_out = jnp.where(lane == k, exps[k] / tot, w_out)
        r_out = jnp.where(lane == k, rank, r_out)
    e_ref[...] = e_out[:, :TOP_K]
    w_ref[...] = w_out
    r_ref[...] = r_out[:, :TOP_K]
    run_scr[...] += jnp.sum(member, axis=0, keepdims=True)
    cnt_ref[...] = run_scr[...].astype(jnp.int32)


def _route(logits):
    t = logits.shape[0]
    tm = ROUTE_TM
    row = pl.BlockSpec((tm, LANES), lambda i: (i, 0))
    narrow = pl.BlockSpec((tm, TOP_K), lambda i: (i, 0))
    return pl.pallas_call(
        _route_kernel,
        out_shape=(
            jax.ShapeDtypeStruct((t, TOP_K), jnp.int32),
            jax.ShapeDtypeStruct((t, LANES), F32),
            jax.ShapeDtypeStruct((t, TOP_K), jnp.int32),
            jax.ShapeDtypeStruct((1, LANES), jnp.int32),
        ),
        grid=(t // tm,),
        in_specs=[row],
        out_specs=(narrow, row, narrow, pl.BlockSpec((1, LANES), lambda i: (0, 0))),
        scratch_shapes=[pltpu.VMEM((1, LANES), F32)],
        compiler_params=_params(("arbitrary",), 32),
        name="route",
    )(logits)


def _dispatch_kernel(dest_ref, h_ref, xb_ref, sem):
    tm = h_ref.shape[0]

    for t in range(tm):
        for k in range(TOP_K):
            pltpu.make_async_copy(h_ref.at[pl.ds(t, 1)], xb_ref.at[pl.ds(dest_ref[TOP_K * t + k], 1)],
                                  sem).start(priority=k % 2)
    for k in range(TOP_K):
        pltpu.make_async_copy(h_ref, xb_ref.at[pl.ds(0, tm)], sem).wait()


def _dispatch(h2, dest_flat):
    t, d = h2.shape
    tm = DISPATCH_TM
    return pl.pallas_call(
        _dispatch_kernel,
        out_shape=jax.ShapeDtypeStruct((t * TOP_K, d), h2.dtype),
        grid=(t // tm,),
        in_specs=[
            pl.BlockSpec((tm * TOP_K,), lambda i: (i,), memory_space=pltpu.SMEM),
            pl.BlockSpec((tm, d), lambda i: (i, 0)),
        ],
        out_specs=pl.BlockSpec(memory_space=pl.ANY),
        scratch_shapes=[pltpu.SemaphoreType.DMA],
        compiler_params=_params(("arbitrary",), 32),
        name="dispatch",
    )(dest_flat, h2)


def _expert_kernel(tile_ref, exp_ref, lo_ref, hi_ref, first_ref,
                   x_hbm, wg_ref, bg_ref, wu_ref, bu_ref, wd_ref, bd_ref, o_ref, xs_scr, x_stage, x_sem):
    w = pl.program_id(0)
    j = pl.program_id(1)
    lo = lo_ref[w]
    hi = hi_ref[w]
    sub = EXPERT_SUB
    tm = o_ref.shape[0]
    n_tiles = x_hbm.shape[0] // tm

    def tile_copy(t):
        return pltpu.make_async_copy(x_hbm.at[pl.ds(pl.multiple_of(t * tm, tm), tm)], x_stage, x_sem)

    @pl.when((j == 0) & (first_ref[w] == 1))
    def _():
        t = tile_ref[w]

        @pl.when(w == 0)
        def _():
            tile_copy(t).start()

        tile_copy(t).wait()
        xs_scr[...] = x_stage[...].astype(BF16)

        @pl.when(t + 1 < n_tiles)
        def _():
            tile_copy(t + 1).start()

        o_ref[...] = jnp.zeros_like(o_ref)

    def ffn_tile(s, n=1):
        rows = slice(s * sub, (s + n) * sub)
        xs = xs_scr[rows, :]
        gate = jnp.dot(xs, wg_ref[0].astype(BF16), preferred_element_type=F32) + bg_ref[0]
        up = jnp.dot(xs, wu_ref[0].astype(BF16), preferred_element_type=F32) + bu_ref[0]
        gate = jnp.minimum(gate, SWIGLU_LIMIT)
        up = jnp.clip(up, -SWIGLU_LIMIT, SWIGLU_LIMIT)
        act = gate * _sigmoid(SWIGLU_ALPHA * gate) * (up + 1.0)
        row = lax.broadcasted_iota(jnp.int32, (n * sub, 1), 0) + s * sub
        mine = (row >= lo) & (row < hi)
        act = jnp.where(mine, act, 0.0).astype(BF16)
        y = jnp.dot(act, wd_ref[0].astype(BF16), preferred_element_type=F32)
        o_ref[rows, :] += y + jnp.where(mine & (j == 0), bd_ref[0], 0.0)

    n_sub = tm // sub
    active = [(lo < (s + 1) * sub) & (hi > s * sub) for s in range(n_sub)]
    whole = functools.reduce(jnp.logical_and, active)

    @pl.when(whole)
    def _():
        ffn_tile(0, n_sub)

    partial = jnp.logical_not(whole)
    for p in range(n_sub // 2):
        s0, s1 = 2 * p, 2 * p + 1

        @pl.when(partial & active[s0] & active[s1])
        def _(s0=s0):
            ffn_tile(s0, 2)

        @pl.when(partial & active[s0] & jnp.logical_not(active[s1]))
        def _(s0=s0):
            ffn_tile(s0)

        @pl.when(partial & active[s1] & jnp.logical_not(active[s0]))
        def _(s1=s1):
            ffn_tile(s1)


def _experts(xb, items, w_gate, b_gate, w_up, b_up, w_down, b_down):
    r = xb.shape[0]
    e, d, f = w_gate.shape
    tm, tf = EXPERT_TM, EXPERT_TF
    nj = f // tf
    tile, expert, lo, hi, first = items
    n_items = tile.shape[0]

    def jj(j, w, hi_ref, lo_ref):
        return jnp.where(hi_ref[w] > lo_ref[w], j, nj - 1)

    return pl.pallas_call(
        _expert_kernel,
        out_shape=jax.ShapeDtypeStruct((r, d), F32),
        grid_spec=pltpu.PrefetchScalarGridSpec(
            num_scalar_prefetch=5,
            grid=(n_items, nj),
            in_specs=[
                pl.BlockSpec(memory_space=pl.ANY),
                pl.BlockSpec((1, d, tf), lambda w, j, ti, ex, lo_, hi_, fi: (ex[w], 0, jj(j, w, hi_, lo_))),
                pl.BlockSpec((1, 1, tf), lambda w, j, ti, ex, lo_, hi_, fi: (ex[w], 0, jj(j, w, hi_, lo_))),
                pl.BlockSpec((1, d, tf), lambda w, j, ti, ex, lo_, hi_, fi: (ex[w], 0, jj(j, w, hi_, lo_))),
                pl.BlockSpec((1, 1, tf), lambda w, j, ti, ex, lo_, hi_, fi: (ex[w], 0, jj(j, w, hi_, lo_))),
                pl.BlockSpec((1, tf, d), lambda w, j, ti, ex, lo_, hi_, fi: (ex[w], jj(j, w, hi_, lo_), 0)),
                pl.BlockSpec((1, 1, d), lambda w, j, ti, ex, lo_, hi_, fi: (ex[w], 0, 0)),
            ],
            out_specs=pl.BlockSpec((tm, d), lambda w, j, ti, ex, lo_, hi_, fi: (ti[w], 0)),
            scratch_shapes=[pltpu.VMEM((tm, d), BF16), pltpu.VMEM((tm, d), F32), pltpu.SemaphoreType.DMA],
        ),
        compiler_params=_params(("arbitrary", "arbitrary"), 60),
        name="experts",
    )(tile, expert, lo, hi, first, xb, w_gate, b_gate.reshape(e, 1, f), w_up, b_up.reshape(e, 1, f),
      w_down, b_down.reshape(e, 1, d))


def _work_items(counts, n_rows):
    tm = EXPERT_TM
    n_tiles = n_rows // tm
    n_items = n_tiles + N_EXPERTS - 1
    def count_le(table, x):
        return jnp.sum((table[None, :] <= x[:, None]).astype(jnp.int32), axis=1)

    def lookup(table, idx):
        hit = idx[:, None] == jnp.arange(table.shape[0], dtype=jnp.int32)[None, :]
        return jnp.sum(jnp.where(hit, table[None, :], 0), axis=1)

    counts = counts.astype(jnp.int32)
    cum = jnp.cumsum(counts)
    start = cum - counts
    tile_lo = jnp.arange(n_tiles, dtype=jnp.int32) * tm
    e_lo = count_le(cum, tile_lo)
    e_hi = count_le(cum, tile_lo + tm - 1)
    per_tile = e_hi - e_lo + 1
    off = jnp.cumsum(per_tile) - per_tile
    total = jnp.sum(per_tile)
    w = jnp.arange(n_items, dtype=jnp.int32)
    valid = w < total
    tile = jnp.clip(count_le(off, w) - 1, 0, n_tiles - 1)
    off_t = lookup(off, tile)
    expert = jnp.where(valid, lookup(e_lo, tile) + w - off_t, e_hi[n_tiles - 1])
    tile = jnp.where(valid, tile, n_tiles - 1)
    lo = jnp.clip(lookup(start, expert) - tile * tm, 0, tm)
    hi = jnp.clip(lookup(cum, expert) - tile * tm, 0, tm)
    hi = jnp.where(valid, jnp.maximum(hi, lo), lo)
    first = (valid & (w == off_t)).astype(jnp.int32)
    return tile, expert.astype(jnp.int32), lo.astype(jnp.int32), hi.astype(jnp.int32), first, start


def _combine_kernel(dest_ref, dnext_ref, dnext2_ref, yb_ref, w_ref, x1_ref, g2_ref, fg_ref, o_ref, buf, sem):
    i = pl.program_id(0)
    n = pl.num_programs(0)
    tm = x1_ref.shape[0]
    slot = i % COMBINE_SLOTS

    def gather(d_ref, into):
        for t in range(tm):
            for k in range(TOP_K):
                pltpu.make_async_copy(yb_ref.at[pl.ds(d_ref[TOP_K * t + k], 1)], buf.at[into, k, pl.ds(t, 1)],
                                      sem.at[into]).start(priority=k % 2)

    def reduce(src):
        wts = w_ref[...]
        moe = buf[src, 0] * wts[:, 0:1]
        for k in range(1, TOP_K):
            moe += buf[src, k] * wts[:, k:k + 1]
        x2 = x1_ref[...] + g2_ref[0] * moe
        ms = jnp.mean(x2 * x2, axis=-1, keepdims=True)
        o_ref[...] = x2 * lax.rsqrt(ms + NORM_EPS) * fg_ref[...]

    @pl.when(i == 0)
    def _():
        gather(dest_ref, 0)
        gather(dnext_ref, 1)

    for k in range(TOP_K):
        pltpu.make_async_copy(yb_ref.at[pl.ds(0, tm)], buf.at[slot, k], sem.at[slot]).wait()

    for v in range(COMBINE_SLOTS):
        @pl.when((slot == v) & (i + 2 < n))
        def _(v=v):
            gather(dnext2_ref, (v + 2) % COMBINE_SLOTS)
            reduce(v)

    @pl.when(i + 2 >= n)
    def _():
        reduce(slot)


def _combine(yb, dest_flat, top_w, x1, g2, final_g):
    b, s, d = x1.shape
    tm = COMBINE_TM
    spt = s // tm
    n = b * spt
    assert n >= COMBINE_SLOTS
    out = pl.pallas_call(
        _combine_kernel,
        out_shape=jax.ShapeDtypeStruct((b * s, d), F32),
        grid=(n,),
        in_specs=[
            pl.BlockSpec((tm * TOP_K,), lambda i: (i,), memory_space=pltpu.SMEM),
            pl.BlockSpec((tm * TOP_K,), lambda i: (jnp.minimum(i + 1, n - 1),), memory_space=pltpu.SMEM),
            pl.BlockSpec((tm * TOP_K,), lambda i: (jnp.minimum(i + 2, n - 1),), memory_space=pltpu.SMEM),
            pl.BlockSpec(memory_space=pl.ANY),
            pl.BlockSpec((tm, LANES), lambda i: (i, 0)),
            pl.BlockSpec((tm, d), lambda i: (i, 0)),
            pl.BlockSpec((1, 1, d), lambda i: (i // spt, 0, 0)),
            pl.BlockSpec((1, d), lambda i: (0, 0)),
        ],
        out_specs=pl.BlockSpec((tm, d), lambda i: (i, 0)),
        scratch_shapes=[pltpu.VMEM((COMBINE_SLOTS, TOP_K, tm, d), F32), pltpu.SemaphoreType.DMA((COMBINE_SLOTS,))],
        compiler_params=_params(("arbitrary",), 56),
        name="combine",
    )(dest_flat, dest_flat, dest_flat, yb, top_w, x1.reshape(b * s, d), g2, final_g)
    return out.reshape(b, s, d)


def kernel(x, c, ctx, c_ctx, ada_w, ada_b, norm1_g, norm2_g, w_in, w_pa, w_pb, w_o, ret_decay_fwd, ret_decay_bwd,
           na_rpb, w_router, b_router, w_gate, b_gate, w_up, b_up, w_down, b_down, final_g):
    assert ada_w.shape[0] == 1, "single layer"
    b, s, d = x.shape
    l = ctx.shape[1]
    in_w = w_in.shape[2]
    rows = s // GRID_W
    assert s % IN_TM == 0 and l % RET_CHUNK == 0 and rows >= NA_WROWS and rows % NA_QROWS == 0
    assert s % OPROJ_TM == 0 and s % COMBINE_TM == 0
    assert (b * s) % max(MERGE_TM, ROUTE_TM, DISPATCH_TM) == 0 and (b * s * TOP_K) % EXPERT_TM == 0

    c_rows = jnp.zeros((16, d), F32).at[:b].set(c).at[b].set(c_ctx)
    mod = _ada(c_rows, ada_w[0], ada_b[0][None, :])
    sh1, sc1, g1, sh2, sc2, g2 = [mod[:b, None, i * d:(i + 1) * d] for i in range(6)]
    shc1 = mod[b, 0 * d:1 * d][None, None, :]
    scc1 = mod[b, 1 * d:2 * d][None, None, :]

    w_in0 = w_in[0]
    n1 = norm1_g[0][None, :]
    proj = _inproj(x, n1, sh1, sc1, w_in0, tuple(range(in_w // IN_TN)), IN_TM)
    qk_w = RET_HEADS * RET_DK
    v_w = RET_HEADS * RET_DV
    na_w = NA_HEADS * NA_DH
    ctx_cols = tuple(range(qk_w // IN_TN, (2 * qk_w + v_w) // IN_TN)) + tuple(
        range((2 * qk_w + 2 * v_w + na_w) // IN_TN, (2 * qk_w + 2 * v_w + 3 * na_w) // IN_TN))
    projc = _inproj(ctx.reshape(1, b * l, d), n1, shc1, scc1, w_in0, ctx_cols, b * l).reshape(b, l, -1)

    lgf = jax.nn.log_sigmoid(ret_decay_fwd[0].astype(F32))
    lgb = jax.nn.log_sigmoid(ret_decay_bwd[0].astype(F32))
    cos_t, sin_t = _rope_tables(s)
    ret_in = _retention(proj, projc, lgf, lgb, cos_t, sin_t)
    na_in = _natt(proj, projc, _natt_pair_tiles(na_rpb[0]))

    t = b * s
    m = _merge(ret_in.reshape(t, -1), na_in.reshape(t, -1), proj.reshape(t, in_w),
               w_pa[0], w_pb[0])

    w_r = jnp.zeros((d, LANES), F32).at[:, :N_EXPERTS].set(w_router[0])
    b_r = jnp.full((1, LANES), NEG_INF, F32).at[0, :N_EXPERTS].set(b_router[0])
    x1, h2, logits = _oproj(m, x, g1, norm2_g[0][None, :], sh2, sc2, w_o[0].astype(BF16), w_r, b_r)

    top_e, top_w, rank, counts = _route(logits)
    counts = counts[0, :N_EXPERTS]
    items = _work_items(counts, t * TOP_K)
    start = items[5]
    is_e = top_e[:, :, None] == jnp.arange(N_EXPERTS, dtype=jnp.int32)
    dest = (jnp.sum(jnp.where(is_e, start.astype(jnp.int32), 0), axis=-1) + rank).reshape(-1)

    xb = _dispatch(h2, dest)
    yb = _experts(xb, items[:5], w_gate[0], b_gate[0], w_up[0], b_up[0], w_down[0], b_down[0])
    return _combine(yb, dest, top_w, x1, g2, final_g[None, :])
```

```python
import functools

import jax
import jax.numpy as jnp
import numpy as np
from jax import lax
from jax.experimental import pallas as pl
from jax.experimental.pallas import tpu as pltpu

F32 = jnp.float32
BF16 = jnp.bfloat16

GRID_W = 64
RET_HEADS = 8
RET_DK = 128
RET_DV = 256
RET_CHUNK = 256
ROPE_AXIS_DIM = RET_DK // 2
ROPE_BASE = 10000.0
NA_HEADS = 16
NA_DH = 128
NA_KR = 8
NA_KC = 16
N_EXPERTS = 32
TOP_K = 4
SWIGLU_ALPHA = 1.702
SWIGLU_LIMIT = 7.0
NORM_EPS = 1e-6
GN_EPS = 1e-5
NEG_INF = -1e30
LOG2_E = 1.4426950408889634

LANES = 128
MIB = 1024 * 1024

NA_QROWS = 4
NA_WROWS = NA_QROWS + NA_KR

IN_TM, IN_TN = 1024, 1024
MERGE_TM, MERGE_TN = 1024, 512
OPROJ_TM = 512
ROUTE_TM = 512
DISPATCH_TM = 512
EXPERT_TM, EXPERT_SUB, EXPERT_TF = 1024, 256, 512
COMBINE_TM = 128
COMBINE_SLOTS = 3


def _params(semantics, vmem_mib):
    return pltpu.CompilerParams(dimension_semantics=semantics, vmem_limit_bytes=vmem_mib * MIB)


def _sigmoid(x):
    return 1.0 / (1.0 + jnp.exp(-x))


def _ada_kernel(c_ref, w_ref, b_ref, o_ref):
    c = c_ref[...]
    s = c * _sigmoid(c)
    o_ref[...] = jnp.dot(s.astype(BF16), w_ref[...].astype(BF16), preferred_element_type=F32) + b_ref[...]


def _ada(c_rows, w, b):
    r, d = c_rows.shape
    n = w.shape[1]
    tn = 1024
    return pl.pallas_call(
        _ada_kernel,
        out_shape=jax.ShapeDtypeStruct((r, n), F32),
        grid=(n // tn,),
        in_specs=[
            pl.BlockSpec((r, d), lambda j: (0, 0)),
            pl.BlockSpec((d, tn), lambda j: (0, j)),
            pl.BlockSpec((1, tn), lambda j: (0, j)),
        ],
        out_specs=pl.BlockSpec((r, tn), lambda j: (0, j)),
        compiler_params=_params(("arbitrary",), 40),
        name="ada",
    )(c_rows, w, b)


def _inproj_kernel(cols_ref, x_ref, g_ref, sh_ref, sc_ref, w_ref, o_ref, h_scr):
    @pl.when(pl.program_id(2) == 0)
    def _():
        x = x_ref[0]
        ms = jnp.mean(x * x, axis=-1, keepdims=True)
        y = x * lax.rsqrt(ms + NORM_EPS) * g_ref[...]
        h_scr[...] = (y * (1.0 + sc_ref[0]) + sh_ref[0]).astype(BF16)

    o_ref[0] = jnp.dot(h_scr[...], w_ref[...].astype(BF16), preferred_element_type=F32).astype(o_ref.dtype)


def _inproj(x, g, shift, scale, w, col_tiles, tm):
    b, n, d = x.shape
    tn = IN_TN
    nct = len(col_tiles)
    cols = jnp.asarray(np.asarray(col_tiles, np.int32))
    return pl.pallas_call(
        _inproj_kernel,
        out_shape=jax.ShapeDtypeStruct((b, n, nct * tn), BF16),
        grid_spec=pltpu.PrefetchScalarGridSpec(
            num_scalar_prefetch=1,
            grid=(b, n // tm, nct),
            in_specs=[
                pl.BlockSpec((1, tm, d), lambda bi, i, j, c: (bi, i, 0)),
                pl.BlockSpec((1, d), lambda bi, i, j, c: (0, 0)),
                pl.BlockSpec((1, 1, d), lambda bi, i, j, c: (bi, 0, 0)),
                pl.BlockSpec((1, 1, d), lambda bi, i, j, c: (bi, 0, 0)),
                pl.BlockSpec((d, tn), lambda bi, i, j, c: (0, c[j])),
            ],
            out_specs=pl.BlockSpec((1, tm, tn), lambda bi, i, j, c: (bi, i, j)),
            scratch_shapes=[pltpu.VMEM((tm, d), BF16)],
        ),
        compiler_params=_params(("arbitrary", "arbitrary", "arbitrary"), 56),
        name="inproj",
    )(cols, x, g, shift, scale, w)


def _swap_halves(x):
    half = ROPE_AXIS_DIM // 2
    lane = lax.broadcasted_iota(jnp.int32, x.shape, 1)
    return jnp.where(lane % ROPE_AXIS_DIM < half, pltpu.roll(x, RET_DK - half, 1), pltpu.roll(x, half, 1))


def _ret_kernel(lgf_ref, lgb_ref, q_ref, k_ref, v_ref, g_ref, kc_ref, vc_ref, cos_ref, sin_ref,
                o_ref, ks_scr, rf_scr, rb_scr, ub_scr):
    h = pl.program_id(1)
    c = RET_CHUNK
    n = q_ref.shape[1]
    nc = n // c
    ncc = kc_ref.shape[1] // c
    lgf = lgf_ref[h]
    lgb = lgb_ref[h]
    k_scale = RET_DK ** -0.5

    pos_c = lax.broadcasted_iota(jnp.int32, (c, 1), 0).astype(F32)
    zeta_f = jnp.exp((c - 1.0 - pos_c) * lgf)
    zeta_b = jnp.exp(pos_c * lgb)
    xi_f = jnp.exp((pos_c + 1.0) * lgf)
    xi_b = jnp.exp((c - pos_c) * lgb)
    one = jnp.ones((1, 1), F32)
    gc_f = jnp.exp(one * (c * lgf))
    gc_b = jnp.exp(one * (c * lgb))
    ii = lax.broadcasted_iota(jnp.int32, (c, c), 0)
    jj = lax.broadcasted_iota(jnp.int32, (c, c), 1)
    diff = (ii - jj).astype(F32)
    dmask = (jnp.where(diff >= 0, jnp.exp(jnp.maximum(diff, 0.0) * lgf), 0.0)
             + jnp.where(diff <= 0, jnp.exp(jnp.maximum(-diff, 0.0) * lgb), 0.0))

    def ktv(k_bf, v_f32, zeta):
        return jnp.dot(k_bf.astype(F32).T.astype(BF16), (v_f32 * zeta).astype(BF16), preferred_element_type=F32)

    r_f = jnp.zeros((RET_DK, RET_DV), F32)
    for i in range(ncc):
        kc = (kc_ref[0, i * c:(i + 1) * c, :].astype(F32) * k_scale).astype(BF16)
        r_f = gc_f * r_f + ktv(kc, vc_ref[0, i * c:(i + 1) * c, :].astype(F32), zeta_f)
    r_b = jnp.zeros((RET_DK, RET_DV), F32)
    for i in reversed(range(ncc)):
        kc = (kc_ref[0, i * c:(i + 1) * c, :].astype(F32) * k_scale).astype(BF16)
        r_b = gc_b * r_b + ktv(kc, vc_ref[0, i * c:(i + 1) * c, :].astype(F32), zeta_b)

    def rope(x, rows):
        x = x.astype(F32)
        return x * cos_ref[rows, :] + _swap_halves(x) * sin_ref[rows, :]

    def chunk_updates(i, r):
        rows = pl.ds(pl.multiple_of(i * c, c), c)
        kb = (rope(k_ref[0, rows, :], rows) * k_scale).astype(BF16)
        ks_scr[rows, :] = kb
        kt = kb.astype(F32).T.astype(BF16)
        v = v_ref[0, rows, :].astype(F32)
        rf_scr[i] = r.astype(BF16)
        ub_scr[i] = jnp.dot(kt, (v * zeta_b).astype(BF16), preferred_element_type=F32)
        return gc_f * r + jnp.dot(kt, (v * zeta_f).astype(BF16), preferred_element_type=F32)

    lax.fori_loop(0, nc, chunk_updates, r_f, unroll=8)

    def bwd_scan(t, r):
        i = nc - 1 - t
        rb_scr[i] = r.astype(BF16)
        return gc_b * r + ub_scr[i]

    lax.fori_loop(0, nc, bwd_scan, r_b, unroll=2)

    def out_chunk(i, carry):
        rows = pl.ds(pl.multiple_of(i * c, c), c)
        q = rope(q_ref[0, rows, :], rows)
        kb = ks_scr[rows, :]
        vb = v_ref[0, rows, :]
        s = lax.dot_general(q.astype(BF16), kb, (((1,), (1,)), ((), ())), preferred_element_type=F32) * dmask
        o = jnp.dot(s.astype(BF16), vb, preferred_element_type=F32)
        o += jnp.dot((q * xi_f).astype(BF16), rf_scr[i], preferred_element_type=F32)
        o += jnp.dot((q * xi_b).astype(BF16), rb_scr[i], preferred_element_type=F32)
        mu = jnp.mean(o, axis=-1, keepdims=True)
        d = o - mu
        var = jnp.mean(d * d, axis=-1, keepdims=True)
        on = d * lax.rsqrt(var + GN_EPS)
        g = g_ref[0, rows, :].astype(F32)
        o_ref[0, rows, :] = (g * _sigmoid(g) * on).astype(o_ref.dtype)
        return carry

    lax.fori_loop(0, nc, out_chunk, 0, unroll=8)


def _retention(proj, projc, lgf, lgb, cos_t, sin_t):
    b, s, _ = proj.shape
    l = projc.shape[1]
    hq = RET_HEADS
    v_off = 2 * hq * RET_DK // RET_DV
    g_off = v_off + hq
    cv_off = hq * RET_DK // RET_DV
    smem = pl.BlockSpec(memory_space=pltpu.SMEM)
    return pl.pallas_call(
        _ret_kernel,
        out_shape=jax.ShapeDtypeStruct((b, s, hq * RET_DV), BF16),
        grid=(b, hq),
        in_specs=[
            smem, smem,
            pl.BlockSpec((1, s, RET_DK), lambda bi, h: (bi, 0, h)),
            pl.BlockSpec((1, s, RET_DK), lambda bi, h: (bi, 0, hq + h)),
            pl.BlockSpec((1, s, RET_DV), lambda bi, h: (bi, 0, v_off + h)),
            pl.BlockSpec((1, s, RET_DV), lambda bi, h: (bi, 0, g_off + h)),
            pl.BlockSpec((1, l, RET_DK), lambda bi, h: (bi, 0, h)),
            pl.BlockSpec((1, l, RET_DV), lambda bi, h: (bi, 0, cv_off + h)),
            pl.BlockSpec((s, RET_DK), lambda bi, h: (0, 0)),
            pl.BlockSpec((s, RET_DK), lambda bi, h: (0, 0)),
        ],
        out_specs=pl.BlockSpec((1, s, RET_DV), lambda bi, h: (bi, 0, h)),
        scratch_shapes=[
            pltpu.VMEM((s, RET_DK), BF16),
            pltpu.VMEM((s // RET_CHUNK, RET_DK, RET_DV), BF16),
            pltpu.VMEM((s // RET_CHUNK, RET_DK, RET_DV), BF16),
            pltpu.VMEM((s // RET_CHUNK, RET_DK, RET_DV), F32),
        ],
        compiler_params=_params(("arbitrary", "arbitrary"), 48),
        name="ret",
    )(lgf, lgb, proj, proj, proj, proj, projc, projc, cos_t, sin_t)


def _rope_tables(n):
    rows = n // GRID_W
    inv = ROPE_BASE ** (-jnp.arange(0, ROPE_AXIS_DIM, 2, dtype=F32) / ROPE_AXIS_DIM)
    ar = jnp.arange(rows, dtype=F32)[:, None] * inv
    ac = jnp.arange(GRID_W, dtype=F32)[:, None] * inv
    f = inv.shape[0]

    def per_token(row_part, col_part):
        row_part = jnp.broadcast_to(row_part[:, None, :], (rows, GRID_W, 2 * f))
        col_part = jnp.broadcast_to(col_part[None, :, :], (rows, GRID_W, 2 * f))
        return jnp.concatenate([row_part, col_part], axis=2).reshape(n, 4 * f)

    cos_t = per_token(jnp.concatenate([jnp.cos(ar), jnp.cos(ar)], axis=1),
                      jnp.concatenate([jnp.cos(ac), jnp.cos(ac)], axis=1))
    sin_t = per_token(jnp.concatenate([-jnp.sin(ar), jnp.sin(ar)], axis=1),
                      jnp.concatenate([-jnp.sin(ac), jnp.sin(ac)], axis=1))
    return cos_t, sin_t


def _natt_kernel(q_ref, k_ref, v_ref, kc_ref, vc_ref, pair_ref, o_ref, bias_scr, v1_scr, vc1_scr):
    @pl.when(pl.program_id(1) == 0)
    def _():
        _natt_fill_bias(pair_ref, bias_scr)

    n = q_ref.shape[1]
    rows = n // GRID_W
    n_blk = rows // NA_QROWS
    nq = NA_QROWS * GRID_W
    nk = NA_WROWS * GRID_W
    scale = NA_DH ** -0.5
    kc = kc_ref[0]
    nt = (((1,), (1,)), ((), ()))
    v1_scr[:, :NA_DH] = v_ref[0]
    v1_scr[:, NA_DH:] = jnp.ones((n, NA_DH), BF16)
    vc1_scr[:, :NA_DH] = vc_ref[0]
    vc1_scr[:, NA_DH:] = jnp.ones((vc_ref.shape[1], NA_DH), BF16)

    def block(blk, carry):
        ws = jnp.clip(blk * NA_QROWS - NA_KR // 2, 0, rows - NA_WROWS)
        variant = jnp.where(blk == 0, 0, jnp.where(blk == n_blk - 1, 2, 1))
        qrows = pl.ds(pl.multiple_of(blk * nq, nq), nq)
        krows = pl.ds(pl.multiple_of(ws * GRID_W, GRID_W), nk)
        q = q_ref[0, qrows, :]
        s_loc = lax.dot_general(q, k_ref[0, krows, :], nt, preferred_element_type=F32) + bias_scr[variant]
        s_ctx = lax.dot_general(q, kc, nt, preferred_element_type=F32)
        m = jnp.maximum(jnp.max(s_loc, axis=-1, keepdims=True), jnp.max(s_ctx, axis=-1, keepdims=True))
        p_loc = jnp.exp2((s_loc - m) * (scale * LOG2_E))
        p_ctx = jnp.exp2((s_ctx - m) * (scale * LOG2_E))
        o = jnp.dot(p_loc.astype(BF16), v1_scr[krows, :], preferred_element_type=F32)
        o += jnp.dot(p_ctx.astype(BF16), vc1_scr[...], preferred_element_type=F32)
        o_ref[0, qrows, :] = (o[:, :NA_DH] / o[:, NA_DH:]).astype(o_ref.dtype)
        return carry

    lax.fori_loop(0, n_blk, block, 0, unroll=4)


def _natt_row_offsets():
    rq = np.arange(NA_QROWS)[:, None]
    wr = np.arange(NA_WROWS)[None, :]
    dr0 = np.where(wr < NA_KR, wr - rq + NA_KR - 1, -1)
    dr1 = np.where((wr >= rq) & (wr < rq + NA_KR), wr - rq + NA_KR // 2 - 1, -1)
    dr2 = np.where(wr >= NA_QROWS, wr - rq - 1, -1)
    return np.stack([dr0, dr1, dr2])


def _natt_fill_bias(pair_ref, bias_scr):
    w = GRID_W
    dr = _natt_row_offsets()
    neg = jnp.full((w, 2 * w), NEG_INF, F32)
    lane_row = lax.broadcasted_iota(jnp.int32, (w, NA_WROWS * w), 1) // w
    for kind in range(dr.shape[0]):
        for rq in range(NA_QROWS):
            tiles = []
            for p in range(NA_WROWS // 2):
                a, b_ = int(dr[kind, rq, 2 * p]), int(dr[kind, rq, 2 * p + 1])
                if a < 0 and b_ < 0:
                    tiles.append(neg)
                else:
                    tiles.append(pair_ref[0, b_ if b_ >= 0 else a + 1])
            strip = jnp.concatenate(tiles, axis=1)
            valid = np.nonzero(dr[kind, rq] >= 0)[0]
            keep = (lane_row >= int(valid[0])) & (lane_row <= int(valid[-1]))
            bias_scr[kind, rq * w:(rq + 1) * w, :] = jnp.where(keep, strip, NEG_INF)


def _natt_pair_tiles(rpb):
    w = GRID_W
    qc = np.arange(w)[:, None]
    kcol = np.arange(w)[None, :]
    cs = np.clip(qc - NA_KC // 2, 0, w - NA_KC)
    col_ok = (kcol >= cs) & (kcol < cs + NA_KC)
    dc_idx = np.clip(kcol - qc, -(NA_KC - 1), NA_KC - 1) + NA_KC - 1
    pick = (dc_idx.reshape(1, -1) == np.arange(2 * NA_KC - 1)[:, None]).astype(np.float32)
    nh, ndr, ndc = rpb.shape
    t = jnp.dot(rpb.astype(F32).reshape(nh * ndr, ndc), jnp.asarray(pick), precision=lax.Precision.HIGHEST)
    t = jnp.where(col_ok[None, None], t.reshape(nh, ndr, w, w) * NA_DH ** 0.5, NEG_INF)
    neg = jnp.full((rpb.shape[0], 1, w, w), NEG_INF, F32)
    return jnp.concatenate([jnp.concatenate([neg, t], axis=1), jnp.concatenate([t, neg], axis=1)], axis=3)


def _natt(proj, projc, pairs):
    b, s, _ = proj.shape
    l = projc.shape[1]
    nh = NA_HEADS
    q_off = (2 * RET_HEADS * RET_DK + 2 * RET_HEADS * RET_DV) // NA_DH
    k_off = q_off + nh
    v_off = k_off + nh
    ck_off = (RET_HEADS * RET_DK + RET_HEADS * RET_DV) // NA_DH
    cv_off = ck_off + nh
    nq = NA_QROWS * GRID_W
    nk = NA_WROWS * GRID_W
    return pl.pallas_call(
        _natt_kernel,
        out_shape=jax.ShapeDtypeStruct((b, s, nh * NA_DH), BF16),
        grid=(nh, b),
        in_specs=[
            pl.BlockSpec((1, s, NA_DH), lambda h, bi: (bi, 0, q_off + h)),
            pl.BlockSpec((1, s, NA_DH), lambda h, bi: (bi, 0, k_off + h)),
            pl.BlockSpec((1, s, NA_DH), lambda h, bi: (bi, 0, v_off + h)),
            pl.BlockSpec((1, l, NA_DH), lambda h, bi: (bi, 0, ck_off + h)),
            pl.BlockSpec((1, l, NA_DH), lambda h, bi: (bi, 0, cv_off + h)),
            pl.BlockSpec((1, 2 * NA_KR, GRID_W, 2 * GRID_W), lambda h, bi: (h, 0, 0, 0)),
        ],
        out_specs=pl.BlockSpec((1, s, NA_DH), lambda h, bi: (bi, 0, h)),
        scratch_shapes=[pltpu.VMEM((3, nq, nk), F32), pltpu.VMEM((s, 2 * NA_DH), BF16),
                        pltpu.VMEM((l, 2 * NA_DH), BF16)],
        compiler_params=_params(("arbitrary", "arbitrary"), 48),
        name="natt",
    )(proj, proj, proj, projc, projc, pairs)


def _merge_kernel(a_ref, n_ref, ga_ref, gb_ref, wa_ref, wb_ref, o_ref):
    ra = jnp.dot(a_ref[...], wa_ref[...].astype(BF16), preferred_element_type=F32)
    rn = jnp.dot(n_ref[...], wb_ref[...].astype(BF16), preferred_element_type=F32)
    o_ref[...] = (_sigmoid(ga_ref[...].astype(F32)) * ra + _sigmoid(gb_ref[...].astype(F32)) * rn).astype(o_ref.dtype)


def _merge(ret_in, na_in, proj2d, w_pa, w_pb):
    t, d = ret_in.shape
    tm, tn = MERGE_TM, MERGE_TN
    ga_off = (proj2d.shape[1] - 2 * d) // tn
    gb_off = (proj2d.shape[1] - d) // tn
    return pl.pallas_call(
        _merge_kernel,
        out_shape=jax.ShapeDtypeStruct((t, d), BF16),
        grid=(t // tm, d // tn),
        in_specs=[
            pl.BlockSpec((tm, ret_in.shape[1]), lambda i, j: (i, 0)),
            pl.BlockSpec((tm, na_in.shape[1]), lambda i, j: (i, 0)),
            pl.BlockSpec((tm, tn), lambda i, j: (i, ga_off + j)),
            pl.BlockSpec((tm, tn), lambda i, j: (i, gb_off + j)),
            pl.BlockSpec((w_pa.shape[0], tn), lambda i, j: (0, j)),
            pl.BlockSpec((w_pb.shape[0], tn), lambda i, j: (0, j)),
        ],
        out_specs=pl.BlockSpec((tm, tn), lambda i, j: (i, j)),
        compiler_params=_params(("arbitrary", "arbitrary"), 48),
        name="merge",
    )(ret_in, na_in, proj2d, proj2d, w_pa, w_pb)


def _oproj_kernel(m_ref, x_ref, g1_ref, ng_ref, sh_ref, sc_ref, wo_ref, wrh_ref, wrl_ref, br_ref,
                  x1_ref, h2_ref, lg_ref):
    y = jnp.dot(m_ref[...], wo_ref[...], preferred_element_type=F32)
    x1 = x_ref[0] + g1_ref[0] * y
    x1_ref[0] = x1
    ms = jnp.mean(x1 * x1, axis=-1, keepdims=True)
    h2 = x1 * lax.rsqrt(ms + NORM_EPS) * ng_ref[...]
    h2 = h2 * (1.0 + sc_ref[0]) + sh_ref[0]
    h2_ref[...] = h2
    h2_hi = h2.astype(BF16)
    h2_lo = (h2 - h2_hi.astype(F32)).astype(BF16)
    lg = jnp.dot(h2_hi, wrh_ref[...], preferred_element_type=F32)
    lg += jnp.dot(h2_lo, wrh_ref[...], preferred_element_type=F32)
    lg += jnp.dot(h2_hi, wrl_ref[...], preferred_element_type=F32)
    lg_ref[...] = lg + br_ref[...]


def _oproj(m, x, g1, norm_g, sh2, sc2, w_o, w_r, b_r):
    b, s, d = x.shape
    tm = OPROJ_TM
    spt = s // tm
    w_r_hi = w_r.astype(BF16)
    w_r_lo = (w_r - w_r_hi.astype(F32)).astype(BF16)
    return pl.pallas_call(
        _oproj_kernel,
        out_shape=(
            jax.ShapeDtypeStruct((b, s, d), F32),
            jax.ShapeDtypeStruct((b * s, d), F32),
            jax.ShapeDtypeStruct((b * s, LANES), F32),
        ),
        grid=(b, spt),
        in_specs=[
            pl.BlockSpec((tm, d), lambda bi, i: (bi * spt + i, 0)),
            pl.BlockSpec((1, tm, d), lambda bi, i: (bi, i, 0)),
            pl.BlockSpec((1, 1, d), lambda bi, i: (bi, 0, 0)),
            pl.BlockSpec((1, d), lambda bi, i: (0, 0)),
            pl.BlockSpec((1, 1, d), lambda bi, i: (bi, 0, 0)),
            pl.BlockSpec((1, 1, d), lambda bi, i: (bi, 0, 0)),
            pl.BlockSpec((d, d), lambda bi, i: (0, 0)),
            pl.BlockSpec((d, LANES), lambda bi, i: (0, 0)),
            pl.BlockSpec((d, LANES), lambda bi, i: (0, 0)),
            pl.BlockSpec((1, LANES), lambda bi, i: (0, 0)),
        ],
        out_specs=(
            pl.BlockSpec((1, tm, d), lambda bi, i: (bi, i, 0)),
            pl.BlockSpec((tm, d), lambda bi, i: (bi * spt + i, 0)),
            pl.BlockSpec((tm, LANES), lambda bi, i: (bi * spt + i, 0)),
        ),
        compiler_params=_params(("arbitrary", "arbitrary"), 48),
        name="oproj",
    )(m, x, g1, norm_g, sh2, sc2, w_o, w_r_hi, w_r_lo, b_r)


def _route_kernel(lg_ref, e_ref, w_ref, r_ref, cnt_ref, run_scr):
    i = pl.program_id(0)
    tm = lg_ref.shape[0]

    @pl.when(i == 0)
    def _():
        run_scr[...] = jnp.zeros_like(run_scr)

    l = lg_ref[...]
    lane = lax.broadcasted_iota(jnp.int32, l.shape, 1)
    vals, idxs, hots = [], [], []
    for _ in range(TOP_K):
        m = jnp.max(l, axis=-1, keepdims=True)
        idx = jnp.min(jnp.where(l == m, lane, LANES), axis=-1, keepdims=True)
        hot = lane == idx
        l = jnp.where(hot, -jnp.inf, l)
        vals.append(m)
        idxs.append(idx)
        hots.append(hot)
    exps = [jnp.exp(v - vals[0]) for v in vals]
    tot = exps[0]
    for e in exps[1:]:
        tot = tot + e

    member = hots[0]
    for hot in hots[1:]:
        member = member | hot
    member = member.astype(F32)
    ri = lax.broadcasted_iota(jnp.int32, (tm, tm), 0)
    ci = lax.broadcasted_iota(jnp.int32, (tm, tm), 1)
    lower = (ci < ri).astype(BF16)
    before = jnp.dot(lower, member.astype(BF16), preferred_element_type=F32) + run_scr[...]

    e_out = jnp.zeros(l.shape, jnp.int32)
    w_out = jnp.zeros(l.shape, F32)
    r_out = jnp.zeros(l.shape, jnp.int32)
    for k in range(TOP_K):
        rank = jnp.sum(jnp.where(hots[k], before, 0.0), axis=-1, keepdims=True).astype(jnp.int32)
        e_out = jnp.where(lane == k, idxs[k], e_out)
        w_out = jnp.where(lane == k, exps[k] / tot, w_out)
        r_out = jnp.where(lane == k, rank, r_out)
    e_ref[...] = e_out[:, :TOP_K]
    w_ref[...] = w_out
    r_ref[...] = r_out[:, :TOP_K]
    run_scr[...] += jnp.sum(member, axis=0, keepdims=True)
    cnt_ref[...] = run_scr[...].astype(jnp.int32)


def _route(logits):
    t = logits.shape[0]
    tm = ROUTE_TM
    row = pl.BlockSpec((tm, LANES), lambda i: (i, 0))
    narrow = pl.BlockSpec((tm, TOP_K), lambda i: (i, 0))
    return pl.pallas_call(
        _route_kernel,
        out_shape=(
            jax.ShapeDtypeStruct((t, TOP_K), jnp.int32),
            jax.ShapeDtypeStruct((t, LANES), F32),
            jax.ShapeDtypeStruct((t, TOP_K), jnp.int32),
            jax.ShapeDtypeStruct((1, LANES), jnp.int32),
        ),
        grid=(t // tm,),
        in_specs=[row],
        out_specs=(narrow, row, narrow, pl.BlockSpec((1, LANES), lambda i: (0, 0))),
        scratch_shapes=[pltpu.VMEM((1, LANES), F32)],
        compiler_params=_params(("arbitrary",), 32),
        name="route",
    )(logits)


def _dispatch_kernel(dest_ref, h_ref, xb_ref, sem):
    tm = h_ref.shape[0]

    for t in range(tm):
        for k in range(TOP_K):
            pltpu.make_async_copy(h_ref.at[pl.ds(t, 1)], xb_ref.at[pl.ds(dest_ref[TOP_K * t + k], 1)],
                                  sem).start(priority=k % 2)
    for k in range(TOP_K):
        pltpu.make_async_copy(h_ref, xb_ref.at[pl.ds(0, tm)], sem).wait()


def _dispatch(h2, dest_flat):
    t, d = h2.shape
    tm = DISPATCH_TM
    return pl.pallas_call(
        _dispatch_kernel,
        out_shape=jax.ShapeDtypeStruct((t * TOP_K, d), h2.dtype),
        grid=(t // tm,),
        in_specs=[
            pl.BlockSpec((tm * TOP_K,), lambda i: (i,), memory_space=pltpu.SMEM),
            pl.BlockSpec((tm, d), lambda i: (i, 0)),
        ],
        out_specs=pl.BlockSpec(memory_space=pl.ANY),
        scratch_shapes=[pltpu.SemaphoreType.DMA],
        compiler_params=_params(("arbitrary",), 32),
        name="dispatch",
    )(dest_flat, h2)


def _expert_kernel(tile_ref, exp_ref, lo_ref, hi_ref, first_ref,
                   x_hbm, wg_ref, bg_ref, wu_ref, bu_ref, wd_ref, bd_ref, o_ref, xs_scr, x_stage, x_sem):
    w = pl.program_id(0)
    j = pl.program_id(1)
    lo = lo_ref[w]
    hi = hi_ref[w]
    sub = EXPERT_SUB
    tm = o_ref.shape[0]
    n_tiles = x_hbm.shape[0] // tm

    def tile_copy(t):
        return pltpu.make_async_copy(x_hbm.at[pl.ds(pl.multiple_of(t * tm, tm), tm)], x_stage, x_sem)

    @pl.when((j == 0) & (first_ref[w] == 1))
    def _():
        t = tile_ref[w]

        @pl.when(w == 0)
        def _():
            tile_copy(t).start()

        tile_copy(t).wait()
        xs_scr[...] = x_stage[...].astype(BF16)

        @pl.when(t + 1 < n_tiles)
        def _():
            tile_copy(t + 1).start()

        o_ref[...] = jnp.zeros_like(o_ref)

    def ffn_tile(s, n=1):
        rows = slice(s * sub, (s + n) * sub)
        xs = xs_scr[rows, :]
        gate = jnp.dot(xs, wg_ref[0].astype(BF16), preferred_element_type=F32) + bg_ref[0]
        up = jnp.dot(xs, wu_ref[0].astype(BF16), preferred_element_type=F32) + bu_ref[0]
        gate = jnp.minimum(gate, SWIGLU_LIMIT)
        up = jnp.clip(up, -SWIGLU_LIMIT, SWIGLU_LIMIT)
        act = gate * _sigmoid(SWIGLU_ALPHA * gate) * (up + 1.0)
        row = lax.broadcasted_iota(jnp.int32, (n * sub, 1), 0) + s * sub
        mine = (row >= lo) & (row < hi)
        act = jnp.where(mine, act, 0.0).astype(BF16)
        y = jnp.dot(act, wd_ref[0].astype(BF16), preferred_element_type=F32)
        o_ref[rows, :] += y + jnp.where(mine & (j == 0), bd_ref[0], 0.0)

    n_sub = tm // sub
    active = [(lo < (s + 1) * sub) & (hi > s * sub) for s in range(n_sub)]
    whole = functools.reduce(jnp.logical_and, active)

    @pl.when(whole)
    def _():
        ffn_tile(0, n_sub)

    partial = jnp.logical_not(whole)
    for p in range(n_sub // 2):
        s0, s1 = 2 * p, 2 * p + 1

        @pl.when(partial & active[s0] & active[s1])
        def _(s0=s0):
            ffn_tile(s0, 2)

        @pl.when(partial & active[s0] & jnp.logical_not(active[s1]))
        def _(s0=s0):
            ffn_tile(s0)

        @pl.when(partial & active[s1] & jnp.logical_not(active[s0]))
        def _(s1=s1):
            ffn_tile(s1)


def _experts(xb, items, w_gate, b_gate, w_up, b_up, w_down, b_down):
    r = xb.shape[0]
    e, d, f = w_gate.shape
    tm, tf = EXPERT_TM, EXPERT_TF
    nj = f // tf
    tile, expert, lo, hi, first = items
    n_items = tile.shape[0]

    def jj(j, w, hi_ref, lo_ref):
        return jnp.where(hi_ref[w] > lo_ref[w], j, nj - 1)

    return pl.pallas_call(
        _expert_kernel,
        out_shape=jax.ShapeDtypeStruct((r, d), F32),
        grid_spec=pltpu.PrefetchScalarGridSpec(
            num_scalar_prefetch=5,
            grid=(n_items, nj),
            in_specs=[
                pl.BlockSpec(memory_space=pl.ANY),
                pl.BlockSpec((1, d, tf), lambda w, j, ti, ex, lo_, hi_, fi: (ex[w], 0, jj(j, w, hi_, lo_))),
                pl.BlockSpec((1, 1, tf), lambda w, j, ti, ex, lo_, hi_, fi: (ex[w], 0, jj(j, w, hi_, lo_))),
                pl.BlockSpec((1, d, tf), lambda w, j, ti, ex, lo_, hi_, fi: (ex[w], 0, jj(j, w, hi_, lo_))),
                pl.BlockSpec((1, 1, tf), lambda w, j, ti, ex, lo_, hi_, fi: (ex[w], 0, jj(j, w, hi_, lo_))),
                pl.BlockSpec((1, tf, d), lambda w, j, ti, ex, lo_, hi_, fi: (ex[w], jj(j, w, hi_, lo_), 0)),
                pl.BlockSpec((1, 1, d), lambda w, j, ti, ex, lo_, hi_, fi: (ex[w], 0, 0)),
            ],
            out_specs=pl.BlockSpec((tm, d), lambda w, j, ti, ex, lo_, hi_, fi: (ti[w], 0)),
            scratch_shapes=[pltpu.VMEM((tm, d), BF16), pltpu.VMEM((tm, d), F32), pltpu.SemaphoreType.DMA],
        ),
        compiler_params=_params(("arbitrary", "arbitrary"), 60),
        name="experts",
    )(tile, expert, lo, hi, first, xb, w_gate, b_gate.reshape(e, 1, f), w_up, b_up.reshape(e, 1, f),
      w_down, b_down.reshape(e, 1, d))


def _work_items(counts, n_rows):
    tm = EXPERT_TM
    n_tiles = n_rows // tm
    n_items = n_tiles + N_EXPERTS - 1
    def count_le(table, x):
        return jnp.sum((table[None, :] <= x[:, None]).astype(jnp.int32), axis=1)

    def lookup(table, idx):
        hit = idx[:, None] == jnp.arange(table.shape[0], dtype=jnp.int32)[None, :]
        return jnp.sum(jnp.where(hit, table[None, :], 0), axis=1)

    counts = counts.astype(jnp.int32)
    cum = jnp.cumsum(counts)
    start = cum - counts
    tile_lo = jnp.arange(n_tiles, dtype=jnp.int32) * tm
    e_lo = count_le(cum, tile_lo)
    e_hi = count_le(cum, tile_lo + tm - 1)
    per_tile = e_hi - e_lo + 1
    off = jnp.cumsum(per_tile) - per_tile
    total = jnp.sum(per_tile)
    w = jnp.arange(n_items, dtype=jnp.int32)
    valid = w < total
    tile = jnp.clip(count_le(off, w) - 1, 0, n_tiles - 1)
    off_t = lookup(off, tile)
    expert = jnp.where(valid, lookup(e_lo, tile) + w - off_t, e_hi[n_tiles - 1])
    tile = jnp.where(valid, tile, n_tiles - 1)
    lo = jnp.clip(lookup(start, expert) - tile * tm, 0, tm)
    hi = jnp.clip(lookup(cum, expert) - tile * tm, 0, tm)
    hi = jnp.where(valid, jnp.maximum(hi, lo), lo)
    first = (valid & (w == off_t)).astype(jnp.int32)
    return tile, expert.astype(jnp.int32), lo.astype(jnp.int32), hi.astype(jnp.int32), first, start


def _combine_kernel(dest_ref, dnext_ref, dnext2_ref, yb_ref, w_ref, x1_ref, g2_ref, fg_ref, o_ref, buf, sem):
    i = pl.program_id(0)
    n = pl.num_programs(0)
    tm = x1_ref.shape[0]
    slot = i % COMBINE_SLOTS

    def gather(d_ref, into):
        for t in range(tm):
            for k in range(TOP_K):
                pltpu.make_async_copy(yb_ref.at[pl.ds(d_ref[TOP_K * t + k], 1)], buf.at[into, k, pl.ds(t, 1)],
                                      sem.at[into]).start(priority=k % 2)

    def reduce(src):
        wts = w_ref[...]
        moe = buf[src, 0] * wts[:, 0:1]
        for k in range(1, TOP_K):
            moe += buf[src, k] * wts[:, k:k + 1]
        x2 = x1_ref[...] + g2_ref[0] * moe
        ms = jnp.mean(x2 * x2, axis=-1, keepdims=True)
        o_ref[...] = x2 * lax.rsqrt(ms + NORM_EPS) * fg_ref[...]

    @pl.when(i == 0)
    def _():
        gather(dest_ref, 0)
        gather(dnext_ref, 1)

    for k in range(TOP_K):
        pltpu.make_async_copy(yb_ref.at[pl.ds(0, tm)], buf.at[slot, k], sem.at[slot]).wait()

    for v in range(COMBINE_SLOTS):
        @pl.when((slot == v) & (i + 2 < n))
        def _(v=v):
            gather(dnext2_ref, (v + 2) % COMBINE_SLOTS)
            reduce(v)

    @pl.when(i + 2 >= n)
    def _():
        reduce(slot)


def _combine(yb, dest_flat, top_w, x1, g2, final_g):
    b, s, d = x1.shape
    tm = COMBINE_TM
    spt = s // tm
    n = b * spt
    assert n >= COMBINE_SLOTS
    out = pl.pallas_call(
        _combine_kernel,
        out_shape=jax.ShapeDtypeStruct((b * s, d), F32),
        grid=(n,),
        in_specs=[
            pl.BlockSpec((tm * TOP_K,), lambda i: (i,), memory_space=pltpu.SMEM),
            pl.BlockSpec((tm * TOP_K,), lambda i: (jnp.minimum(i + 1, n - 1),), memory_space=pltpu.SMEM),
            pl.BlockSpec((tm * TOP_K,), lambda i: (jnp.minimum(i + 2, n - 1),), memory_space=pltpu.SMEM),
            pl.BlockSpec(memory_space=pl.ANY),
            pl.BlockSpec((tm, LANES), lambda i: (i, 0)),
            pl.BlockSpec((tm, d), lambda i: (i, 0)),
            pl.BlockSpec((1, 1, d), lambda i: (i // spt, 0, 0)),
            pl.BlockSpec((1, d), lambda i: (0, 0)),
        ],
        out_specs=pl.BlockSpec((tm, d), lambda i: (i, 0)),
        scratch_shapes=[pltpu.VMEM((COMBINE_SLOTS, TOP_K, tm, d), F32), pltpu.SemaphoreType.DMA((COMBINE_SLOTS,))],
        compiler_params=_params(("arbitrary",), 56),
        name="combine",
    )(dest_flat, dest_flat, dest_flat, yb, top_w, x1.reshape(b * s, d), g2, final_g)
    return out.reshape(b, s, d)


def kernel(x, c, ctx, c_ctx, ada_w, ada_b, norm1_g, norm2_g, w_in, w_pa, w_pb, w_o, ret_decay_fwd, ret_decay_bwd,
           na_rpb, w_router, b_router, w_gate, b_gate, w_up, b_up, w_down, b_down, final_g):
    assert ada_w.shape[0] == 1, "single layer"
    b, s, d = x.shape
    l = ctx.shape[1]
    in_w = w_in.shape[2]
    rows = s // GRID_W
    assert s % IN_TM == 0 and l % RET_CHUNK == 0 and rows >= NA_WROWS and rows % NA_QROWS == 0
    assert s % OPROJ_TM == 0 and s % COMBINE_TM == 0
    assert (b * s) % max(MERGE_TM, ROUTE_TM, DISPATCH_TM) == 0 and (b * s * TOP_K) % EXPERT_TM == 0

    c_rows = jnp.zeros((16, d), F32).at[:b].set(c).at[b].set(c_ctx)
    mod = _ada(c_rows, ada_w[0], ada_b[0][None, :])
    sh1, sc1, g1, sh2, sc2, g2 = [mod[:b, None, i * d:(i + 1) * d] for i in range(6)]
    shc1 = mod[b, 0 * d:1 * d][None, None, :]
    scc1 = mod[b, 1 * d:2 * d][None, None, :]

    w_in0 = w_in[0]
    n1 = norm1_g[0][None, :]
    proj = _inproj(x, n1, sh1, sc1, w_in0, tuple(range(in_w // IN_TN)), IN_TM)
    qk_w = RET_HEADS * RET_DK
    v_w = RET_HEADS * RET_DV
    na_w = NA_HEADS * NA_DH
    ctx_cols = tuple(range(qk_w // IN_TN, (2 * qk_w + v_w) // IN_TN)) + tuple(
        range((2 * qk_w + 2 * v_w + na_w) // IN_TN, (2 * qk_w + 2 * v_w + 3 * na_w) // IN_TN))
    projc = _inproj(ctx.reshape(1, b * l, d), n1, shc1, scc1, w_in0, ctx_cols, b * l).reshape(b, l, -1)

    lgf = jax.nn.log_sigmoid(ret_decay_fwd[0].astype(F32))
    lgb = jax.nn.log_sigmoid(ret_decay_bwd[0].astype(F32))
    cos_t, sin_t = _rope_tables(s)
    ret_in = _retention(proj, projc, lgf, lgb, cos_t, sin_t)
    na_in = _natt(proj, projc, _natt_pair_tiles(na_rpb[0]))

    t = b * s
    m = _merge(ret_in.reshape(t, -1), na_in.reshape(t, -1), proj.reshape(t, in_w),
               w_pa[0], w_pb[0])

    w_r = jnp.zeros((d, LANES), F32).at[:, :N_EXPERTS].set(w_router[0])
    b_r = jnp.full((1, LANES), NEG_INF, F32).at[0, :N_EXPERTS].set(b_router[0])
    x1, h2, logits = _oproj(m, x, g1, norm2_g[0][None, :], sh2, sc2, w_o[0].astype(BF16), w_r, b_r)

    top_e, top_w, rank, counts = _route(logits)
    counts = counts[0, :N_EXPERTS]
    items = _work_items(counts, t * TOP_K)
    start = items[5]
    is_e = top_e[:, :, None] == jnp.arange(N_EXPERTS, dtype=jnp.int32)
    dest = (jnp.sum(jnp.where(is_e, start.astype(jnp.int32), 0), axis=-1) + rank).reshape(-1)

    xb = _dispatch(h2, dest)
    yb = _experts(xb, items[:5], w_gate[0], b_gate[0], w_up[0], b_up[0], w_down[0], b_down[0])
    return _combine(yb, dest, top_w, x1, g2, final_g[None, :])
```
